```python
import math
import jax, jax.numpy as jnp
from jax import lax
import numpy as np

D_MODEL = 2048
BATCH = 32
SEQ = 256
DEPTH = 2
DEC_BATCH = 8
DEC_SEQ = 4096
PAST_LEN = 256

GRID_W = 64
D_FF = 4 * D_MODEL
N_MOD = 6
NORM_EPS = 1e-6

GLA_HEADS = 4
GLA_DK = 128
GLA_DV = 256
GLA_KEY_W = GLA_HEADS * GLA_DK
GLA_VAL_W = GLA_HEADS * GLA_DV
GLA_LOWRANK = 16
GLA_TAU = 16.0
GLA_CHUNK = 64
ROPE_BASE = 10000.0

NA_HEADS = 8
NA_HD = 128
NA_W = NA_HEADS * NA_HD
NA_KH = 8
NA_KW = 16
Q_BLOCK = 128

HY_W = 1024
HY_ORDER = 2
HY_SHORT = 3
HY_BANDS = 16
HY_POS_DIM = 1 + 2 * HY_BANDS
HY_HIDDEN = 64
HY_DECAY_TARGET = 1e-2
HY_DECAY_PCT_SHORT = 0.3
HY_DECAY_PCT_LONG = 1.5

N_BRANCH = 3
IN_SIZES = (GLA_KEY_W, GLA_KEY_W, GLA_VAL_W, GLA_VAL_W, 2 * GLA_LOWRANK,
            NA_W, NA_W, NA_W, (HY_ORDER + 1) * HY_W, N_BRANCH * D_MODEL)
IN_TOTAL = 2 * GLA_KEY_W + 2 * GLA_VAL_W + 2 * GLA_LOWRANK + 3 * NA_W + (HY_ORDER + 1) * HY_W + N_BRANCH * D_MODEL

kernel_name = 'hybrid_gla_natten_hyena_prefix_dit_step'

F32 = jnp.float32


def in_offsets():
    return [int(o) for o in np.cumsum(IN_SIZES)[:-1]]


def rms_norm(x, g):
    xf = x.astype(F32)
    y = xf * lax.rsqrt(jnp.mean(xf * xf, axis=-1, keepdims=True) + NORM_EPS)
    return (y * g.astype(F32)).astype(x.dtype)


def adaln(cond, w_mod, b_mod):
    m = jax.nn.silu(cond) @ w_mod + b_mod
    return jnp.split(m[:, None, :], N_MOD, axis=-1)


def to_heads(t, n_heads):
    B, L, W = t.shape
    return t.reshape(B, L, n_heads, W // n_heads).transpose(0, 2, 1, 3)


def from_heads(t):
    B, H, L, d = t.shape
    return t.transpose(0, 2, 1, 3).reshape(B, L, H * d)


def axial_rope(t):
    L, d = t.shape[2], t.shape[3]
    n_freq = d // 4
    pos = jnp.arange(L)
    row = (pos // GRID_W).astype(F32)
    col = (pos % GRID_W).astype(F32)
    inv = ROPE_BASE ** (-jnp.arange(n_freq, dtype=F32) / n_freq)
    ang = jnp.concatenate([row[:, None] * inv, col[:, None] * inv], axis=-1)
    cos, sin = jnp.cos(ang), jnp.sin(ang)
    t1, t2 = t[..., : d // 2], t[..., d // 2:]
    return jnp.concatenate([t1 * cos - t2 * sin, t1 * sin + t2 * cos], axis=-1)


def gla_scan(q, k, v, log_a, s0):
    B, H, L, _ = q.shape
    C = GLA_CHUNK
    n = L // C

    def to_chunks(t):
        return jnp.moveaxis(t.reshape(B, H, n, C, t.shape[-1]), 2, 0)

    causal = jnp.tril(jnp.ones((C, C), dtype=bool))[None, None, :, :, None]

    def step(s, inp):
        qi, ki, vi, ai = inp
        b = jnp.cumsum(ai, axis=2)
        o_inter = jnp.einsum('bhtd,bhde->bhte', qi * jnp.exp(b), s)
        diff = b[:, :, :, None, :] - b[:, :, None, :, :]
        decay = jnp.exp(jnp.where(causal, diff, -jnp.inf))
        attn = jnp.einsum('bhtd,bhsd,bhtsd->bhts', qi, ki, decay)
        o = o_inter + jnp.einsum('bhts,bhse->bhte', attn, vi)
        b_last = b[:, :, -1]
        s_new = jnp.exp(b_last)[..., None] * s + jnp.einsum(
            'bhsd,bhse->bhde', ki * jnp.exp(b_last[:, :, None, :] - b), vi)
        return s_new, o

    s_fin, o = lax.scan(step, s0, (to_chunks(q), to_chunks(k), to_chunks(v), to_chunks(log_a)))
    o = jnp.moveaxis(o, 0, 2).reshape(B, H, L, v.shape[-1])
    return o, s_fin


def gla_bidirectional(q, k, v, la_f, la_b, s0_f, s0_b):
    o_f, s_f = gla_scan(q, k, v, la_f, s0_f)
    rev = lambda t: jnp.flip(t, axis=2)
    o_b, s_b = gla_scan(rev(q), rev(k), rev(v), rev(la_b), s0_b)
    return o_f + rev(o_b), s_f, s_b


def dense_attention(q, k, v):
    B, H, L, hd = q.shape
    scale = hd ** -0.5
    qb = jnp.moveaxis(q.reshape(B, H, L // Q_BLOCK, Q_BLOCK, hd), 2, 0)

    def block(qi):
        s = jnp.einsum('bhqd,bhkd->bhqk', qi, k).astype(F32) * scale
        p = jax.nn.softmax(s, axis=-1).astype(v.dtype)
        return jnp.einsum('bhqk,bhkd->bhqd', p, v)

    o = lax.map(block, qb)
    return jnp.moveaxis(o, 0, 2).reshape(B, H, L, hd)


def neighbourhood_attention(q, k, v, k_ctx, v_ctx, rpb):
    B, H, L, hd = q.shape
    rows = L // GRID_W
    kh, kw = min(NA_KH, rows), NA_KW
    scale = hd ** -0.5
    qg = q.reshape(B, H, rows, GRID_W, hd)
    kg = k.reshape(B, H, rows, GRID_W, hd)
    vg = v.reshape(B, H, rows, GRID_W, hd)
    col = jnp.arange(GRID_W)
    c_start = jnp.clip(col - kw // 2, 0, GRID_W - kw)
    col_in = (col[None, :] >= c_start[:, None]) & (col[None, :] < c_start[:, None] + kw)
    col_idx = jnp.clip(col[None, :] - col[:, None] + kw - 1, 0, 2 * kw - 2)
    rpb_cols = rpb[:, :, col_idx]
    mask = col_in[None, None, :, None, :]

    def row_block(r):
        r_start = jnp.clip(r - kh // 2, 0, rows - kh)
        q_r = lax.dynamic_index_in_dim(qg, r, axis=2, keepdims=False)
        k_b = lax.dynamic_slice_in_dim(kg, r_start, kh, axis=2)
        v_b = lax.dynamic_slice_in_dim(vg, r_start, kh, axis=2)
        row_off = r_start + jnp.arange(kh) - r + (NA_KH - 1)
        bias = jnp.transpose(jnp.take(rpb_cols, row_off, axis=1), (0, 2, 1, 3))
        s_loc = jnp.einsum('bhqd,bhrkd->bhqrk', q_r, k_b).astype(F32) * scale + bias[None].astype(F32)
        s_loc = jnp.where(mask, s_loc, -jnp.inf)
        s_ctx = jnp.einsum('bhqd,bhcd->bhqc', q_r, k_ctx).astype(F32) * scale
        s = jnp.concatenate([s_loc.reshape(B, H, GRID_W, kh * GRID_W), s_ctx], axis=-1)
        p = jax.nn.softmax(s, axis=-1).astype(v.dtype)
        p_loc = p[..., : kh * GRID_W].reshape(B, H, GRID_W, kh, GRID_W)
        p_ctx = p[..., kh * GRID_W:]
        return (jnp.einsum('bhqrk,bhrkd->bhqd', p_loc, v_b)
                + jnp.einsum('bhqc,bhcd->bhqd', p_ctx, v_ctx))

    out = lax.map(row_block, jnp.arange(rows))
    return jnp.moveaxis(out, 0, 2).reshape(B, H, L, hd)


def hyena_filters(L, f1_w, f1_b, f2_w, f2_b, f3_w, freq):
    t = jnp.arange(L, dtype=F32)
    tn = t / L
    bands = jnp.arange(1, HY_BANDS + 1, dtype=F32)
    ang = 2.0 * jnp.pi * tn[:, None] * bands[None, :]
    z = jnp.concatenate([tn[:, None], jnp.cos(ang), jnp.sin(ang)], axis=-1)
    fr = freq.astype(F32)
    h = jnp.sin(fr * (z @ f1_w.astype(F32) + f1_b.astype(F32)))
    h = jnp.sin(fr * (h @ f2_w.astype(F32) + f2_b.astype(F32)))
    h = (h @ f3_w.astype(F32)).reshape(L, HY_ORDER, HY_W)
    dist = jnp.abs(t - L // 2) * (2.0 / L)
    deltas = jnp.abs(jnp.linspace(math.log(HY_DECAY_TARGET) / HY_DECAY_PCT_LONG,
                                  math.log(HY_DECAY_TARGET) / HY_DECAY_PCT_SHORT, HY_W, dtype=F32))
    h = h * jnp.exp(-dist[:, None, None] * deltas[None, None, :])
    return h * lax.rsqrt(jnp.sum(h * h, axis=0, keepdims=True) + NORM_EPS)


def hyena_operator(u, conv_w, conv_b, filters, bias):
    B, L, _ = u.shape
    pad = HY_SHORT // 2
    up = jnp.pad(u, ((0, 0), (pad, pad), (0, 0)))
    uc = conv_b + sum(up[:, j:j + L] * conv_w[j] for j in range(HY_SHORT))
    parts = jnp.split(uc.astype(F32), HY_ORDER + 1, axis=-1)
    gates, z = parts[:HY_ORDER], parts[HY_ORDER]
    h_f = jnp.fft.rfft(filters, n=2 * L, axis=0)
    for o in range(HY_ORDER):
        z_f = jnp.fft.rfft(z, n=2 * L, axis=1)
        conv = jnp.fft.irfft(z_f * h_f[None, :, o], n=2 * L, axis=1)[:, L // 2: L // 2 + L]
        z = gates[o] * (conv + bias[o].astype(F32) * z)
    return z


def token_mixing(x, shift, scale, p, latent_ctx):
    B, L, _ = x.shape
    h = rms_norm(x, p['norm1_g']) * (1 + scale) + shift
    gq, gk, gv, gg, ga, nq, nk, nv, hy, gates = jnp.split(h @ p['w_in'], in_offsets(), axis=-1)

    q = to_heads(gq, GLA_HEADS).astype(F32) * GLA_DK ** -0.5
    k = to_heads(gk, GLA_HEADS).astype(F32)
    v = to_heads(gv, GLA_HEADS).astype(F32)
    la = jnp.einsum('blir,ird->blid', ga.reshape(B, L, 2, GLA_LOWRANK), p['gla_wa2']) + p['gla_ba2']
    la = jax.nn.log_sigmoid(la.astype(F32)) / GLA_TAU
    la_f, la_b = to_heads(la[:, :, 0], GLA_HEADS), to_heads(la[:, :, 1], GLA_HEADS)
    if latent_ctx is None:
        s0_f = s0_b = jnp.zeros((B, GLA_HEADS, GLA_DK, GLA_DV), F32)
    else:
        s0_f, s0_b, k_ctx, v_ctx = latent_ctx
        q, k = axial_rope(q), axial_rope(k)
    o_a, s_f, s_b = gla_bidirectional(q, k, v, la_f, la_b, s0_f.astype(F32), s0_b.astype(F32))
    out_a = from_heads(rms_norm(o_a, p['gla_norm_g'])).astype(x.dtype) * jax.nn.silu(gg)

    qn = rms_norm(to_heads(nq, NA_HEADS), p['na_qnorm_g'])
    kn = rms_norm(to_heads(nk, NA_HEADS), p['na_knorm_g'])
    vn = to_heads(nv, NA_HEADS)
    if latent_ctx is None:
        o_b = dense_attention(qn, kn, vn)
    else:
        o_b = neighbourhood_attention(qn, kn, vn, k_ctx.astype(x.dtype), v_ctx.astype(x.dtype), p['na_rpb'])
    out_b = from_heads(o_b)

    filters = hyena_filters(L, p['hy_f1_w'], p['hy_f1_b'], p['hy_f2_w'], p['hy_f2_b'], p['hy_f3_w'], p['hy_freq'])
    out_c = hyena_operator(hy, p['hy_conv_w'], p['hy_conv_b'], filters, p['hy_bias']).astype(x.dtype)

    g_a, g_b, g_c = jnp.split(jax.nn.sigmoid(gates), N_BRANCH, axis=-1)
    merged = g_a * (out_a @ p['w_br_a']) + g_b * (out_b @ p['w_br_b']) + g_c * (out_c @ p['w_br_c'])
    y = merged @ p['w_out']
    new_ctx = (jnp.stack([s_f, s_b], axis=1), kn, vn) if latent_ctx is None else None
    return y, new_ctx


def trunk_layer(x, cond, p, latent_ctx):
    sh1, sc1, g1, sh2, sc2, g2 = adaln(cond, p['w_mod'], p['b_mod'])
    y, new_ctx = token_mixing(x, sh1, sc1, p, latent_ctx)
    x = x + g1 * y
    h = rms_norm(x, p['norm2_g']) * (1 + sc2) + sh2
    x = x + g2 * (jnp.square(jax.nn.relu(h @ p['w_mlp1'])) @ p['w_mlp2'])
    return x, new_ctx


def setup_inputs(seed: int = 0) -> dict:
    key = jax.random.key(seed)
    ks = iter(jax.random.split(key, 40))

    def nrm(shape, scale):
        return jax.random.normal(next(ks), shape, F32) * scale

    def gain(shape):
        return 1.0 + nrm(shape, 0.02)

    return {
        'x_prompt': nrm((BATCH, SEQ, D_MODEL), 1.0),
        'x_sample': nrm((DEC_BATCH, DEC_SEQ, D_MODEL), 1.0),
        'state_gla': nrm((DEC_BATCH, DEPTH, 2, GLA_HEADS, GLA_DK, GLA_DV), 0.5),
        'cache_na_k': nrm((DEC_BATCH, DEPTH, NA_HEADS, PAST_LEN, NA_HD), 1.0),
        'cache_na_v': nrm((DEC_BATCH, DEPTH, NA_HEADS, PAST_LEN, NA_HD), 1.0),
        'c': nrm((DEC_BATCH, D_MODEL), 1.0),
        'c_ctx': nrm((D_MODEL,), 1.0),
        'w_mod': nrm((DEPTH, D_MODEL, N_MOD * D_MODEL), D_MODEL ** -0.5),
        'b_mod': nrm((DEPTH, N_MOD * D_MODEL), 0.02),
        'norm1_g': gain((DEPTH, D_MODEL)),
        'norm2_g': gain((DEPTH, D_MODEL)),
        'w_in': nrm((DEPTH, D_MODEL, IN_TOTAL), D_MODEL ** -0.5),
        'gla_wa2': nrm((DEPTH, 2, GLA_LOWRANK, GLA_KEY_W), GLA_LOWRANK ** -0.5),
        'gla_ba2': nrm((DEPTH, 2, GLA_KEY_W), 0.02),
        'gla_norm_g': gain((DEPTH, GLA_DV)),
        'na_qnorm_g': gain((DEPTH, NA_HD)),
        'na_knorm_g': gain((DEPTH, NA_HD)),
        'na_rpb': nrm((DEPTH, NA_HEADS, 2 * NA_KH - 1, 2 * NA_KW - 1), 0.1),
        'hy_conv_w': nrm((DEPTH, HY_SHORT, (HY_ORDER + 1) * HY_W), HY_SHORT ** -0.5),
        'hy_conv_b': nrm((DEPTH, (HY_ORDER + 1) * HY_W), 0.02),
        'hy_f1_w': nrm((DEPTH, HY_POS_DIM, HY_HIDDEN), HY_POS_DIM ** -0.5),
        'hy_f1_b': nrm((DEPTH, HY_HIDDEN), 0.02),
        'hy_f2_w': nrm((DEPTH, HY_HIDDEN, HY_HIDDEN), HY_HIDDEN ** -0.5),
        'hy_f2_b': nrm((DEPTH, HY_HIDDEN), 0.02),
        'hy_f3_w': nrm((DEPTH, HY_HIDDEN, HY_ORDER * HY_W), HY_HIDDEN ** -0.5),
        'hy_freq': gain((DEPTH, HY_HIDDEN)),
        'hy_bias': nrm((DEPTH, HY_ORDER, HY_W), 0.1),
        'w_br_a': nrm((DEPTH, GLA_VAL_W, D_MODEL), GLA_VAL_W ** -0.5),
        'w_br_b': nrm((DEPTH, NA_W, D_MODEL), NA_W ** -0.5),
        'w_br_c': nrm((DEPTH, HY_W, D_MODEL), HY_W ** -0.5),
        'w_out': nrm((DEPTH, D_MODEL, D_MODEL), D_MODEL ** -0.5),
        'w_mlp1': nrm((DEPTH, D_MODEL, D_FF), D_MODEL ** -0.5),
        'w_mlp2': nrm((DEPTH, D_FF, D_MODEL), D_FF ** -0.5),
    }


def reference(x_prompt, x_sample, state_gla, cache_na_k, cache_na_v, c, c_ctx,
              w_mod, b_mod, norm1_g, norm2_g, w_in, gla_wa2, gla_ba2, gla_norm_g,
              na_qnorm_g, na_knorm_g, na_rpb, hy_conv_w, hy_conv_b, hy_f1_w, hy_f1_b,
              hy_f2_w, hy_f2_b, hy_f3_w, hy_freq, hy_bias, w_br_a, w_br_b, w_br_c,
              w_out, w_mlp1, w_mlp2):
    xp, xs = x_prompt, x_sample
    ctx_cond = c_ctx[None, :]
    new_gla, new_k, new_v = [], [], []
    for l in range(DEPTH):
        p = dict(w_mod=w_mod[l], b_mod=b_mod[l], norm1_g=norm1_g[l], norm2_g=norm2_g[l],
                 w_in=w_in[l], gla_wa2=gla_wa2[l], gla_ba2=gla_ba2[l], gla_norm_g=gla_norm_g[l],
                 na_qnorm_g=na_qnorm_g[l], na_knorm_g=na_knorm_g[l], na_rpb=na_rpb[l],
                 hy_conv_w=hy_conv_w[l], hy_conv_b=hy_conv_b[l], hy_f1_w=hy_f1_w[l],
                 hy_f1_b=hy_f1_b[l], hy_f2_w=hy_f2_w[l], hy_f2_b=hy_f2_b[l],
                 hy_f3_w=hy_f3_w[l], hy_freq=hy_freq[l], hy_bias=hy_bias[l],
                 w_br_a=w_br_a[l], w_br_b=w_br_b[l], w_br_c=w_br_c[l], w_out=w_out[l],
                 w_mlp1=w_mlp1[l], w_mlp2=w_mlp2[l])
        xp, (s_gla, k_ctx, v_ctx) = trunk_layer(xp, ctx_cond, p, None)
        new_gla.append(s_gla)
        new_k.append(k_ctx)
        new_v.append(v_ctx)
        latent_ctx = (state_gla[:, l, 0], state_gla[:, l, 1], cache_na_k[:, l], cache_na_v[:, l])
        xs, _ = trunk_layer(xs, c, p, latent_ctx)
    new_state_gla = jnp.stack(new_gla, axis=1)
    new_cache_na_k = jnp.stack(new_k, axis=1)
    new_cache_na_v = jnp.stack(new_v, axis=1)
    return (xp, xs, new_state_gla, new_cache_na_k, new_cache_na_v)
```

```python
import functools
import math

import jax
import jax.numpy as jnp
from jax import lax
from jax.experimental import pallas as pl
from jax.experimental.pallas import tpu as pltpu

F32 = jnp.float32
BF16 = jnp.bfloat16
HIGHEST = lax.Precision.HIGHEST

GRID_W = 64
N_MOD = 6
NORM_EPS = 1e-6
GLA_HEADS = 4
GLA_DK = 128
GLA_DV = 256
GLA_LOWRANK = 16
GLA_TAU = 16.0
GLA_CHUNK = 64
GLA_SUB = 16
ROPE_BASE = 10000.0
NA_HEADS = 8
NA_HD = 128
NA_KH = 8
NA_KW = 16
HY_W = 1024
HY_ORDER = 2
HY_BANDS = 16
HY_HIDDEN = 64
HY_DECAY_TARGET = 1e-2
HY_DECAY_PCT_SHORT = 0.3
HY_DECAY_PCT_LONG = 1.5

V7X_VMEM_LIMIT_BYTES = 56 * 1024 * 1024
LANES = 128

KEY_W = GLA_HEADS * GLA_DK
VAL_W = GLA_HEADS * GLA_DV
NA_W = NA_HEADS * NA_HD
OFF_GQ = 0
OFF_GK = OFF_GQ + KEY_W
OFF_GV = OFF_GK + KEY_W
OFF_GG = OFF_GV + VAL_W
OFF_NQ = OFF_GG + VAL_W
OFF_NK = OFF_NQ + NA_W
OFF_NV = OFF_NK + NA_W
OFF_HY = OFF_NV + NA_W
OFF_GATES = OFF_HY + (HY_ORDER + 1) * HY_W


def _cparams(*sem):
    return pltpu.CompilerParams(dimension_semantics=sem, vmem_limit_bytes=V7X_VMEM_LIMIT_BYTES)


def _modulated_norm(x, g, scale, shift):
    ms = jnp.mean(x * x, axis=-1, keepdims=True)
    y = x * lax.rsqrt(ms + NORM_EPS) * g
    return y * (1.0 + scale) + shift


def _adaln_kernel(c_ref, w_ref, b_ref, o_ref):
    c = c_ref[...]
    a = (c * jax.nn.sigmoid(c)).astype(BF16)
    o_ref[...] = jnp.dot(a, w_ref[...].astype(BF16), preferred_element_type=F32) + b_ref[...]


def adaln(cond, w_mod, b_mod):
    m, d = cond.shape
    n = w_mod.shape[1]
    tn = 1024
    return pl.pallas_call(
        _adaln_kernel,
        grid=(n // tn,),
        in_specs=[pl.BlockSpec((m, d), lambda j: (0, 0)),
                  pl.BlockSpec((d, tn), lambda j: (0, j)),
                  pl.BlockSpec((1, tn), lambda j: (0, j))],
        out_specs=pl.BlockSpec((m, tn), lambda j: (0, j)),
        out_shape=jax.ShapeDtypeStruct((m, n), F32),
        compiler_params=_cparams("parallel"),
        name="adaln",
    )(cond, w_mod, b_mod)


def _norm_matmul_kernel(x_ref, g_ref, sc_ref, sh_ref, w_ref, o_ref, h_ref, *, relu2):
    @pl.when(pl.program_id(1) == 0)
    def _():
        h = _modulated_norm(x_ref[...], g_ref[...], sc_ref[0], sh_ref[0])
        h_ref[...] = h.astype(BF16)

    acc = jnp.dot(h_ref[...], w_ref[...], preferred_element_type=F32)
    if relu2:
        acc = jnp.square(jnp.maximum(acc, 0.0))
    o_ref[...] = acc.astype(o_ref.dtype)


def norm_matmul(x, g, scale, shift, w, *, rows_per_mod, tm, tn, relu2, out_dtype):
    t, d = x.shape
    n = w.shape[1]
    mod_map = lambda i, j: ((i * tm) // rows_per_mod, 0, 0)
    return pl.pallas_call(
        functools.partial(_norm_matmul_kernel, relu2=relu2),
        grid=(t // tm, n // tn),
        in_specs=[pl.BlockSpec((tm, d), lambda i, j: (i, 0)),
                  pl.BlockSpec((1, d), lambda i, j: (0, 0)),
                  pl.BlockSpec((1, 1, d), mod_map),
                  pl.BlockSpec((1, 1, d), mod_map),
                  pl.BlockSpec((d, tn), lambda i, j: (0, j))],
        out_specs=pl.BlockSpec((tm, tn), lambda i, j: (i, j)),
        out_shape=jax.ShapeDtypeStruct((t, n), out_dtype),
        scratch_shapes=[pltpu.VMEM((tm, d), BF16)],
        compiler_params=_cparams("parallel", "arbitrary"),
        name="norm_matmul_relu2" if relu2 else "norm_matmul",
    )(x, g, scale, shift, w)


def _decay_gate_kernel(x_ref, g_ref, sc_ref, sh_ref, wga_ref, wa2_ref, ba2_ref, o_ref):
    h = _modulated_norm(x_ref[...], g_ref[...], sc_ref[0], sh_ref[0]).astype(BF16)
    ga = jnp.dot(h, wga_ref[...], preferred_element_type=F32)
    pre = jnp.dot(ga, wa2_ref[...], preferred_element_type=F32, precision=HIGHEST) + ba2_ref[...]
    log_sig = jnp.minimum(pre, 0.0) - jnp.log(1.0 + jnp.exp(-jnp.abs(pre)))
    o_ref[...] = log_sig * (1.0 / GLA_TAU)


def decay_gate(x, g, scale, shift, w_ga, wa2_bd, ba2, *, rows_per_mod, tm):
    t, d = x.shape
    n = wa2_bd.shape[1]
    mod_map = lambda i: ((i * tm) // rows_per_mod, 0, 0)
    return pl.pallas_call(
        _decay_gate_kernel,
        grid=(t // tm,),
        in_specs=[pl.BlockSpec((tm, d), lambda i: (i, 0)),
                  pl.BlockSpec((1, d), lambda i: (0, 0)),
                  pl.BlockSpec((1, 1, d), mod_map),
                  pl.BlockSpec((1, 1, d), mod_map),
                  pl.BlockSpec((d, LANES), lambda i: (0, 0)),
                  pl.BlockSpec((LANES, n), lambda i: (0, 0)),
                  pl.BlockSpec((1, n), lambda i: (0, 0))],
        out_specs=pl.BlockSpec((tm, n), lambda i: (i, 0)),
        out_shape=jax.ShapeDtypeStruct((t, n), F32),
        compiler_params=_cparams("parallel"),
        name="decay_gate",
    )(x, g, scale, shift, w_ga, wa2_bd, ba2)


def _gla_chunk(q, k, v, la, st, *, reverse):
    c = GLA_CHUNK
    n = GLA_SUB
    row = lax.broadcasted_iota(jnp.int32, (c, c), 0)
    col = lax.broadcasted_iota(jnp.int32, (c, c), 1)
    tri = (row <= col) if reverse else (row >= col)
    b = jnp.dot(tri.astype(F32), la, preferred_element_type=F32, precision=HIGHEST)
    edge = b[0:1, :] if reverse else b[c - 1:c, :]

    o = lax.dot_general((q * jnp.exp(b)).astype(BF16), st.astype(BF16),
                        (((1,), (1,)), ((), ())), preferred_element_type=F32)
    kd = (k * jnp.exp(edge - b)).astype(BF16)
    st_new = st * jnp.exp(edge) + jnp.dot(v.T.astype(BF16), kd, preferred_element_type=F32)

    lane = lax.broadcasted_iota(jnp.int32, (n, c), 1)
    sub_row = lax.broadcasted_iota(jnp.int32, (n, c), 0)
    blocks = []
    for i in range(c // n):
        lo, hi = i * n, (i + 1) * n
        q_i = q[lo:hi, :]
        b_i = b[lo:hi, :]
        has_off = (hi < c) if reverse else (lo > 0)
        if has_off:
            r = b[hi:hi + 1, :] if reverse else b[lo - 1:lo, :]
            lhs = (q_i * jnp.exp(b_i - r)).astype(BF16)
            rhs = (k * jnp.exp(jnp.minimum(r - b, 0.0))).astype(BF16)
            a_off = lax.dot_general(lhs, rhs, (((1,), (1,)), ((), ())), preferred_element_type=F32)
        else:
            a_off = jnp.zeros((n, c), F32)
        a_diag = jnp.zeros((n, c), F32)
        for j in range(n):
            s = lo + j
            p = q_i * k[s:s + 1, :] * jnp.exp(jnp.minimum(b_i - b[s:s + 1, :], 0.0))
            a_diag = jnp.where(lane == s, jnp.sum(p, axis=1, keepdims=True), a_diag)
        in_off = (lane >= hi) if reverse else (lane < lo)
        a_i = jnp.where(in_off, a_off, a_diag)
        causal = (lane >= sub_row + lo) if reverse else (lane <= sub_row + lo)
        blocks.append(jnp.where(causal, a_i, 0.0))
    a = jnp.concatenate(blocks, axis=0)
    o = o + jnp.dot(a.astype(BF16), v.astype(BF16), preferred_element_type=F32)
    return o, st_new


def _gla_kernel(*refs, reverse, rope, has_s0, nchunk):
    it = iter(refs)
    q_ref, k_ref, v_ref, la_ref = next(it), next(it), next(it), next(it)
    cos_ref = sin_ref = s0_ref = None
    if rope:
        cos_ref, sin_ref = next(it), next(it)
    if has_s0:
        s0_ref = next(it)
    o_ref, sfin_ref, st_ref = next(it), next(it), next(it)

    blk = pl.program_id(2)

    @pl.when(blk == 0)
    def _():
        if has_s0:
            st_ref[...] = s0_ref[0, 0, 0, 0].T
        else:
            st_ref[...] = jnp.zeros(st_ref.shape, F32)

    def body(step, carry):
        ci = (nchunk - 1 - step) if reverse else step
        rows = pl.ds(pl.multiple_of(ci * GLA_CHUNK, GLA_CHUNK), GLA_CHUNK)
        q = q_ref[rows, :] * (GLA_DK ** -0.5)
        k = k_ref[rows, :]
        if rope:
            cs, sn = cos_ref[rows, :], sin_ref[rows, :]
            q = q * cs + pltpu.roll(q, GLA_DK // 2, 1) * sn
            k = k * cs + pltpu.roll(k, GLA_DK // 2, 1) * sn
        o, st_new = _gla_chunk(q, k, v_ref[rows, :], la_ref[rows, :], st_ref[...], reverse=reverse)
        o_ref[rows, :] = o
        st_ref[...] = st_new
        return carry

    lax.fori_loop(0, nchunk, body, 0)

    @pl.when(blk == pl.num_programs(2) - 1)
    def _():
        sfin_ref[0, 0] = st_ref[...].T


def gla_direction(proj, la, rope_tabs, s0, *, batch, seq, reverse, layer, direction, rows_blk):
    t = proj.shape[0]
    nblk = seq // rows_blk
    nchunk = rows_blk // GLA_CHUNK
    rope = rope_tabs is not None
    has_s0 = s0 is not None

    def rb(b, c):
        cc = (nblk - 1 - c) if reverse else c
        return b * nblk + cc

    kq, kv = OFF_GQ // GLA_DK, OFF_GV // GLA_DV
    kk = OFF_GK // GLA_DK
    kla = direction * GLA_HEADS
    in_specs = [pl.BlockSpec((rows_blk, GLA_DK), lambda b, h, c: (rb(b, c), kq + h)),
                pl.BlockSpec((rows_blk, GLA_DK), lambda b, h, c: (rb(b, c), kk + h)),
                pl.BlockSpec((rows_blk, GLA_DV), lambda b, h, c: (rb(b, c), kv + h)),
                pl.BlockSpec((rows_blk, GLA_DK), lambda b, h, c: (rb(b, c), kla + h))]
    args = [proj, proj, proj, la]
    if rope:
        tab_map = lambda b, h, c: ((nblk - 1 - c) if reverse else c, 0)
        in_specs += [pl.BlockSpec((rows_blk, GLA_DK), tab_map)] * 2
        args += list(rope_tabs)
    if has_s0:
        in_specs.append(pl.BlockSpec((1, 1, 1, 1, GLA_DK, GLA_DV), lambda b, h, c: (b, layer, direction, h, 0, 0)))
        args.append(s0)
    o, sfin = pl.pallas_call(
        functools.partial(_gla_kernel, reverse=reverse, rope=rope, has_s0=has_s0, nchunk=nchunk),
        grid=(batch, GLA_HEADS, nblk),
        in_specs=in_specs,
        out_specs=[pl.BlockSpec((rows_blk, GLA_DV), lambda b, h, c: (rb(b, c), h)),
                   pl.BlockSpec((1, 1, GLA_DK, GLA_DV), lambda b, h, c: (b, h, 0, 0))],
        out_shape=[jax.ShapeDtypeStruct((t, VAL_W), F32),
                   jax.ShapeDtypeStruct((batch, GLA_HEADS, GLA_DK, GLA_DV), F32)],
        scratch_shapes=[pltpu.VMEM((GLA_DV, GLA_DK), F32)],
        compiler_params=_cparams("parallel", "parallel", "arbitrary"),
        name="gla_bwd" if reverse else "gla_fwd",
    )(*args)
    return o, sfin


def _head_norm(x, g):
    return x * lax.rsqrt(jnp.mean(x * x, axis=-1, keepdims=True) + NORM_EPS) * g


def _ctx_attn_kernel(q_ref, k_ref, v_ref, gq_ref, gk_ref, o_ref, kn_ref, vn_ref):
    scale = NA_HD ** -0.5
    for h in range(NA_HEADS):
        cols = slice(h * NA_HD, (h + 1) * NA_HD)
        qn = _head_norm(q_ref[:, cols], gq_ref[...]) * scale
        kn = _head_norm(k_ref[:, cols], gk_ref[...])
        v = v_ref[:, cols]
        kn_ref[0, h] = kn
        vn_ref[0, h] = v
        s = lax.dot_general(qn.astype(BF16), kn.astype(BF16), (((1,), (1,)), ((), ())),
                            preferred_element_type=F32)
        e = jnp.exp(s - jnp.max(s, axis=-1, keepdims=True))
        p = e / jnp.sum(e, axis=-1, keepdims=True)
        o_ref[:, cols] = jnp.dot(p.astype(BF16), v.astype(BF16), preferred_element_type=F32)


def ctx_attention(proj, gq, gk, *, batch, seq):
    t = proj.shape[0]
    cq, ck, cv = OFF_NQ // NA_W, OFF_NK // NA_W, OFF_NV // NA_W
    cache_shape = jax.ShapeDtypeStruct((batch, NA_HEADS, seq, NA_HD), F32)
    cache_spec = pl.BlockSpec((1, NA_HEADS, seq, NA_HD), lambda b: (b, 0, 0, 0))
    return pl.pallas_call(
        _ctx_attn_kernel,
        grid=(batch,),
        in_specs=[pl.BlockSpec((seq, NA_W), lambda b: (b, cq)),
                  pl.BlockSpec((seq, NA_W), lambda b: (b, ck)),
                  pl.BlockSpec((seq, NA_W), lambda b: (b, cv)),
                  pl.BlockSpec((1, NA_HD), lambda b: (0, 0)),
                  pl.BlockSpec((1, NA_HD), lambda b: (0, 0))],
        out_specs=[pl.BlockSpec((seq, NA_W), lambda b: (b, 0)), cache_spec, cache_spec],
        out_shape=[jax.ShapeDtypeStruct((t, NA_W), F32), cache_shape, cache_shape],
        compiler_params=_cparams("parallel"),
        name="ctx_attention",
    )(proj, proj, proj, gq, gk)


def _na_kernel(q_ref, k_ref, v_ref, kc_ref, vc_ref, gq_ref, gk_ref, tab_ref, o_ref, kn_s, v_s, *, rows, kh):
    scale = NA_HD ** -0.5
    kn_s[...] = _head_norm(k_ref[...], gk_ref[...]).astype(BF16)
    v_s[...] = v_ref[...].astype(BF16)
    kctx = kc_ref[0, 0, 0].astype(BF16)
    vctx = vc_ref[0, 0, 0].astype(BF16)
    win = kh * GRID_W

    def body(r, carry):
        r_start = jnp.clip(r - kh // 2, 0, rows - kh)
        qrows = pl.ds(pl.multiple_of(r * GRID_W, GRID_W), GRID_W)
        wrows = pl.ds(pl.multiple_of(r_start * GRID_W, GRID_W), win)
        qn = (_head_norm(q_ref[qrows, :], gq_ref[...]) * scale).astype(BF16)
        s_loc = lax.dot_general(qn, kn_s[wrows, :], (((1,), (1,)), ((), ())), preferred_element_type=F32)
        s_loc = s_loc + tab_ref[0, r_start - r + (NA_KH - 1)]
        s_ctx = lax.dot_general(qn, kctx, (((1,), (1,)), ((), ())), preferred_element_type=F32)
        m = jnp.maximum(jnp.max(s_loc, axis=-1, keepdims=True), jnp.max(s_ctx, axis=-1, keepdims=True))
        e_loc = jnp.exp(s_loc - m)
        e_ctx = jnp.exp(s_ctx - m)
        den = jnp.sum(e_loc, axis=-1, keepdims=True) + jnp.sum(e_ctx, axis=-1, keepdims=True)
        acc = jnp.dot(e_loc.astype(BF16), v_s[wrows, :], preferred_element_type=F32)
        acc = acc + jnp.dot(e_ctx.astype(BF16), vctx, preferred_element_type=F32)
        o_ref[qrows, :] = acc / den
        return carry

    lax.fori_loop(0, rows, body, 0)


def neighbourhood_attention(proj, cache_k, cache_v, gq, gk, table, *, batch, seq, layer, row0):
    rows = seq // GRID_W
    kh = min(NA_KH, rows)
    past = cache_k.shape[3]
    cq, ck, cv = OFF_NQ // NA_HD, OFF_NK // NA_HD, OFF_NV // NA_HD
    cache_spec = pl.BlockSpec((1, 1, 1, past, NA_HD), lambda b, h: (b, layer, h, 0, 0))
    return pl.pallas_call(
        functools.partial(_na_kernel, rows=rows, kh=kh),
        grid=(batch, NA_HEADS),
        in_specs=[pl.BlockSpec((seq, NA_HD), lambda b, h: (row0 + b, cq + h)),
                  pl.BlockSpec((seq, NA_HD), lambda b, h: (row0 + b, ck + h)),
                  pl.BlockSpec((seq, NA_HD), lambda b, h: (row0 + b, cv + h)),
                  cache_spec, cache_spec,
                  pl.BlockSpec((1, NA_HD), lambda b, h: (0, 0)),
                  pl.BlockSpec((1, NA_HD), lambda b, h: (0, 0)),
                  pl.BlockSpec((1, NA_KH, GRID_W, kh * GRID_W), lambda b, h: (h, 0, 0, 0))],
        out_specs=pl.BlockSpec((seq, NA_HD), lambda b, h: (b, h)),
        out_shape=jax.ShapeDtypeStruct((batch * seq, NA_W), F32),
        scratch_shapes=[pltpu.VMEM((seq, NA_HD), BF16), pltpu.VMEM((seq, NA_HD), BF16)],
        compiler_params=_cparams("parallel", "parallel"),
        name="neighbourhood_attention",
    )(proj, proj, proj, cache_k, cache_v, gq, gk, table)


def na_bias_table(rpb, rows):
    kh = min(NA_KH, rows)
    col = jnp.arange(GRID_W)
    c_start = jnp.clip(col - NA_KW // 2, 0, GRID_W - NA_KW)
    col_in = (col[None, :] >= c_start[:, None]) & (col[None, :] < c_start[:, None] + NA_KW)
    col_idx = jnp.clip(col[None, :] - col[:, None] + NA_KW - 1, 0, 2 * NA_KW - 2)
    rpb_cols = rpb[:, :, col_idx]
    row_idx = jnp.clip(jnp.arange(NA_KH)[:, None] + jnp.arange(kh)[None, :], 0, 2 * NA_KH - 2)
    tab = rpb_cols[:, row_idx]
    tab = jnp.where(col_in[None, None, None], tab, -jnp.inf)
    tab = jnp.transpose(tab, (0, 1, 3, 2, 4))
    return tab.reshape(rpb.shape[0], NA_KH, GRID_W, kh * GRID_W).astype(F32)


def _filter_kernel(z_ref, f1w_ref, f1b_ref, f2w_ref, f2b_ref, f3w_ref, fr_ref, dl_ref, o_ref, *, seq):
    fr = fr_ref[...]
    h = jnp.sin(fr * (jnp.dot(z_ref[...], f1w_ref[...], preferred_element_type=F32, precision=HIGHEST)
                      + f1b_ref[...]))
    h = jnp.sin(fr * (jnp.dot(h, f2w_ref[...], preferred_element_type=F32, precision=HIGHEST)
                      + f2b_ref[...]))
    h = jnp.dot(h, f3w_ref[...], preferred_element_type=F32, precision=HIGHEST)
    t = lax.broadcasted_iota(jnp.int32, (seq, 1), 0).astype(F32)
    dist = jnp.abs(t - float(seq // 2)) * (2.0 / seq)
    h = h * jnp.exp(-dist * dl_ref[...])
    o_ref[...] = h * lax.rsqrt(jnp.sum(h * h, axis=0, keepdims=True) + NORM_EPS)


def hyena_filters(zpos, f1w, f1b, f2w, f2b, f3w, freq, deltas, *, seq):
    n = f3w.shape[1]
    tc = 512
    full = lambda a: pl.BlockSpec(a.shape, lambda j: (0,) * a.ndim)
    return pl.pallas_call(
        functools.partial(_filter_kernel, seq=seq),
        grid=(n // tc,),
        in_specs=[full(zpos), full(f1w), full(f1b), full(f2w), full(f2b),
                  pl.BlockSpec((HY_HIDDEN, tc), lambda j: (0, j)), full(freq),
                  pl.BlockSpec((1, tc), lambda j: (0, j))],
        out_specs=pl.BlockSpec((seq, tc), lambda j: (0, j)),
        out_shape=jax.ShapeDtypeStruct((seq, n), F32),
        compiler_params=_cparams("parallel"),
        name="hyena_filters",
    )(zpos, f1w, f1b, f2w, f2b, f3w, freq, deltas)


def _spectrum_kernel(a_ref, h_ref, o_ref, *, seq, tkb):
    hs = jnp.dot(a_ref[0], h_ref[...].astype(BF16), preferred_element_type=F32)
    hc, hsn = hs[:tkb], hs[tkb:]
    k = pl.program_id(1) * tkb + lax.broadcasted_iota(jnp.int32, (tkb, 1), 0)
    km = k % 4
    gr = jnp.where(km == 0, hc, jnp.where(km == 1, hsn, jnp.where(km == 2, -hc, -hsn))) * (1.0 / seq)
    gi = jnp.where(km == 0, -hsn, jnp.where(km == 1, hc, jnp.where(km == 2, hsn, -hc))) * (1.0 / seq)
    dc = k == 0
    o_ref[0] = jnp.where(dc, hc * (0.5 / seq), gr)
    o_ref[1] = jnp.where(dc, 0.0, gi)
    o_ref[2] = jnp.where(dc, hsn * (0.5 / seq), gr)


def filter_spectrum(dft_a, filt, *, seq, tkb):
    n = filt.shape[1]
    tc = 1024
    nkb = seq // tkb
    return pl.pallas_call(
        functools.partial(_spectrum_kernel, seq=seq, tkb=tkb),
        grid=(n // tc, nkb),
        in_specs=[pl.BlockSpec((1, 2 * tkb, seq), lambda j, kb: (kb, 0, 0)),
                  pl.BlockSpec((seq, tc), lambda j, kb: (0, j))],
        out_specs=pl.BlockSpec((3, tkb, tc), lambda j, kb: (0, kb, j)),
        out_shape=jax.ShapeDtypeStruct((3, seq, n), F32),
        compiler_params=_cparams("parallel", "arbitrary"),
        name="filter_spectrum",
    )(dft_a, filt)


def _short_conv(u, w, bias, seq):
    t = lax.broadcasted_iota(jnp.int32, (seq, 1), 0)
    prev = jnp.where(t == 0, 0.0, pltpu.roll(u, 1, 0))
    nxt = jnp.where(t == seq - 1, 0.0, pltpu.roll(u, seq - 1, 0))
    return bias + prev * w[0:1, :] + u * w[1:2, :] + nxt * w[2:3, :]


def _hy_pre_kernel(u_ref, w_ref, b_ref, z_ref, *, seq):
    z_ref[...] = _short_conv(u_ref[...], w_ref[...], b_ref[...], seq).astype(BF16)


def _hy_gate_kernel(ux_ref, uz_ref, wx_ref, bx_ref, wz_ref, bz_ref, cv_ref, hb_ref, zo_ref, zb_ref, *, seq):
    x = _short_conv(ux_ref[...], wx_ref[...], bx_ref[...], seq)
    z = _short_conv(uz_ref[...], wz_ref[...], bz_ref[...], seq)
    out = x * (cv_ref[...] + hb_ref[...] * z)
    zo_ref[...] = out
    zb_ref[...] = out.astype(BF16)


def _hy_final_kernel(ux_ref, wx_ref, bx_ref, cv_ref, hb_ref, z_ref, zb_ref, *, seq):
    x = _short_conv(ux_ref[...], wx_ref[...], bx_ref[...], seq)
    zb_ref[...] = (x * (cv_ref[...] + hb_ref[...] * z_ref[...])).astype(BF16)


HY_TC_ELEM = 128


def _hy_specs(seq, row0, col0):
    cb = col0 // HY_TC_ELEM
    u = pl.BlockSpec((seq, HY_TC_ELEM), lambda b, j: (row0 + b, cb + j))
    w = pl.BlockSpec((3, HY_TC_ELEM), lambda b, j: (0, (col0 - OFF_HY) // HY_TC_ELEM + j))
    bb = pl.BlockSpec((1, HY_TC_ELEM), lambda b, j: (0, (col0 - OFF_HY) // HY_TC_ELEM + j))
    return u, w, bb


def hyena_pre(proj, conv_w, conv_b, *, batch, seq, row0):
    u, w, bb = _hy_specs(seq, row0, OFF_HY + HY_ORDER * HY_W)
    act = pl.BlockSpec((seq, HY_TC_ELEM), lambda b, j: (b, j))
    return pl.pallas_call(
        functools.partial(_hy_pre_kernel, seq=seq),
        grid=(batch, HY_W // HY_TC_ELEM),
        in_specs=[u, w, bb], out_specs=act,
        out_shape=jax.ShapeDtypeStruct((batch * seq, HY_W), BF16),
        compiler_params=_cparams("parallel", "parallel"), name="hyena_pre",
    )(proj, conv_w, conv_b)


def hyena_gate(proj, conv_w, conv_b, conv, hy_bias, *, batch, seq, row0):
    ux, wx, bx = _hy_specs(seq, row0, OFF_HY)
    uz, wz, bz = _hy_specs(seq, row0, OFF_HY + HY_ORDER * HY_W)
    act = pl.BlockSpec((seq, HY_TC_ELEM), lambda b, j: (b, j))
    hb = pl.BlockSpec((1, HY_TC_ELEM), lambda b, j: (0, j))
    return pl.pallas_call(
        functools.partial(_hy_gate_kernel, seq=seq),
        grid=(batch, HY_W // HY_TC_ELEM),
        in_specs=[ux, uz, wx, bx, wz, bz, act, hb], out_specs=[act, act],
        out_shape=[jax.ShapeDtypeStruct((batch * seq, HY_W), F32),
                   jax.ShapeDtypeStruct((batch * seq, HY_W), BF16)],
        compiler_params=_cparams("parallel", "parallel"), name="hyena_gate",
    )(proj, proj, conv_w, conv_b, conv_w, conv_b, conv, hy_bias[0:1])


def hyena_final(proj, conv_w, conv_b, conv, hy_bias, z1, *, batch, seq, row0):
    ux, wx, bx = _hy_specs(seq, row0, OFF_HY + HY_W)
    act = pl.BlockSpec((seq, HY_TC_ELEM), lambda b, j: (b, j))
    hb = pl.BlockSpec((1, HY_TC_ELEM), lambda b, j: (0, j))
    return pl.pallas_call(
        functools.partial(_hy_final_kernel, seq=seq),
        grid=(batch, HY_W // HY_TC_ELEM),
        in_specs=[ux, wx, bx, act, hb, act], out_specs=act,
        out_shape=jax.ShapeDtypeStruct((batch * seq, HY_W), BF16),
        compiler_params=_cparams("parallel", "parallel"), name="hyena_final",
    )(proj, conv_w, conv_b, conv, hy_bias[1:2], z1)


def _long_conv_kernel(z_ref, a_ref, bt_ref, g_ref, o_ref, *, tkb, seq, trb):
    kb = pl.program_id(2)

    @pl.when(kb == 0)
    def _():
        o_ref[...] = jnp.zeros(o_ref.shape, F32)

    zs = jnp.dot(a_ref[0], z_ref[...], preferred_element_type=F32)
    zr, zi = zs[:tkb], zs[tkb:]
    gr, gi, gq = g_ref[0], g_ref[1], g_ref[2]
    p = zr * gr + zi * gi
    q = zi * gq - zr * gi
    pq = jnp.concatenate([p, q], axis=0).astype(BF16)

    def body(rb, carry):
        rows = pl.ds(pl.multiple_of(rb * trb, trb), trb)
        o_ref[rows, :] += jnp.dot(bt_ref[0, rows, :], pq, preferred_element_type=F32)
        return carry

    lax.fori_loop(0, seq // trb, body, 0)


def hyena_long_conv(z, dft_a, dft_bt, g, *, batch, seq, order, tkb):
    c = z.shape[1]
    tc = 512
    nkb = seq // tkb
    trb = min(seq, 512)
    goff = order * (c // tc)
    return pl.pallas_call(
        functools.partial(_long_conv_kernel, tkb=tkb, seq=seq, trb=trb),
        grid=(batch, c // tc, nkb),
        in_specs=[pl.BlockSpec((seq, tc), lambda b, j, kb: (b, j)),
                  pl.BlockSpec((1, 2 * tkb, seq), lambda b, j, kb: (kb, 0, 0)),
                  pl.BlockSpec((1, seq, 2 * tkb), lambda b, j, kb: (kb, 0, 0)),
                  pl.BlockSpec((3, tkb, tc), lambda b, j, kb: (0, kb, goff + j))],
        out_specs=pl.BlockSpec((seq, tc), lambda b, j, kb: (b, j)),
        out_shape=jax.ShapeDtypeStruct((batch * seq, c), F32),
        compiler_params=_cparams("parallel", "parallel", "arbitrary"),
        name="hyena_long_conv",
    )(z, dft_a, dft_bt, g)


def dft_tables(seq, tkb):
    hi = max(seq // 256, 1)
    lo = seq // hi
    m = jnp.arange(seq, dtype=jnp.int32)

    def cs(kvals):
        r = (kvals[:, None] * m[None, :]) % (2 * seq)
        ang = r.astype(F32) * (math.pi / seq)
        return jnp.cos(ang), jnp.sin(ang)

    c1, s1 = cs(jnp.arange(hi, dtype=jnp.int32) * lo)
    c2, s2 = cs(jnp.arange(lo, dtype=jnp.int32))
    cm = (c1[:, None] * c2[None] - s1[:, None] * s2[None]).reshape(seq, seq)
    sm = (s1[:, None] * c2[None] + c1[:, None] * s2[None]).reshape(seq, seq)
    nyq = jnp.where(m % 2 == 0, 1.0, -1.0).astype(F32)
    sm = jnp.where(jnp.arange(seq)[:, None] == 0, nyq[None, :], sm)
    nkb = seq // tkb
    a = jnp.concatenate([cm.reshape(nkb, tkb, seq), sm.reshape(nkb, tkb, seq)], axis=1).astype(BF16)
    return a, jnp.transpose(a, (0, 2, 1))


def position_features(seq):
    t = jnp.arange(seq, dtype=F32)
    tn = t / seq
    bands = jnp.arange(1, HY_BANDS + 1, dtype=F32)
    ang = 2.0 * jnp.pi * tn[:, None] * bands[None, :]
    z = jnp.concatenate([tn[:, None], jnp.cos(ang), jnp.sin(ang)], axis=-1)
    return jnp.pad(z, ((0, 0), (0, LANES - z.shape[1])))


def rope_tables(seq):
    n_freq = GLA_DK // 4
    pos = jnp.arange(seq)
    row = (pos // GRID_W).astype(F32)
    col = (pos % GRID_W).astype(F32)
    inv = ROPE_BASE ** (-jnp.arange(n_freq, dtype=F32) / n_freq)
    ang = jnp.concatenate([row[:, None] * inv, col[:, None] * inv], axis=-1)
    cos, sin = jnp.cos(ang), jnp.sin(ang)
    return jnp.concatenate([cos, cos], axis=-1), jnp.concatenate([-sin, sin], axis=-1)


def _merge_kernel(of_ref, ob_ref, gg_ref, gn_ref, nb_ref, hc_ref, ga_ref, gb_ref, gc_ref,
                  wa_ref, wb_ref, wc_ref, o_ref, a_s):
    @pl.when(pl.program_id(1) == 0)
    def _():
        for h in range(GLA_HEADS):
            cols = slice(h * GLA_DV, (h + 1) * GLA_DV)
            oa = of_ref[:, cols] + ob_ref[:, cols]
            oa = _head_norm(oa, gn_ref[...])
            gg = gg_ref[:, cols]
            a_s[:, cols] = (oa * (gg * jax.nn.sigmoid(gg))).astype(BF16)

    ya = jnp.dot(a_s[...], wa_ref[...], preferred_element_type=F32)
    yb = jnp.dot(nb_ref[...].astype(BF16), wb_ref[...], preferred_element_type=F32)
    yc = jnp.dot(hc_ref[...], wc_ref[...], preferred_element_type=F32)
    merged = (jax.nn.sigmoid(ga_ref[...]) * ya + jax.nn.sigmoid(gb_ref[...]) * yb
              + jax.nn.sigmoid(gc_ref[...]) * yc)
    o_ref[...] = merged.astype(o_ref.dtype)


def branch_merge(o_f, o_b, proj, gnorm, o_attn, o_hy, w_a, w_b, w_c, *, row0_blocks, tm, tn):
    t = o_f.shape[0]
    d = w_a.shape[1]
    act = lambda w: pl.BlockSpec((tm, w), lambda i, j: (i, 0))
    gate = lambda g: pl.BlockSpec((tm, tn), lambda i, j: (row0_blocks + i, (OFF_GATES + g * d) // tn + j))
    wspec = lambda w: pl.BlockSpec((w.shape[0], tn), lambda i, j: (0, j))
    return pl.pallas_call(
        _merge_kernel,
        grid=(t // tm, d // tn),
        in_specs=[act(VAL_W), act(VAL_W),
                  pl.BlockSpec((tm, VAL_W), lambda i, j: (row0_blocks + i, OFF_GG // VAL_W)),
                  pl.BlockSpec((1, GLA_DV), lambda i, j: (0, 0)),
                  act(NA_W), act(HY_W), gate(0), gate(1), gate(2),
                  wspec(w_a), wspec(w_b), wspec(w_c)],
        out_specs=pl.BlockSpec((tm, tn), lambda i, j: (i, j)),
        out_shape=jax.ShapeDtypeStruct((t, d), BF16),
        scratch_shapes=[pltpu.VMEM((tm, VAL_W), BF16)],
        compiler_params=_cparams("parallel", "arbitrary"),
        name="branch_merge",
    )(o_f, o_b, proj, gnorm, o_attn, o_hy, proj, proj, proj, w_a, w_b, w_c)


def _matmul_residual_kernel(a_ref, w_ref, g_ref, r_ref, o_ref):
    y = jnp.dot(a_ref[...], w_ref[...], preferred_element_type=F32)
    o_ref[...] = r_ref[...] + g_ref[0] * y


def matmul_residual(a, w, gate, resid, *, rows_per_mod, tm, tn):
    t, k = a.shape
    n = w.shape[1]
    return pl.pallas_call(
        _matmul_residual_kernel,
        grid=(t // tm, n // tn),
        in_specs=[pl.BlockSpec((tm, k), lambda i, j: (i, 0)),
                  pl.BlockSpec((k, tn), lambda i, j: (0, j)),
                  pl.BlockSpec((1, 1, tn), lambda i, j: ((i * tm) // rows_per_mod, 0, j)),
                  pl.BlockSpec((tm, tn), lambda i, j: (i, j))],
        out_specs=pl.BlockSpec((tm, tn), lambda i, j: (i, j)),
        out_shape=jax.ShapeDtypeStruct((t, n), F32),
        compiler_params=_cparams("parallel", "arbitrary"),
        name="matmul_residual",
    )(a, w, gate, resid)


def _trunk_layer(x, mods, p, consts, *, batch, seq, layer, latent):
    sh1, sc1, g1, sh2, sc2, g2 = mods
    t = x.shape[0]
    rows_per_mod = t // sh1.shape[0]
    tm = 512

    proj = norm_matmul(x, p["norm1_g"], sc1, sh1, p["w_in_main"], rows_per_mod=rows_per_mod,
                       tm=tm, tn=1536, relu2=False, out_dtype=F32)
    la = decay_gate(x, p["norm1_g"], sc1, sh1, p["w_in_ga"], p["wa2_bd"], p["ba2"],
                    rows_per_mod=rows_per_mod, tm=tm)

    rows_blk = min(seq, 512)
    rope = consts["rope"] if latent else None
    s0 = consts["state_gla"] if latent else None
    o_f, s_f = gla_direction(proj, la, rope, s0, batch=batch, seq=seq, reverse=False, layer=layer,
                             direction=0, rows_blk=rows_blk)
    o_b, s_b = gla_direction(proj, la, rope, s0, batch=batch, seq=seq, reverse=True, layer=layer,
                             direction=1, rows_blk=rows_blk)

    if latent:
        o_attn = neighbourhood_attention(proj, consts["cache_k"], consts["cache_v"], p["na_qnorm_g"],
                                         p["na_knorm_g"], p["na_table"], batch=batch, seq=seq,
                                         layer=layer, row0=0)
        new_ctx = None
    else:
        o_attn, kn, vn = ctx_attention(proj, p["na_qnorm_g"], p["na_knorm_g"], batch=batch, seq=seq)
        new_ctx = (jnp.stack([s_f, s_b], axis=1), kn, vn)

    key = "lat" if latent else "ctx"
    dft_a, dft_bt = consts["dft_" + key]
    tkb = consts["tkb_" + key]
    g_spec = p["spectrum_" + key]
    z0 = hyena_pre(proj, p["hy_conv_w"], p["hy_conv_b"], batch=batch, seq=seq, row0=0)
    cv0 = hyena_long_conv(z0, dft_a, dft_bt, g_spec, batch=batch, seq=seq, order=0, tkb=tkb)
    z1, z1b = hyena_gate(proj, p["hy_conv_w"], p["hy_conv_b"], cv0, p["hy_bias"], batch=batch, seq=seq, row0=0)
    cv1 = hyena_long_conv(z1b, dft_a, dft_bt, g_spec, batch=batch, seq=seq, order=1, tkb=tkb)
    o_hy = hyena_final(proj, p["hy_conv_w"], p["hy_conv_b"], cv1, p["hy_bias"], z1, batch=batch, seq=seq, row0=0)

    merged = branch_merge(o_f, o_b, proj, p["gla_norm_g"], o_attn, o_hy, p["w_br_a"], p["w_br_b"],
                          p["w_br_c"], row0_blocks=0, tm=256, tn=1024)
    x = matmul_residual(merged, p["w_out"], g1, x, rows_per_mod=rows_per_mod, tm=tm, tn=2048)

    u = norm_matmul(x, p["norm2_g"], sc2, sh2, p["w_mlp1"], rows_per_mod=rows_per_mod,
                    tm=tm, tn=2048, relu2=True, out_dtype=BF16)
    x = matmul_residual(u, p["w_mlp2"], g2, x, rows_per_mod=rows_per_mod, tm=tm, tn=512)
    return x, new_ctx


def _reorder_w_in(w_in):
    o_ga = 2 * KEY_W + 2 * VAL_W
    n_ga = 2 * GLA_LOWRANK
    main = jnp.concatenate([w_in[..., :o_ga], w_in[..., o_ga + n_ga:]], axis=-1).astype(BF16)
    ga = jnp.pad(w_in[..., o_ga:o_ga + n_ga], ((0, 0), (0, 0), (0, LANES - n_ga))).astype(BF16)
    return main, ga


def kernel(x_prompt, x_sample, state_gla, cache_na_k, cache_na_v, c, c_ctx, w_mod, b_mod, norm1_g, norm2_g, w_in, gla_wa2, gla_ba2, gla_norm_g, na_qnorm_g, na_knorm_g, na_rpb, hy_conv_w, hy_conv_b, hy_f1_w, hy_f1_b, hy_f2_w, hy_f2_b, hy_f3_w, hy_freq, hy_bias, w_br_a, w_br_b, w_br_c, w_out, w_mlp1, w_mlp2):
    batch, seq, d = x_prompt.shape
    dec_batch, dec_seq, _ = x_sample.shape
    depth = w_mod.shape[0]

    w_in_main, w_in_ga = _reorder_w_in(w_in)
    bf = lambda w: w.astype(BF16)
    w_br_a_b, w_br_b_b, w_br_c_b, w_out_b, w_mlp1_b, w_mlp2_b = map(bf, (w_br_a, w_br_b, w_br_c, w_out, w_mlp1, w_mlp2))
    wa2_bd = jnp.zeros((depth, LANES, 2 * KEY_W), F32)
    wa2_bd = wa2_bd.at[:, :GLA_LOWRANK, :KEY_W].set(gla_wa2[:, 0])
    wa2_bd = wa2_bd.at[:, GLA_LOWRANK:2 * GLA_LOWRANK, KEY_W:].set(gla_wa2[:, 1])
    ba2 = gla_ba2.reshape(depth, 1, 2 * KEY_W)
    f1w = jnp.pad(hy_f1_w, ((0, 0), (0, LANES - hy_f1_w.shape[1]), (0, 0)))

    tkb_ctx, tkb_lat = min(seq, 256), min(dec_seq, 256)
    consts = {
        "rope": rope_tables(dec_seq),
        "state_gla": state_gla, "cache_k": cache_na_k, "cache_v": cache_na_v,
        "dft_ctx": dft_tables(seq, tkb_ctx), "dft_lat": dft_tables(dec_seq, tkb_lat),
        "tkb_ctx": tkb_ctx, "tkb_lat": tkb_lat,
    }
    deltas = jnp.abs(jnp.linspace(math.log(HY_DECAY_TARGET) / HY_DECAY_PCT_LONG,
                                  math.log(HY_DECAY_TARGET) / HY_DECAY_PCT_SHORT, HY_W, dtype=F32))
    deltas = jnp.tile(deltas, HY_ORDER)[None, :]
    zpos = {"ctx": position_features(seq), "lat": position_features(dec_seq)}

    cond = jnp.zeros((16, d), F32).at[0].set(c_ctx).at[1:1 + dec_batch].set(c)

    xp = x_prompt.reshape(batch * seq, d)
    xs = x_sample.reshape(dec_batch * dec_seq, d)
    new_gla, new_k, new_v = [], [], []
    for l in range(depth):
        m = adaln(cond, w_mod[l], b_mod[l][None, :])
        mods_ctx = tuple(m[0:1, i * d:(i + 1) * d][:, None, :] for i in range(N_MOD))
        mods_lat = tuple(m[1:1 + dec_batch, i * d:(i + 1) * d][:, None, :] for i in range(N_MOD))
        p = dict(norm1_g=norm1_g[l][None], norm2_g=norm2_g[l][None], w_in_main=w_in_main[l], w_in_ga=w_in_ga[l],
                 wa2_bd=wa2_bd[l], ba2=ba2[l], gla_norm_g=gla_norm_g[l][None], na_qnorm_g=na_qnorm_g[l][None],
                 na_knorm_g=na_knorm_g[l][None], na_table=na_bias_table(na_rpb[l], dec_seq // GRID_W),
                 hy_conv_w=hy_conv_w[l], hy_conv_b=hy_conv_b[l][None], hy_bias=hy_bias[l],
                 w_br_a=w_br_a_b[l], w_br_b=w_br_b_b[l], w_br_c=w_br_c_b[l], w_out=w_out_b[l],
                 w_mlp1=w_mlp1_b[l], w_mlp2=w_mlp2_b[l])
        for key, length in (("ctx", seq), ("lat", dec_seq)):
            filt = hyena_filters(zpos[key], f1w[l], hy_f1_b[l][None], hy_f2_w[l], hy_f2_b[l][None],
                                 hy_f3_w[l], hy_freq[l][None], deltas, seq=length)
            p["spectrum_" + key] = filter_spectrum(consts["dft_" + key][0], filt, seq=length,
                                                   tkb=consts["tkb_" + key])
        xp, (s_gla, k_ctx, v_ctx) = _trunk_layer(xp, mods_ctx, p, consts, batch=batch, seq=seq, layer=l, latent=False)
        new_gla.append(s_gla)
        new_k.append(k_ctx)
        new_v.append(v_ctx)
        xs, _ = _trunk_layer(xs, mods_lat, p, consts, batch=dec_batch, seq=dec_seq, layer=l, latent=True)
    return (xp.reshape(batch, seq, d), xs.reshape(dec_batch, dec_seq, d),
            jnp.stack(new_gla, axis=1), jnp.stack(new_k, axis=1), jnp.stack(new_v, axis=1))
```

```python
import functools
import math

import jax
import jax.numpy as jnp
import numpy as np
from jax import lax
from jax.experimental import pallas as pl
from jax.experimental.pallas import tpu as pltpu

F32 = jnp.float32
BF16 = jnp.bfloat16
HIGHEST = lax.Precision.HIGHEST

GRID_W = 64
N_MOD = 6
NORM_EPS = 1e-6
GLA_HEADS = 4
GLA_DK = 128
GLA_DV = 256
GLA_LOWRANK = 16
GLA_TAU = 16.0
GLA_CHUNK = 64
GLA_SUB = 16
GLA_SAFE_EXP = 60.0
ROPE_BASE = 10000.0
NA_HEADS = 8
NA_HD = 128
NA_KH = 8
NA_KW = 16
NA_QROWS = 4
NA_UNION = NA_KH + NA_QROWS
NA_BLOCK_UNROLL = 2
HY_W = 1024
HY_ORDER = 2
HY_BANDS = 16
HY_HIDDEN = 64
HY_DECAY_TARGET = 1e-2
HY_DECAY_PCT_SHORT = 0.3
HY_DECAY_PCT_LONG = 1.5

V7X_VMEM_LIMIT_BYTES = 56 * 1024 * 1024
LANES = 128

KEY_W = GLA_HEADS * GLA_DK
VAL_W = GLA_HEADS * GLA_DV
NA_W = NA_HEADS * NA_HD
OFF_GQ = 0
OFF_GK = OFF_GQ + KEY_W
OFF_GV = OFF_GK + KEY_W
OFF_GG = OFF_GV + VAL_W
OFF_NQ = OFF_GG + VAL_W
OFF_NK = OFF_NQ + NA_W
OFF_NV = OFF_NK + NA_W
OFF_GATES = OFF_NV + NA_W
OFF_HY = OFF_GATES + 3 * 2048


def _cparams(*sem):
    return pltpu.CompilerParams(dimension_semantics=sem, vmem_limit_bytes=V7X_VMEM_LIMIT_BYTES)


def _modulated_norm(x, g, scale, shift):
    ms = jnp.mean(x * x, axis=-1, keepdims=True)
    y = x * lax.rsqrt(ms + NORM_EPS) * g
    return y * (1.0 + scale) + shift


def _adaln_kernel(c_ref, w_ref, b_ref, o_ref):
    c = c_ref[...]
    a = (c * jax.nn.sigmoid(c)).astype(BF16)
    o_ref[...] = jnp.dot(a, w_ref[...].astype(BF16), preferred_element_type=F32) + b_ref[...]


def adaln(cond, w_mod, b_mod):
    m, d = cond.shape
    n = w_mod.shape[1]
    tn = 1024
    return pl.pallas_call(
        _adaln_kernel,
        grid=(n // tn,),
        in_specs=[pl.BlockSpec((m, d), lambda j: (0, 0)),
                  pl.BlockSpec((d, tn), lambda j: (0, j)),
                  pl.BlockSpec((1, tn), lambda j: (0, j))],
        out_specs=pl.BlockSpec((m, tn), lambda j: (0, j)),
        out_shape=jax.ShapeDtypeStruct((m, n), F32),
        compiler_params=_cparams("parallel"),
        name="adaln",
    )(cond, w_mod, b_mod)


def _norm_matmul_kernel(x_ref, g_ref, sc_ref, sh_ref, w_ref, o_ref, h_ref, *, relu2):
    @pl.when(pl.program_id(1) == 0)
    def _():
        h = _modulated_norm(x_ref[...], g_ref[...], sc_ref[0], sh_ref[0])
        h_ref[...] = h.astype(BF16)

    acc = jnp.dot(h_ref[...], w_ref[...], preferred_element_type=F32)
    if relu2:
        acc = jnp.square(jnp.maximum(acc, 0.0))
    o_ref[...] = acc.astype(o_ref.dtype)


def norm_matmul(x, g, scale, shift, w, *, rows_per_mod, tm, tn, relu2, out_dtype):
    t, d = x.shape
    n = w.shape[1]
    mod_map = lambda i, j: ((i * tm) // rows_per_mod, 0, 0)
    return pl.pallas_call(
        functools.partial(_norm_matmul_kernel, relu2=relu2),
        grid=(t // tm, n // tn),
        in_specs=[pl.BlockSpec((tm, d), lambda i, j: (i, 0)),
                  pl.BlockSpec((1, d), lambda i, j: (0, 0)),
                  pl.BlockSpec((1, 1, d), mod_map),
                  pl.BlockSpec((1, 1, d), mod_map),
                  pl.BlockSpec((d, tn), lambda i, j: (0, j))],
        out_specs=pl.BlockSpec((tm, tn), lambda i, j: (i, j)),
        out_shape=jax.ShapeDtypeStruct((t, n), out_dtype),
        scratch_shapes=[pltpu.VMEM((tm, d), BF16)],
        compiler_params=_cparams("parallel", "arbitrary"),
        name="norm_matmul_relu2" if relu2 else "norm_matmul",
    )(x, g, scale, shift, w)


def _decay_gate_kernel(x_ref, g_ref, sc_ref, sh_ref, wga_ref, wa2_ref, ba2_ref, o_ref):
    h = _modulated_norm(x_ref[...], g_ref[...], sc_ref[0], sh_ref[0]).astype(BF16)
    ga = jnp.dot(h, wga_ref[...], preferred_element_type=F32)
    pre = jnp.dot(ga, wa2_ref[...], preferred_element_type=F32, precision=HIGHEST) + ba2_ref[...]
    log_sig = jnp.minimum(pre, 0.0) - jnp.log(1.0 + jnp.exp(-jnp.abs(pre)))
    o_ref[...] = log_sig * (1.0 / GLA_TAU)


def decay_gate(x, g, scale, shift, w_ga, wa2_bd, ba2, *, rows_per_mod, tm):
    t, d = x.shape
    n = wa2_bd.shape[1]
    mod_map = lambda i: ((i * tm) // rows_per_mod, 0, 0)
    return pl.pallas_call(
        _decay_gate_kernel,
        grid=(t // tm,),
        in_specs=[pl.BlockSpec((tm, d), lambda i: (i, 0)),
                  pl.BlockSpec((1, d), lambda i: (0, 0)),
                  pl.BlockSpec((1, 1, d), mod_map),
                  pl.BlockSpec((1, 1, d), mod_map),
                  pl.BlockSpec((d, LANES), lambda i: (0, 0)),
                  pl.BlockSpec((LANES, n), lambda i: (0, 0)),
                  pl.BlockSpec((1, n), lambda i: (0, 0))],
        out_specs=pl.BlockSpec((tm, n), lambda i: (i, 0)),
        out_shape=jax.ShapeDtypeStruct((t, n), F32),
        compiler_params=_cparams("parallel"),
        name="decay_gate",
    )(x, g, scale, shift, w_ga, wa2_bd, ba2)


def _cumulative_log_decay(la, reverse):
    c = la.shape[0]
    row = lax.broadcasted_iota(jnp.int32, (c, c), 0)
    col = lax.broadcasted_iota(jnp.int32, (c, c), 1)
    tri = jnp.where((row <= col) if reverse else (row >= col), 1.0, 0.0).astype(BF16)
    hi = la.astype(BF16)
    rest = la - hi.astype(F32)
    mid = rest.astype(BF16)
    lo = (rest - mid.astype(F32)).astype(BF16)
    dot = lambda t: jnp.dot(tri, t, preferred_element_type=F32)
    return dot(hi) + dot(mid) + dot(lo)


def _gla_chunk_scores(q, k, v, b, *, reverse, exact):
    c = GLA_CHUNK
    n = GLA_SUB
    edge = b[0:1, :] if reverse else b[c - 1:c, :]
    qe = (q * jnp.exp(b)).astype(BF16)
    kd = (k * jnp.exp(edge - b)).astype(BF16)
    m = jnp.dot(v.T.astype(BF16), kd, preferred_element_type=F32)
    decay = jnp.exp(edge)

    lane = lax.broadcasted_iota(jnp.int32, (n, c), 1)
    sub_row = lax.broadcasted_iota(jnp.int32, (n, c), 0)
    blocks = []
    cap = 0.0 if exact else GLA_SAFE_EXP
    for i in range(c // n):
        lo, hi = i * n, (i + 1) * n
        q_i = q[lo:hi, :]
        b_i = b[lo:hi, :]
        has_border = (hi < c) if reverse else (lo > 0)
        if has_border:
            r = b[hi:hi + 1, :] if reverse else b[lo - 1:lo, :]
            lhs = (q_i * jnp.exp(b_i - r)).astype(BF16)
            rhs = (k * jnp.exp(jnp.minimum(r - b, cap))).astype(BF16)
        else:
            lhs = (q_i * jnp.exp(b_i)).astype(BF16)
            rhs = (k * jnp.exp(jnp.minimum(-b, cap))).astype(BF16)
        a_i = lax.dot_general(lhs, rhs, (((1,), (1,)), ((), ())), preferred_element_type=F32)
        if exact:
            a_diag = jnp.zeros((n, c), F32)
            for j in range(n):
                s = lo + j
                p = q_i * k[s:s + 1, :] * jnp.exp(b_i - b[s:s + 1, :])
                a_diag = jnp.where(lane == s, jnp.sum(p, axis=1, keepdims=True), a_diag)
            in_block = (lane >= lo) & (lane < hi)
            a_i = jnp.where(in_block, a_diag, a_i)
        causal = (lane >= sub_row + lo) if reverse else (lane <= sub_row + lo)
        blocks.append(jnp.where(causal, a_i, 0.0))
    a = jnp.concatenate(blocks, axis=0).astype(BF16)
    return qe, a, m, decay


def _gla_inter(qe, st):
    return lax.dot_general(qe, st.astype(BF16), (((1,), (1,)), ((), ())), preferred_element_type=F32)


def _gla_kernel(*refs, reverse, rope, has_s0, nchunk):
    it = iter(refs)
    q_ref, k_ref, v_ref, la_ref = next(it), next(it), next(it), next(it)
    cos_ref = sin_ref = s0_ref = None
    if rope:
        cos_ref, sin_ref = next(it), next(it)
    if has_s0:
        s0_ref = next(it)
    o_ref, sfin_ref, st_ref = next(it), next(it), next(it)

    blk = pl.program_id(2)

    @pl.when(blk == 0)
    def _():
        if has_s0:
            st_ref[...] = s0_ref[0, 0, 0, 0].T
        else:
            st_ref[...] = jnp.zeros(st_ref.shape, F32)

    def load_qk(rows):
        q = q_ref[rows, :] * (GLA_DK ** -0.5)
        k = k_ref[rows, :]
        if rope:
            cs, sn = cos_ref[rows, :], sin_ref[rows, :]
            q = q * cs + pltpu.roll(q, GLA_DK // 2, 1) * sn
            k = k * cs + pltpu.roll(k, GLA_DK // 2, 1) * sn
        return q, k

    def intra(a, rows):
        return jnp.dot(a, v_ref[rows, :].astype(BF16), preferred_element_type=F32)

    def run_exact():
        def body(step, carry):
            ci = (nchunk - 1 - step) if reverse else step
            rows = pl.ds(pl.multiple_of(ci * GLA_CHUNK, GLA_CHUNK), GLA_CHUNK)
            q, k = load_qk(rows)
            b = _cumulative_log_decay(la_ref[rows, :], reverse)
            qe, a, m, decay = _gla_chunk_scores(q, k, v_ref[rows, :], b, reverse=reverse, exact=True)
            o_ref[rows, :] = _gla_inter(qe, st_ref[...]) + intra(a, rows)
            st_ref[...] = st_ref[...] * decay + m
            return carry

        lax.fori_loop(0, nchunk, body, 0)

    def run_fast():
        order = list(range(nchunk))[::-1] if reverse else list(range(nchunk))
        rows = [pl.ds(ci * GLA_CHUNK, GLA_CHUNK) for ci in range(nchunk)]
        bs = [_cumulative_log_decay(la_ref[r, :], reverse) for r in rows]
        scores = [_gla_chunk_scores(*load_qk(r), v_ref[r, :], b, reverse=reverse, exact=False)
                  for r, b in zip(rows, bs)]
        o_intra = [intra(sc[1], r) for sc, r in zip(scores, rows)]
        st = st_ref[...]
        states = {}
        for ci in order:
            states[ci] = st
            st = st * scores[ci][3] + scores[ci][2]
        st_ref[...] = st
        for ci in range(nchunk):
            o_ref[rows[ci], :] = _gla_inter(scores[ci][0], states[ci]) + o_intra[ci]

    steep = jnp.min(la_ref[...]) < -(GLA_SAFE_EXP / GLA_SUB)
    pl.when(steep)(run_exact)
    pl.when(jnp.logical_not(steep))(run_fast)

    @pl.when(blk == pl.num_programs(2) - 1)
    def _():
        sfin_ref[0, 0] = st_ref[...].T


def gla_direction(proj, la, rope_tabs, s0, *, batch, seq, reverse, layer, direction, rows_blk):
    t = proj.shape[0]
    nblk = seq // rows_blk
    nchunk = rows_blk // GLA_CHUNK
    rope = rope_tabs is not None
    has_s0 = s0 is not None

    def rb(b, c):
        cc = (nblk - 1 - c) if reverse else c
        return b * nblk + cc

    kq, kv = OFF_GQ // GLA_DK, OFF_GV // GLA_DV
    kk = OFF_GK // GLA_DK
    kla = direction * GLA_HEADS
    in_specs = [pl.BlockSpec((rows_blk, GLA_DK), lambda b, h, c: (rb(b, c), kq + h)),
                pl.BlockSpec((rows_blk, GLA_DK), lambda b, h, c: (rb(b, c), kk + h)),
                pl.BlockSpec((rows_blk, GLA_DV), lambda b, h, c: (rb(b, c), kv + h)),
                pl.BlockSpec((rows_blk, GLA_DK), lambda b, h, c: (rb(b, c), kla + h))]
    args = [proj, proj, proj, la]
    if rope:
        tab_map = lambda b, h, c: ((nblk - 1 - c) if reverse else c, 0)
        in_specs += [pl.BlockSpec((rows_blk, GLA_DK), tab_map)] * 2
        args += list(rope_tabs)
    if has_s0:
        in_specs.append(pl.BlockSpec((1, 1, 1, 1, GLA_DK, GLA_DV), lambda b, h, c: (b, layer, direction, h, 0, 0)))
        args.append(s0)
    o, sfin = pl.pallas_call(
        functools.partial(_gla_kernel, reverse=reverse, rope=rope, has_s0=has_s0, nchunk=nchunk),
        grid=(batch, GLA_HEADS, nblk),
        in_specs=in_specs,
        out_specs=[pl.BlockSpec((rows_blk, GLA_DV), lambda b, h, c: (rb(b, c), h)),
                   pl.BlockSpec((1, 1, GLA_DK, GLA_DV), lambda b, h, c: (b, h, 0, 0))],
        out_shape=[jax.ShapeDtypeStruct((t, VAL_W), F32),
                   jax.ShapeDtypeStruct((batch, GLA_HEADS, GLA_DK, GLA_DV), F32)],
        scratch_shapes=[pltpu.VMEM((GLA_DV, GLA_DK), F32)],
        compiler_params=_cparams("parallel", "parallel", "arbitrary"),
        name="gla_bwd" if reverse else "gla_fwd",
    )(*args)
    return o, sfin


def _head_norm(x, g):
    return x * lax.rsqrt(jnp.mean(x * x, axis=-1, keepdims=True) + NORM_EPS) * g


def _ctx_attn_kernel(q_ref, k_ref, v_ref, gq_ref, gk_ref, o_ref, kn_ref, vn_ref):
    scale = NA_HD ** -0.5
    for h in range(NA_HEADS):
        cols = slice(h * NA_HD, (h + 1) * NA_HD)
        qn = _head_norm(q_ref[:, cols], gq_ref[...]) * scale
        kn = _head_norm(k_ref[:, cols], gk_ref[...])
        v = v_ref[:, cols]
        kn_ref[0, h] = kn
        vn_ref[0, h] = v
        s = lax.dot_general(qn.astype(BF16), kn.astype(BF16), (((1,), (1,)), ((), ())),
                            preferred_element_type=F32)
        e = jnp.exp(s - jnp.max(s, axis=-1, keepdims=True))
        p = e / jnp.sum(e, axis=-1, keepdims=True)
        o_ref[:, cols] = jnp.dot(p.astype(BF16), v.astype(BF16), preferred_element_type=F32)


def ctx_attention(proj, gq, gk, *, batch, seq):
    t = proj.shape[0]
    cq, ck, cv = OFF_NQ // NA_W, OFF_NK // NA_W, OFF_NV // NA_W
    cache_shape = jax.ShapeDtypeStruct((batch, NA_HEADS, seq, NA_HD), F32)
    cache_spec = pl.BlockSpec((1, NA_HEADS, seq, NA_HD), lambda b: (b, 0, 0, 0))
    return pl.pallas_call(
        _ctx_attn_kernel,
        grid=(batch,),
        in_specs=[pl.BlockSpec((seq, NA_W), lambda b: (b, cq)),
                  pl.BlockSpec((seq, NA_W), lambda b: (b, ck)),
                  pl.BlockSpec((seq, NA_W), lambda b: (b, cv)),
                  pl.BlockSpec((1, NA_HD), lambda b: (0, 0)),
                  pl.BlockSpec((1, NA_HD), lambda b: (0, 0))],
        out_specs=[pl.BlockSpec((seq, NA_W), lambda b: (b, 0)), cache_spec, cache_spec],
        out_shape=[jax.ShapeDtypeStruct((t, NA_W), F32), cache_shape, cache_shape],
        compiler_params=_cparams("parallel"),
        name="ctx_attention",
    )(proj, proj, proj, gq, gk)


def _na_kernel(q_ref, k_ref, v_ref, kc_ref, vc_ref, gq_ref, gk_ref, tab_ref, o_ref, qn_s, kn_s, v_s, *, rows):
    scale = NA_HD ** -0.5
    qn_s[...] = (_head_norm(q_ref[...], gq_ref[...]) * scale).astype(BF16)
    kn_s[...] = _head_norm(k_ref[...], gk_ref[...]).astype(BF16)
    v_s[...] = v_ref[...].astype(BF16)
    kctx = kc_ref[0, 0, 0].astype(BF16)
    vctx = vc_ref[0, 0, 0].astype(BF16)
    nblk = rows // NA_QROWS
    qlen, win = NA_QROWS * GRID_W, NA_UNION * GRID_W

    def scores(g):
        u0 = jnp.clip(g * NA_QROWS - NA_KH // 2, 0, rows - NA_UNION)
        cls = jnp.where(g == 0, 0, jnp.where(g == nblk - 1, 2, 1))
        qrows = pl.ds(pl.multiple_of(g * qlen, qlen), qlen)
        wrows = pl.ds(pl.multiple_of(u0 * GRID_W, GRID_W), win)
        qn = qn_s[qrows, :]
        s_loc = lax.dot_general(qn, kn_s[wrows, :], (((1,), (1,)), ((), ())), preferred_element_type=F32)
        s_ctx = lax.dot_general(qn, kctx, (((1,), (1,)), ((), ())), preferred_element_type=F32)
        return qrows, wrows, s_loc + tab_ref[0, cls], s_ctx

    def softmax(s_loc, s_ctx):
        m = jnp.maximum(jnp.max(s_loc, axis=-1, keepdims=True), jnp.max(s_ctx, axis=-1, keepdims=True))
        e_loc = jnp.exp(s_loc - m)
        e_ctx = jnp.exp(s_ctx - m)
        den = jnp.sum(e_loc, axis=-1, keepdims=True) + jnp.sum(e_ctx, axis=-1, keepdims=True)
        return e_loc.astype(BF16), e_ctx.astype(BF16), den

    def body(step, carry):
        blocks = [scores(step * NA_BLOCK_UNROLL + u) for u in range(NA_BLOCK_UNROLL)]
        probs = [softmax(s_loc, s_ctx) for _, _, s_loc, s_ctx in blocks]
        for (qrows, wrows, _, _), (e_loc, e_ctx, den) in zip(blocks, probs):
            acc = jnp.dot(e_loc, v_s[wrows, :], preferred_element_type=F32)
            acc = acc + jnp.dot(e_ctx, vctx, preferred_element_type=F32)
            o_ref[qrows, :] = acc / den
        return carry

    lax.fori_loop(0, nblk // NA_BLOCK_UNROLL, body, 0)


def neighbourhood_attention(proj, cache_k, cache_v, gq, gk, table, *, batch, seq, layer, row0):
    rows = seq // GRID_W
    assert rows % NA_QROWS == 0 and rows // NA_QROWS >= 3 and rows >= NA_UNION
    past = cache_k.shape[3]
    cq, ck, cv = OFF_NQ // NA_HD, OFF_NK // NA_HD, OFF_NV // NA_HD
    cache_spec = pl.BlockSpec((1, 1, 1, past, NA_HD), lambda b, h: (b, layer, h, 0, 0))
    return pl.pallas_call(
        functools.partial(_na_kernel, rows=rows),
        grid=(batch, NA_HEADS),
        in_specs=[pl.BlockSpec((seq, NA_HD), lambda b, h: (row0 + b, cq + h)),
                  pl.BlockSpec((seq, NA_HD), lambda b, h: (row0 + b, ck + h)),
                  pl.BlockSpec((seq, NA_HD), lambda b, h: (row0 + b, cv + h)),
                  cache_spec, cache_spec,
                  pl.BlockSpec((1, NA_HD), lambda b, h: (0, 0)),
                  pl.BlockSpec((1, NA_HD), lambda b, h: (0, 0)),
                  pl.BlockSpec((1, 3, NA_QROWS * GRID_W, NA_UNION * GRID_W), lambda b, h: (h, 0, 0, 0))],
        out_specs=pl.BlockSpec((seq, NA_HD), lambda b, h: (b, h)),
        out_shape=jax.ShapeDtypeStruct((batch * seq, NA_W), F32),
        scratch_shapes=[pltpu.VMEM((seq, NA_HD), BF16)] * 3,
        compiler_params=_cparams("parallel", "parallel"),
        name="neighbourhood_attention",
    )(proj, proj, proj, cache_k, cache_v, gq, gk, table)


def na_bias_table(rpb, rows):
    nblk = rows // NA_QROWS
    col = np.arange(GRID_W)
    c_start = np.clip(col - NA_KW // 2, 0, GRID_W - NA_KW)
    col_in = (col[None, :] >= c_start[:, None]) & (col[None, :] < c_start[:, None] + NA_KW)
    col_idx = np.clip(col[None, :] - col[:, None] + NA_KW - 1, 0, 2 * NA_KW - 2)
    row_off, row_in = [], []
    for g in (0, 1, nblk - 1):
        u0 = np.clip(g * NA_QROWS - NA_KH // 2, 0, rows - NA_UNION)
        r = g * NA_QROWS + np.arange(NA_QROWS)[:, None]
        r_start = np.clip(r - NA_KH // 2, 0, rows - NA_KH)
        key_row = u0 + np.arange(NA_UNION)[None, :]
        row_in.append((key_row >= r_start) & (key_row < r_start + NA_KH))
        row_off.append(np.clip(key_row - r + NA_KH - 1, 0, 2 * NA_KH - 2))
    row_off, row_in = np.stack(row_off), np.stack(row_in)
    tab = rpb[:, row_off][..., col_idx]
    keep = row_in[None, :, :, :, None, None] & col_in[None, None, None, None]
    tab = jnp.where(keep, tab, -jnp.inf)
    tab = jnp.transpose(tab, (0, 1, 2, 4, 3, 5))
    return tab.reshape(rpb.shape[0], 3, NA_QROWS * GRID_W, NA_UNION * GRID_W).astype(F32)


def _filter_kernel(z_ref, f1w_ref, f1b_ref, f2w_ref, f2b_ref, f3w_ref, fr_ref, dl_ref, o_ref, *, seq):
    fr = fr_ref[...]
    h = jnp.sin(fr * (jnp.dot(z_ref[...], f1w_ref[...], preferred_element_type=F32, precision=HIGHEST)
                      + f1b_ref[...]))
    h = jnp.sin(fr * (jnp.dot(h, f2w_ref[...], preferred_element_type=F32, precision=HIGHEST)
                      + f2b_ref[...]))
    h = jnp.dot(h, f3w_ref[...], preferred_element_type=F32, precision=HIGHEST)
    t = lax.broadcasted_iota(jnp.int32, (seq, 1), 0).astype(F32)
    dist = jnp.abs(t - float(seq // 2)) * (2.0 / seq)
    h = h * jnp.exp(-dist * dl_ref[...])
    o_ref[...] = h * lax.rsqrt(jnp.sum(h * h, axis=0, keepdims=True) + NORM_EPS)


def hyena_filters(zpos, f1w, f1b, f2w, f2b, f3w, freq, deltas, *, seq):
    n = f3w.shape[1]
    tc = 512
    full = lambda a: pl.BlockSpec(a.shape, lambda j: (0,) * a.ndim)
    return pl.pallas_call(
        functools.partial(_filter_kernel, seq=seq),
        grid=(n // tc,),
        in_specs=[full(zpos), full(f1w), full(f1b), full(f2w), full(f2b),
                  pl.BlockSpec((HY_HIDDEN, tc), lambda j: (0, j)), full(freq),
                  pl.BlockSpec((1, tc), lambda j: (0, j))],
        out_specs=pl.BlockSpec((seq, tc), lambda j: (0, j)),
        out_shape=jax.ShapeDtypeStruct((seq, n), F32),
        compiler_params=_cparams("parallel"),
        name="hyena_filters",
    )(zpos, f1w, f1b, f2w, f2b, f3w, freq, deltas)


def _spectrum_kernel(a_ref, h_ref, o_ref, *, seq, tkb):
    hs = jnp.dot(a_ref[0], h_ref[...].astype(BF16), preferred_element_type=F32)
    hc, hsn = hs[:tkb], hs[tkb:]
    k = pl.program_id(1) * tkb + lax.broadcasted_iota(jnp.int32, (tkb, 1), 0)
    km = k % 4
    gr = jnp.where(km == 0, hc, jnp.where(km == 1, hsn, jnp.where(km == 2, -hc, -hsn))) * (1.0 / seq)
    gi = jnp.where(km == 0, -hsn, jnp.where(km == 1, hc, jnp.where(km == 2, hsn, -hc))) * (1.0 / seq)
    dc = k == 0
    o_ref[0] = jnp.where(dc, hc * (0.5 / seq), gr)
    o_ref[1] = jnp.where(dc, 0.0, gi)
    o_ref[2] = jnp.where(dc, hsn * (0.5 / seq), gr)


def filter_spectrum(dft_a, filt, *, seq, tkb):
    n = filt.shape[1]
    tc = 1024
    nkb = seq // tkb
    return pl.pallas_call(
        functools.partial(_spectrum_kernel, seq=seq, tkb=tkb),
        grid=(n // tc, nkb),
        in_specs=[pl.BlockSpec((1, 2 * tkb, seq), lambda j, kb: (kb, 0, 0)),
                  pl.BlockSpec((seq, tc), lambda j, kb: (0, j))],
        out_specs=pl.BlockSpec((3, tkb, tc), lambda j, kb: (0, kb, j)),
        out_shape=jax.ShapeDtypeStruct((3, seq, n), F32),
        compiler_params=_cparams("parallel", "arbitrary"),
        name="filter_spectrum",
    )(dft_a, filt)


def _short_conv(u, w, bias, seq):
    t = lax.broadcasted_iota(jnp.int32, (seq, 1), 0)
    prev = jnp.where(t == 0, 0.0, pltpu.roll(u, 1, 0))
    nxt = jnp.where(t == seq - 1, 0.0, pltpu.roll(u, seq - 1, 0))
    return bias + prev * w[0:1, :] + u * w[1:2, :] + nxt * w[2:3, :]


def _hy_pre_kernel(u_ref, w_ref, b_ref, z_ref, *, seq):
    z_ref[...] = _short_conv(u_ref[...], w_ref[...], b_ref[...], seq).astype(BF16)


def _hy_gate_kernel(ux_ref, uz_ref, wx_ref, bx_ref, wz_ref, bz_ref, cv_ref, hb_ref, zo_ref, zb_ref, *, seq):
    x = _short_conv(ux_ref[...], wx_ref[...], bx_ref[...], seq)
    z = _short_conv(uz_ref[...], wz_ref[...], bz_ref[...], seq)
    out = x * (cv_ref[...] + hb_ref[...] * z)
    zo_ref[...] = out
    zb_ref[...] = out.astype(BF16)


def _hy_final_kernel(ux_ref, wx_ref, bx_ref, cv_ref, hb_ref, z_ref, zb_ref, *, seq):
    x = _short_conv(ux_ref[...], wx_ref[...], bx_ref[...], seq)
    zb_ref[...] = (x * (cv_ref[...] + hb_ref[...] * z_ref[...])).astype(BF16)


HY_TC_ELEM = 128


def _hy_specs(seq, row0, col0):
    cb = col0 // HY_TC_ELEM
    u = pl.BlockSpec((seq, HY_TC_ELEM), lambda b, j: (row0 + b, cb + j))
    w = pl.BlockSpec((3, HY_TC_ELEM), lambda b, j: (0, (col0 - OFF_HY) // HY_TC_ELEM + j))
    bb = pl.BlockSpec((1, HY_TC_ELEM), lambda b, j: (0, (col0 - OFF_HY) // HY_TC_ELEM + j))
    return u, w, bb


def hyena_pre(proj, conv_w, conv_b, *, batch, seq, row0):
    u, w, bb = _hy_specs(seq, row0, OFF_HY + HY_ORDER * HY_W)
    act = pl.BlockSpec((seq, HY_TC_ELEM), lambda b, j: (b, j))
    return pl.pallas_call(
        functools.partial(_hy_pre_kernel, seq=seq),
        grid=(batch, HY_W // HY_TC_ELEM),
        in_specs=[u, w, bb], out_specs=act,
        out_shape=jax.ShapeDtypeStruct((batch * seq, HY_W), BF16),
        compiler_params=_cparams("parallel", "parallel"), name="hyena_pre",
    )(proj, conv_w, conv_b)


def hyena_gate(proj, conv_w, conv_b, conv, hy_bias, *, batch, seq, row0):
    ux, wx, bx = _hy_specs(seq, row0, OFF_HY)
    uz, wz, bz = _hy_specs(seq, row0, OFF_HY + HY_ORDER * HY_W)
    act = pl.BlockSpec((seq, HY_TC_ELEM), lambda b, j: (b, j))
    hb = pl.BlockSpec((1, HY_TC_ELEM), lambda b, j: (0, j))
    return pl.pallas_call(
        functools.partial(_hy_gate_kernel, seq=seq),
        grid=(batch, HY_W // HY_TC_ELEM),
        in_specs=[ux, uz, wx, bx, wz, bz, act, hb], out_specs=[act, act],
        out_shape=[jax.ShapeDtypeStruct((batch * seq, HY_W), F32),
                   jax.ShapeDtypeStruct((batch * seq, HY_W), BF16)],
        compiler_params=_cparams("parallel", "parallel"), name="hyena_gate",
    )(proj, proj, conv_w, conv_b, conv_w, conv_b, conv, hy_bias[0:1])


def hyena_final(proj, conv_w, conv_b, conv, hy_bias, z1, *, batch, seq, row0):
    ux, wx, bx = _hy_specs(seq, row0, OFF_HY + HY_W)
    act = pl.BlockSpec((seq, HY_TC_ELEM), lambda b, j: (b, j))
    hb = pl.BlockSpec((1, HY_TC_ELEM), lambda b, j: (0, j))
    return pl.pallas_call(
        functools.partial(_hy_final_kernel, seq=seq),
        grid=(batch, HY_W // HY_TC_ELEM),
        in_specs=[ux, wx, bx, act, hb, act], out_specs=act,
        out_shape=jax.ShapeDtypeStruct((batch * seq, HY_W), BF16),
        compiler_params=_cparams("parallel", "parallel"), name="hyena_final",
    )(proj, conv_w, conv_b, conv, hy_bias[1:2], z1)


def _long_conv_kernel(z_ref, a_ref, bt_ref, g_ref, o_ref, *, tkb, seq, trb):
    kb = pl.program_id(2)

    @pl.when(kb == 0)
    def _():
        o_ref[...] = jnp.zeros(o_ref.shape, F32)

    zs = jnp.dot(a_ref[0], z_ref[...], preferred_element_type=F32)
    zr, zi = zs[:tkb], zs[tkb:]
    gr, gi, gq = g_ref[0], g_ref[1], g_ref[2]
    p = zr * gr + zi * gi
    q = zi * gq - zr * gi
    pq = jnp.concatenate([p, q], axis=0).astype(BF16)

    def body(rb, carry):
        rows = pl.ds(pl.multiple_of(rb * trb, trb), trb)
        o_ref[rows, :] += jnp.dot(bt_ref[0, rows, :], pq, preferred_element_type=F32)
        return carry

    lax.fori_loop(0, seq // trb, body, 0)


def hyena_long_conv(z, dft_a, dft_bt, g, *, batch, seq, order, tkb):
    c = z.shape[1]
    tc = 512
    nkb = seq // tkb
    trb = min(seq, 512)
    goff = order * (c // tc)
    return pl.pallas_call(
        functools.partial(_long_conv_kernel, tkb=tkb, seq=seq, trb=trb),
        grid=(batch, c // tc, nkb),
        in_specs=[pl.BlockSpec((seq, tc), lambda b, j, kb: (b, j)),
                  pl.BlockSpec((1, 2 * tkb, seq), lambda b, j, kb: (kb, 0, 0)),
                  pl.BlockSpec((1, seq, 2 * tkb), lambda b, j, kb: (kb, 0, 0)),
                  pl.BlockSpec((3, tkb, tc), lambda b, j, kb: (0, kb, goff + j))],
        out_specs=pl.BlockSpec((seq, tc), lambda b, j, kb: (b, j)),
        out_shape=jax.ShapeDtypeStruct((batch * seq, c), F32),
        compiler_params=_cparams("parallel", "parallel", "arbitrary"),
        name="hyena_long_conv",
    )(z, dft_a, dft_bt, g)


def dft_tables(seq, tkb):
    hi = max(seq // 256, 1)
    lo = seq // hi
    m = jnp.arange(seq, dtype=jnp.int32)

    def cs(kvals):
        r = (kvals[:, None] * m[None, :]) % (2 * seq)
        ang = r.astype(F32) * (math.pi / seq)
        return jnp.cos(ang), jnp.sin(ang)

    c1, s1 = cs(jnp.arange(hi, dtype=jnp.int32) * lo)
    c2, s2 = cs(jnp.arange(lo, dtype=jnp.int32))
    cm = (c1[:, None] * c2[None] - s1[:, None] * s2[None]).reshape(seq, seq)
    sm = (s1[:, None] * c2[None] + c1[:, None] * s2[None]).reshape(seq, seq)
    nyq = jnp.where(m % 2 == 0, 1.0, -1.0).astype(F32)
    sm = jnp.where(jnp.arange(seq)[:, None] == 0, nyq[None, :], sm)
    nkb = seq // tkb
    a = jnp.concatenate([cm.reshape(nkb, tkb, seq), sm.reshape(nkb, tkb, seq)], axis=1).astype(BF16)
    return a, jnp.transpose(a, (0, 2, 1))


def position_features(seq):
    t = jnp.arange(seq, dtype=F32)
    tn = t / seq
    bands = jnp.arange(1, HY_BANDS + 1, dtype=F32)
    ang = 2.0 * jnp.pi * tn[:, None] * bands[None, :]
    z = jnp.concatenate([tn[:, None], jnp.cos(ang), jnp.sin(ang)], axis=-1)
    return jnp.pad(z, ((0, 0), (0, LANES - z.shape[1])))


def rope_tables(seq):
    n_freq = GLA_DK // 4
    pos = jnp.arange(seq)
    row = (pos // GRID_W).astype(F32)
    col = (pos % GRID_W).astype(F32)
    inv = ROPE_BASE ** (-jnp.arange(n_freq, dtype=F32) / n_freq)
    ang = jnp.concatenate([row[:, None] * inv, col[:, None] * inv], axis=-1)
    cos, sin = jnp.cos(ang), jnp.sin(ang)
    return jnp.concatenate([cos, cos], axis=-1), jnp.concatenate([-sin, sin], axis=-1)


def _merge_kernel(of_ref, ob_ref, gg_ref, gn_ref, nb_ref, hc_ref, ga_ref, gb_ref, gc_ref,
                  wa_ref, wb_ref, wc_ref, o_ref, a_s):
    @pl.when(pl.program_id(1) == 0)
    def _():
        for h in range(GLA_HEADS):
            cols = slice(h * GLA_DV, (h + 1) * GLA_DV)
            oa = of_ref[:, cols] + ob_ref[:, cols]
            oa = _head_norm(oa, gn_ref[...])
            gg = gg_ref[:, cols]
            a_s[:, cols] = (oa * (gg * jax.nn.sigmoid(gg))).astype(BF16)

    ya = jnp.dot(a_s[...], wa_ref[...], preferred_element_type=F32)
    yb = jnp.dot(nb_ref[...].astype(BF16), wb_ref[...], preferred_element_type=F32)
    yc = jnp.dot(hc_ref[...], wc_ref[...], preferred_element_type=F32)
    merged = (jax.nn.sigmoid(ga_ref[...]) * ya + jax.nn.sigmoid(gb_ref[...]) * yb
              + jax.nn.sigmoid(gc_ref[...]) * yc)
    o_ref[...] = merged.astype(o_ref.dtype)


def branch_merge(o_f, o_b, proj, gnorm, o_attn, o_hy, w_a, w_b, w_c, *, row0_blocks, tm, tn):
    t = o_f.shape[0]
    d = w_a.shape[1]
    assert OFF_GATES % tn == 0 and d % tn == 0
    act = lambda w: pl.BlockSpec((tm, w), lambda i, j: (i, 0))
    gate = lambda g: pl.BlockSpec((tm, tn), lambda i, j: (row0_blocks + i, (OFF_GATES + g * d) // tn + j))
    wspec = lambda w: pl.BlockSpec((w.shape[0], tn), lambda i, j: (0, j))
    return pl.pallas_call(
        _merge_kernel,
        grid=(t // tm, d // tn),
        in_specs=[act(VAL_W), act(VAL_W),
                  pl.BlockSpec((tm, VAL_W), lambda i, j: (row0_blocks + i, OFF_GG // VAL_W)),
                  pl.BlockSpec((1, GLA_DV), lambda i, j: (0, 0)),
                  act(NA_W), act(HY_W), gate(0), gate(1), gate(2),
                  wspec(w_a), wspec(w_b), wspec(w_c)],
        out_specs=pl.BlockSpec((tm, tn), lambda i, j: (i, j)),
        out_shape=jax.ShapeDtypeStruct((t, d), BF16),
        scratch_shapes=[pltpu.VMEM((tm, VAL_W), BF16)],
        compiler_params=_cparams("parallel", "arbitrary"),
        name="branch_merge",
    )(o_f, o_b, proj, gnorm, o_attn, o_hy, proj, proj, proj, w_a, w_b, w_c)


def _matmul_residual_kernel(a_ref, w_ref, g_ref, r_ref, o_ref):
    y = jnp.dot(a_ref[...], w_ref[...], preferred_element_type=F32)
    o_ref[...] = r_ref[...] + g_ref[0] * y


def matmul_residual(a, w, gate, resid, *, rows_per_mod, tm, tn):
    t, k = a.shape
    n = w.shape[1]
    return pl.pallas_call(
        _matmul_residual_kernel,
        grid=(t // tm, n // tn),
        in_specs=[pl.BlockSpec((tm, k), lambda i, j: (i, 0)),
                  pl.BlockSpec((k, tn), lambda i, j: (0, j)),
                  pl.BlockSpec((1, 1, tn), lambda i, j: ((i * tm) // rows_per_mod, 0, j)),
                  pl.BlockSpec((tm, tn), lambda i, j: (i, j))],
        out_specs=pl.BlockSpec((tm, tn), lambda i, j: (i, j)),
        out_shape=jax.ShapeDtypeStruct((t, n), F32),
        compiler_params=_cparams("parallel", "arbitrary"),
        name="matmul_residual",
    )(a, w, gate, resid)


def _trunk_layer(x, mods, p, consts, *, batch, seq, layer, latent):
    sh1, sc1, g1, sh2, sc2, g2 = mods
    t = x.shape[0]
    rows_per_mod = t // sh1.shape[0]
    tm = 512

    proj = norm_matmul(x, p["norm1_g"], sc1, sh1, p["w_in_main"], rows_per_mod=rows_per_mod,
                       tm=tm, tn=1536, relu2=False, out_dtype=F32)
    la = decay_gate(x, p["norm1_g"], sc1, sh1, p["w_in_ga"], p["wa2_bd"], p["ba2"],
                    rows_per_mod=rows_per_mod, tm=tm)

    rows_blk = min(seq, 512)
    rope = consts["rope"] if latent else None
    s0 = consts["state_gla"] if latent else None
    o_f, s_f = gla_direction(proj, la, rope, s0, batch=batch, seq=seq, reverse=False, layer=layer,
                             direction=0, rows_blk=rows_blk)
    o_b, s_b = gla_direction(proj, la, rope, s0, batch=batch, seq=seq, reverse=True, layer=layer,
                             direction=1, rows_blk=rows_blk)

    if latent:
        o_attn = neighbourhood_attention(proj, consts["cache_k"], consts["cache_v"], p["na_qnorm_g"],
                                         p["na_knorm_g"], p["na_table"], batch=batch, seq=seq,
                                         layer=layer, row0=0)
        new_ctx = None
    else:
        o_attn, kn, vn = ctx_attention(proj, p["na_qnorm_g"], p["na_knorm_g"], batch=batch, seq=seq)
        new_ctx = (jnp.stack([s_f, s_b], axis=1), kn, vn)

    key = "lat" if latent else "ctx"
    dft_a, dft_bt = consts["dft_" + key]
    tkb = consts["tkb_" + key]
    g_spec = p["spectrum_" + key]
    z0 = hyena_pre(proj, p["hy_conv_w"], p["hy_conv_b"], batch=batch, seq=seq, row0=0)
    cv0 = hyena_long_conv(z0, dft_a, dft_bt, g_spec, batch=batch, seq=seq, order=0, tkb=tkb)
    z1, z1b = hyena_gate(proj, p["hy_conv_w"], p["hy_conv_b"], cv0, p["hy_bias"], batch=batch, seq=seq, row0=0)
    cv1 = hyena_long_conv(z1b, dft_a, dft_bt, g_spec, batch=batch, seq=seq, order=1, tkb=tkb)
    o_hy = hyena_final(proj, p["hy_conv_w"], p["hy_conv_b"], cv1, p["hy_bias"], z1, batch=batch, seq=seq, row0=0)

    merged = branch_merge(o_f, o_b, proj, p["gla_norm_g"], o_attn, o_hy, p["w_br_a"], p["w_br_b"],
                          p["w_br_c"], row0_blocks=0, tm=256, tn=2048)
    x = matmul_residual(merged, p["w_out"], g1, x, rows_per_mod=rows_per_mod, tm=tm, tn=2048)

    u = norm_matmul(x, p["norm2_g"], sc2, sh2, p["w_mlp1"], rows_per_mod=rows_per_mod,
                    tm=tm, tn=2048, relu2=True, out_dtype=BF16)
    x = matmul_residual(u, p["w_mlp2"], g2, x, rows_per_mod=rows_per_mod, tm=tm, tn=512)
    return x, new_ctx


def _reorder_w_in(w_in):
    o_ga = 2 * KEY_W + 2 * VAL_W
    n_ga = 2 * GLA_LOWRANK
    o_hy = o_ga + n_ga + 3 * NA_W
    o_gates = o_hy + (HY_ORDER + 1) * HY_W
    main = jnp.concatenate([w_in[..., :o_ga], w_in[..., o_ga + n_ga:o_hy], w_in[..., o_gates:],
                            w_in[..., o_hy:o_gates]], axis=-1).astype(BF16)
    ga = jnp.pad(w_in[..., o_ga:o_ga + n_ga], ((0, 0), (0, 0), (0, LANES - n_ga))).astype(BF16)
    return main, ga


def kernel(x_prompt, x_sample, state_gla, cache_na_k, cache_na_v, c, c_ctx, w_mod, b_mod, norm1_g, norm2_g, w_in, gla_wa2, gla_ba2, gla_norm_g, na_qnorm_g, na_knorm_g, na_rpb, hy_conv_w, hy_conv_b, hy_f1_w, hy_f1_b, hy_f2_w, hy_f2_b, hy_f3_w, hy_freq, hy_bias, w_br_a, w_br_b, w_br_c, w_out, w_mlp1, w_mlp2):
    batch, seq, d = x_prompt.shape
    dec_batch, dec_seq, _ = x_sample.shape
    depth = w_mod.shape[0]

    w_in_main, w_in_ga = _reorder_w_in(w_in)
    bf = lambda w: w.astype(BF16)
    w_br_a_b, w_br_b_b, w_br_c_b, w_out_b, w_mlp1_b, w_mlp2_b = map(bf, (w_br_a, w_br_b, w_br_c, w_out, w_mlp1, w_mlp2))
    wa2_bd = jnp.zeros((depth, LANES, 2 * KEY_W), F32)
    wa2_bd = wa2_bd.at[:, :GLA_LOWRANK, :KEY_W].set(gla_wa2[:, 0])
    wa2_bd = wa2_bd.at[:, GLA_LOWRANK:2 * GLA_LOWRANK, KEY_W:].set(gla_wa2[:, 1])
    ba2 = gla_ba2.reshape(depth, 1, 2 * KEY_W)
    f1w = jnp.pad(hy_f1_w, ((0, 0), (0, LANES - hy_f1_w.shape[1]), (0, 0)))

    tkb_ctx, tkb_lat = min(seq, 256), min(dec_seq, 256)
    consts = {
        "rope": rope_tables(dec_seq),
        "state_gla": state_gla, "cache_k": cache_na_k, "cache_v": cache_na_v,
        "dft_ctx": dft_tables(seq, tkb_ctx), "dft_lat": dft_tables(dec_seq, tkb_lat),
        "tkb_ctx": tkb_ctx, "tkb_lat": tkb_lat,
    }
    deltas = jnp.abs(jnp.linspace(math.log(HY_DECAY_TARGET) / HY_DECAY_PCT_LONG,
                                  math.log(HY_DECAY_TARGET) / HY_DECAY_PCT_SHORT, HY_W, dtype=F32))
    deltas = jnp.tile(deltas, HY_ORDER)[None, :]
    zpos = {"ctx": position_features(seq), "lat": position_features(dec_seq)}

    cond = jnp.zeros((16, d), F32).at[0].set(c_ctx).at[1:1 + dec_batch].set(c)

    xp = x_prompt.reshape(batch * seq, d)
    xs = x_sample.reshape(dec_batch * dec_seq, d)
    new_gla, new_k, new_v = [], [], []
    for l in range(depth):
        m = adaln(cond, w_mod[l], b_mod[l][None, :])
        mods_ctx = tuple(m[0:1, i * d:(i + 1) * d][:, None, :] for i in range(N_MOD))
        mods_lat = tuple(m[1:1 + dec_batch, i * d:(i + 1) * d][:, None, :] for i in range(N_MOD))
        p = dict(norm1_g=norm1_g[l][None], norm2_g=norm2_g[l][None], w_in_main=w_in_main[l], w_in_ga=w_in_ga[l],
                 wa2_bd=wa2_bd[l], ba2=ba2[l], gla_norm_g=gla_norm_g[l][None], na_qnorm_g=na_qnorm_g[l][None],
                 na_knorm_g=na_knorm_g[l][None], na_table=na_bias_table(na_rpb[l], dec_seq // GRID_W),
                 hy_conv_w=hy_conv_w[l], hy_conv_b=hy_conv_b[l][None], hy_bias=hy_bias[l],
                 w_br_a=w_br_a_b[l], w_br_b=w_br_b_b[l], w_br_c=w_br_c_b[l], w_out=w_out_b[l],
                 w_mlp1=w_mlp1_b[l], w_mlp2=w_mlp2_b[l])
        for key, length in (("ctx", seq), ("lat", dec_seq)):
            filt = hyena_filters(zpos[key], f1w[l], hy_f1_b[l][None], hy_f2_w[l], hy_f2_b[l][None],
                                 hy_f3_w[l], hy_freq[l][None], deltas, seq=length)
            p["spectrum_" + key] = filter_spectrum(consts["dft_" + key][0], filt, seq=length,
                                                   tkb=consts["tkb_" + key])
        xp, (s_gla, k_ctx, v_ctx) = _trunk_layer(xp, mods_ctx, p, consts, batch=batch, seq=seq, layer=l, latent=False)
        new_gla.append(s_gla)
        new_k.append(k_ctx)
        new_v.append(v_ctx)
        xs, _ = _trunk_layer(xs, mods_lat, p, consts, batch=dec_batch, seq=dec_seq, layer=l, latent=True)
    return (xp.reshape(batch, seq, d), xs.reshape(dec_batch, dec_seq, d),
            jnp.stack(new_gla, axis=1), jnp.stack(new_k, axis=1), jnp.stack(new_v, axis=1))
```

```python
import functools
import math

import jax
import jax.numpy as jnp
import numpy as np
from jax import lax
from jax.experimental import pallas as pl
from jax.experimental.pallas import tpu as pltpu

F32 = jnp.float32
BF16 = jnp.bfloat16
HIGHEST = lax.Precision.HIGHEST

GRID_W = 64
N_MOD = 6
NORM_EPS = 1e-6
GLA_HEADS = 4
GLA_DK = 128
GLA_DV = 256
GLA_LOWRANK = 16
GLA_TAU = 16.0
GLA_CHUNK = 64
GLA_SUB = 16
GLA_SAFE_EXP = 60.0
ROPE_BASE = 10000.0
NA_HEADS = 8
NA_HD = 128
NA_KH = 8
NA_KW = 16
NA_QROWS = 4
NA_UNION = NA_KH + NA_QROWS
NA_BLOCK_UNROLL = 2
HY_W = 1024
HY_ORDER = 2
HY_BANDS = 16
HY_HIDDEN = 64
HY_DECAY_TARGET = 1e-2
HY_DECAY_PCT_SHORT = 0.3
HY_DECAY_PCT_LONG = 1.5
HY_RADIX = 4
HY_KC = 256
HY_TC_FFT = 256

V7X_VMEM_LIMIT_BYTES = 56 * 1024 * 1024
LANES = 128

KEY_W = GLA_HEADS * GLA_DK
VAL_W = GLA_HEADS * GLA_DV
NA_W = NA_HEADS * NA_HD
OFF_GQ = 0
OFF_GK = OFF_GQ + KEY_W
OFF_GV = OFF_GK + KEY_W
OFF_GG = OFF_GV + VAL_W
OFF_NQ = OFF_GG + VAL_W
OFF_NK = OFF_NQ + NA_W
OFF_NV = OFF_NK + NA_W
OFF_GATES = OFF_NV + NA_W
OFF_HY = OFF_GATES + 3 * 2048


def _cparams(*sem):
    return pltpu.CompilerParams(dimension_semantics=sem, vmem_limit_bytes=V7X_VMEM_LIMIT_BYTES)


def _modulated_norm(x, g, scale, shift):
    ms = jnp.mean(x * x, axis=-1, keepdims=True)
    y = x * lax.rsqrt(ms + NORM_EPS) * g
    return y * (1.0 + scale) + shift


def _adaln_kernel(c_ref, w_ref, b_ref, o_ref):
    c = c_ref[...]
    a = (c * jax.nn.sigmoid(c)).astype(BF16)
    o_ref[...] = jnp.dot(a, w_ref[...].astype(BF16), preferred_element_type=F32) + b_ref[...]


def adaln(cond, w_mod, b_mod, layer):
    m, d = cond.shape
    n = w_mod.shape[2]
    tn = 1024
    return pl.pallas_call(
        _adaln_kernel,
        grid=(n // tn,),
        in_specs=[pl.BlockSpec((m, d), lambda j: (0, 0)),
                  pl.BlockSpec((None, d, tn), lambda j: (layer, 0, j)),
                  pl.BlockSpec((1, tn), lambda j: (0, j))],
        out_specs=pl.BlockSpec((m, tn), lambda j: (0, j)),
        out_shape=jax.ShapeDtypeStruct((m, n), F32),
        compiler_params=_cparams("parallel"),
        name="adaln",
    )(cond, w_mod, b_mod)


def _norm_matmul_kernel(x_ref, g_ref, sc_ref, sh_ref, w_ref, o_ref, h_ref, *, relu2):
    @pl.when(pl.program_id(1) == 0)
    def _():
        h = _modulated_norm(x_ref[...], g_ref[...], sc_ref[0], sh_ref[0])
        h_ref[...] = h.astype(BF16)

    acc = jnp.dot(h_ref[...], w_ref[...], preferred_element_type=F32)
    if relu2:
        acc = jnp.square(jnp.maximum(acc, 0.0))
    o_ref[...] = acc.astype(o_ref.dtype)


def norm_matmul(x, g, scale, shift, w, *, rows_per_mod, tm, tn, relu2, out_dtype):
    t, d = x.shape
    n = w.shape[1]
    mod_map = lambda i, j: ((i * tm) // rows_per_mod, 0, 0)
    return pl.pallas_call(
        functools.partial(_norm_matmul_kernel, relu2=relu2),
        grid=(t // tm, n // tn),
        in_specs=[pl.BlockSpec((tm, d), lambda i, j: (i, 0)),
                  pl.BlockSpec((1, d), lambda i, j: (0, 0)),
                  pl.BlockSpec((1, 1, d), mod_map),
                  pl.BlockSpec((1, 1, d), mod_map),
                  pl.BlockSpec((d, tn), lambda i, j: (0, j))],
        out_specs=pl.BlockSpec((tm, tn), lambda i, j: (i, j)),
        out_shape=jax.ShapeDtypeStruct((t, n), out_dtype),
        scratch_shapes=[pltpu.VMEM((tm, d), BF16)],
        compiler_params=_cparams("parallel", "arbitrary"),
        name="norm_matmul_relu2" if relu2 else "norm_matmul",
    )(x, g, scale, shift, w)


def _decay_gate_kernel(x_ref, g_ref, sc_ref, sh_ref, wga_ref, wa2_ref, ba2_ref, o_ref):
    h = _modulated_norm(x_ref[...], g_ref[...], sc_ref[0], sh_ref[0]).astype(BF16)
    ga = jnp.dot(h, wga_ref[...], preferred_element_type=F32)
    pre = jnp.dot(ga, wa2_ref[...], preferred_element_type=F32, precision=HIGHEST) + ba2_ref[...]
    log_sig = jnp.minimum(pre, 0.0) - jnp.log(1.0 + jnp.exp(-jnp.abs(pre)))
    o_ref[...] = log_sig * (1.0 / GLA_TAU)


def decay_gate(x, g, scale, shift, w_ga, wa2_bd, ba2, *, rows_per_mod, tm):
    t, d = x.shape
    n = wa2_bd.shape[1]
    mod_map = lambda i: ((i * tm) // rows_per_mod, 0, 0)
    return pl.pallas_call(
        _decay_gate_kernel,
        grid=(t // tm,),
        in_specs=[pl.BlockSpec((tm, d), lambda i: (i, 0)),
                  pl.BlockSpec((1, d), lambda i: (0, 0)),
                  pl.BlockSpec((1, 1, d), mod_map),
                  pl.BlockSpec((1, 1, d), mod_map),
                  pl.BlockSpec((d, LANES), lambda i: (0, 0)),
                  pl.BlockSpec((LANES, n), lambda i: (0, 0)),
                  pl.BlockSpec((1, n), lambda i: (0, 0))],
        out_specs=pl.BlockSpec((tm, n), lambda i: (i, 0)),
        out_shape=jax.ShapeDtypeStruct((t, n), F32),
        compiler_params=_cparams("parallel"),
        name="decay_gate",
    )(x, g, scale, shift, w_ga, wa2_bd, ba2)


def _cumulative_log_decay(la, reverse):
    c = la.shape[0]
    row = lax.broadcasted_iota(jnp.int32, (c, c), 0)
    col = lax.broadcasted_iota(jnp.int32, (c, c), 1)
    tri = jnp.where((row <= col) if reverse else (row >= col), 1.0, 0.0).astype(BF16)
    hi = la.astype(BF16)
    rest = la - hi.astype(F32)
    mid = rest.astype(BF16)
    lo = (rest - mid.astype(F32)).astype(BF16)
    dot = lambda t: jnp.dot(tri, t, preferred_element_type=F32)
    return dot(hi) + dot(mid) + dot(lo)


def _gla_chunk_scores(q, k, v, b, *, reverse, exact):
    c = GLA_CHUNK
    n = GLA_SUB
    edge = b[0:1, :] if reverse else b[c - 1:c, :]
    qe = (q * jnp.exp(b)).astype(BF16)
    kd = (k * jnp.exp(edge - b)).astype(BF16)
    m = jnp.dot(v.T.astype(BF16), kd, preferred_element_type=F32)
    decay = jnp.exp(edge)

    lane = lax.broadcasted_iota(jnp.int32, (n, c), 1)
    sub_row = lax.broadcasted_iota(jnp.int32, (n, c), 0)
    blocks = []
    cap = 0.0 if exact else GLA_SAFE_EXP
    for i in range(c // n):
        lo, hi = i * n, (i + 1) * n
        q_i = q[lo:hi, :]
        b_i = b[lo:hi, :]
        has_border = (hi < c) if reverse else (lo > 0)
        if has_border:
            r = b[hi:hi + 1, :] if reverse else b[lo - 1:lo, :]
            lhs = (q_i * jnp.exp(b_i - r)).astype(BF16)
            rhs = (k * jnp.exp(jnp.minimum(r - b, cap))).astype(BF16)
        else:
            lhs = (q_i * jnp.exp(b_i)).astype(BF16)
            rhs = (k * jnp.exp(jnp.minimum(-b, cap))).astype(BF16)
        a_i = lax.dot_general(lhs, rhs, (((1,), (1,)), ((), ())), preferred_element_type=F32)
        if exact:
            a_diag = jnp.zeros((n, c), F32)
            for j in range(n):
                s = lo + j
                p = q_i * k[s:s + 1, :] * jnp.exp(b_i - b[s:s + 1, :])
                a_diag = jnp.where(lane == s, jnp.sum(p, axis=1, keepdims=True), a_diag)
            in_block = (lane >= lo) & (lane < hi)
            a_i = jnp.where(in_block, a_diag, a_i)
        causal = (lane >= sub_row + lo) if reverse else (lane <= sub_row + lo)
        blocks.append(jnp.where(causal, a_i, 0.0))
    a = jnp.concatenate(blocks, axis=0).astype(BF16)
    return qe, a, m, decay


def _gla_inter(qe, st):
    return lax.dot_general(qe, st.astype(BF16), (((1,), (1,)), ((), ())), preferred_element_type=F32)


def _gla_kernel(*refs, reverse, rope, has_s0, nchunk):
    it = iter(refs)
    q_ref, k_ref, v_ref, la_ref = next(it), next(it), next(it), next(it)
    cos_ref = sin_ref = s0_ref = None
    if rope:
        cos_ref, sin_ref = next(it), next(it)
    if has_s0:
        s0_ref = next(it)
    o_ref, sfin_ref, st_ref = next(it), next(it), next(it)

    blk = pl.program_id(2)

    @pl.when(blk == 0)
    def _():
        if has_s0:
            st_ref[...] = s0_ref[0, 0, 0, 0].T
        else:
            st_ref[...] = jnp.zeros(st_ref.shape, F32)

    def load_qk(rows):
        q = q_ref[rows, :] * (GLA_DK ** -0.5)
        k = k_ref[rows, :]
        if rope:
            cs, sn = cos_ref[rows, :], sin_ref[rows, :]
            q = q * cs + pltpu.roll(q, GLA_DK // 2, 1) * sn
            k = k * cs + pltpu.roll(k, GLA_DK // 2, 1) * sn
        return q, k

    def intra(a, rows):
        return jnp.dot(a, v_ref[rows, :].astype(BF16), preferred_element_type=F32)

    def run_exact():
        def body(step, carry):
            ci = (nchunk - 1 - step) if reverse else step
            rows = pl.ds(pl.multiple_of(ci * GLA_CHUNK, GLA_CHUNK), GLA_CHUNK)
            q, k = load_qk(rows)
            b = _cumulative_log_decay(la_ref[rows, :], reverse)
            qe, a, m, decay = _gla_chunk_scores(q, k, v_ref[rows, :], b, reverse=reverse, exact=True)
            o_ref[rows, :] = _gla_inter(qe, st_ref[...]) + intra(a, rows)
            st_ref[...] = st_ref[...] * decay + m
            return carry

        lax.fori_loop(0, nchunk, body, 0)

    def run_fast():
        order = list(range(nchunk))[::-1] if reverse else list(range(nchunk))
        rows = [pl.ds(ci * GLA_CHUNK, GLA_CHUNK) for ci in range(nchunk)]
        bs = [_cumulative_log_decay(la_ref[r, :], reverse) for r in rows]
        scores = [_gla_chunk_scores(*load_qk(r), v_ref[r, :], b, reverse=reverse, exact=False)
                  for r, b in zip(rows, bs)]
        o_intra = [intra(sc[1], r) for sc, r in zip(scores, rows)]
        st = st_ref[...]
        states = {}
        for ci in order:
            states[ci] = st
            st = st * scores[ci][3] + scores[ci][2]
        st_ref[...] = st
        for ci in range(nchunk):
            o_ref[rows[ci], :] = _gla_inter(scores[ci][0], states[ci]) + o_intra[ci]

    steep = jnp.min(la_ref[...]) < -(GLA_SAFE_EXP / GLA_SUB)
    pl.when(steep)(run_exact)
    pl.when(jnp.logical_not(steep))(run_fast)

    @pl.when(blk == pl.num_programs(2) - 1)
    def _():
        sfin_ref[0, 0] = st_ref[...].T


def gla_direction(proj, la, rope_tabs, s0, *, batch, seq, reverse, layer, direction, rows_blk):
    t = proj.shape[0]
    nblk = seq // rows_blk
    nchunk = rows_blk // GLA_CHUNK
    rope = rope_tabs is not None
    has_s0 = s0 is not None

    def rb(b, c):
        cc = (nblk - 1 - c) if reverse else c
        return b * nblk + cc

    kq, kv = OFF_GQ // GLA_DK, OFF_GV // GLA_DV
    kk = OFF_GK // GLA_DK
    kla = direction * GLA_HEADS
    in_specs = [pl.BlockSpec((rows_blk, GLA_DK), lambda b, h, c: (rb(b, c), kq + h)),
                pl.BlockSpec((rows_blk, GLA_DK), lambda b, h, c: (rb(b, c), kk + h)),
                pl.BlockSpec((rows_blk, GLA_DV), lambda b, h, c: (rb(b, c), kv + h)),
                pl.BlockSpec((rows_blk, GLA_DK), lambda b, h, c: (rb(b, c), kla + h))]
    args = [proj, proj, proj, la]
    if rope:
        tab_map = lambda b, h, c: ((nblk - 1 - c) if reverse else c, 0)
        in_specs += [pl.BlockSpec((rows_blk, GLA_DK), tab_map)] * 2
        args += list(rope_tabs)
    if has_s0:
        in_specs.append(pl.BlockSpec((1, 1, 1, 1, GLA_DK, GLA_DV), lambda b, h, c: (b, layer, direction, h, 0, 0)))
        args.append(s0)
    o, sfin = pl.pallas_call(
        functools.partial(_gla_kernel, reverse=reverse, rope=rope, has_s0=has_s0, nchunk=nchunk),
        grid=(batch, GLA_HEADS, nblk),
        in_specs=in_specs,
        out_specs=[pl.BlockSpec((rows_blk, GLA_DV), lambda b, h, c: (rb(b, c), h)),
                   pl.BlockSpec((1, 1, GLA_DK, GLA_DV), lambda b, h, c: (b, h, 0, 0))],
        out_shape=[jax.ShapeDtypeStruct((t, VAL_W), F32),
                   jax.ShapeDtypeStruct((batch, GLA_HEADS, GLA_DK, GLA_DV), F32)],
        scratch_shapes=[pltpu.VMEM((GLA_DV, GLA_DK), F32)],
        compiler_params=_cparams("parallel", "parallel", "arbitrary"),
        name="gla_bwd" if reverse else "gla_fwd",
    )(*args)
    return o, sfin


def _head_norm(x, g):
    return x * lax.rsqrt(jnp.mean(x * x, axis=-1, keepdims=True) + NORM_EPS) * g


def _ctx_attn_kernel(q_ref, k_ref, v_ref, gq_ref, gk_ref, o_ref, kn_ref, vn_ref):
    scale = NA_HD ** -0.5
    for h in range(NA_HEADS):
        cols = slice(h * NA_HD, (h + 1) * NA_HD)
        qn = _head_norm(q_ref[:, cols], gq_ref[...]) * scale
        kn = _head_norm(k_ref[:, cols], gk_ref[...])
        v = v_ref[:, cols]
        kn_ref[0, h] = kn
        vn_ref[0, h] = v
        s = lax.dot_general(qn.astype(BF16), kn.astype(BF16), (((1,), (1,)), ((), ())),
                            preferred_element_type=F32)
        e = jnp.exp(s - jnp.max(s, axis=-1, keepdims=True))
        p = e / jnp.sum(e, axis=-1, keepdims=True)
        o_ref[:, cols] = jnp.dot(p.astype(BF16), v.astype(BF16), preferred_element_type=F32)


def ctx_attention(proj, gq, gk, *, batch, seq):
    t = proj.shape[0]
    cq, ck, cv = OFF_NQ // NA_W, OFF_NK // NA_W, OFF_NV // NA_W
    cache_shape = jax.ShapeDtypeStruct((batch, NA_HEADS, seq, NA_HD), F32)
    cache_spec = pl.BlockSpec((1, NA_HEADS, seq, NA_HD), lambda b: (b, 0, 0, 0))
    return pl.pallas_call(
        _ctx_attn_kernel,
        grid=(batch,),
        in_specs=[pl.BlockSpec((seq, NA_W), lambda b: (b, cq)),
                  pl.BlockSpec((seq, NA_W), lambda b: (b, ck)),
                  pl.BlockSpec((seq, NA_W), lambda b: (b, cv)),
                  pl.BlockSpec((1, NA_HD), lambda b: (0, 0)),
                  pl.BlockSpec((1, NA_HD), lambda b: (0, 0))],
        out_specs=[pl.BlockSpec((seq, NA_W), lambda b: (b, 0)), cache_spec, cache_spec],
        out_shape=[jax.ShapeDtypeStruct((t, NA_W), F32), cache_shape, cache_shape],
        compiler_params=_cparams("parallel"),
        name="ctx_attention",
    )(proj, proj, proj, gq, gk)


def _na_kernel(q_ref, k_ref, v_ref, kc_ref, vc_ref, gq_ref, gk_ref, tab_ref, o_ref, qn_s, kn_s, v_s, *, rows):
    scale = NA_HD ** -0.5
    qn_s[...] = (_head_norm(q_ref[...], gq_ref[...]) * scale).astype(BF16)
    kn_s[...] = _head_norm(k_ref[...], gk_ref[...]).astype(BF16)
    v_s[...] = v_ref[...].astype(BF16)
    kctx = kc_ref[0, 0, 0].astype(BF16)
    vctx = vc_ref[0, 0, 0].astype(BF16)
    nblk = rows // NA_QROWS
    qlen, win = NA_QROWS * GRID_W, NA_UNION * GRID_W

    def scores(g):
        u0 = jnp.clip(g * NA_QROWS - NA_KH // 2, 0, rows - NA_UNION)
        cls = jnp.where(g == 0, 0, jnp.where(g == nblk - 1, 2, 1))
        qrows = pl.ds(pl.multiple_of(g * qlen, qlen), qlen)
        wrows = pl.ds(pl.multiple_of(u0 * GRID_W, GRID_W), win)
        qn = qn_s[qrows, :]
        s_loc = lax.dot_general(qn, kn_s[wrows, :], (((1,), (1,)), ((), ())), preferred_element_type=F32)
        s_ctx = lax.dot_general(qn, kctx, (((1,), (1,)), ((), ())), preferred_element_type=F32)
        return qrows, wrows, s_loc + tab_ref[0, cls], s_ctx

    def softmax(s_loc, s_ctx):
        m = jnp.maximum(jnp.max(s_loc, axis=-1, keepdims=True), jnp.max(s_ctx, axis=-1, keepdims=True))
        e_loc = jnp.exp(s_loc - m)
        e_ctx = jnp.exp(s_ctx - m)
        den = jnp.sum(e_loc, axis=-1, keepdims=True) + jnp.sum(e_ctx, axis=-1, keepdims=True)
        return e_loc.astype(BF16), e_ctx.astype(BF16), den

    def body(step, carry):
        blocks = [scores(step * NA_BLOCK_UNROLL + u) for u in range(NA_BLOCK_UNROLL)]
        probs = [softmax(s_loc, s_ctx) for _, _, s_loc, s_ctx in blocks]
        for (qrows, wrows, _, _), (e_loc, e_ctx, den) in zip(blocks, probs):
            acc = jnp.dot(e_loc, v_s[wrows, :], preferred_element_type=F32)
            acc = acc + jnp.dot(e_ctx, vctx, preferred_element_type=F32)
            o_ref[qrows, :] = acc / den
        return carry

    lax.fori_loop(0, nblk // NA_BLOCK_UNROLL, body, 0)


def neighbourhood_attention(proj, cache_k, cache_v, gq, gk, table, *, batch, seq, layer, row0):
    rows = seq // GRID_W
    assert rows % NA_QROWS == 0 and rows // NA_QROWS >= 3 and rows >= NA_UNION
    past = cache_k.shape[3]
    cq, ck, cv = OFF_NQ // NA_HD, OFF_NK // NA_HD, OFF_NV // NA_HD
    cache_spec = pl.BlockSpec((1, 1, 1, past, NA_HD), lambda b, h: (b, layer, h, 0, 0))
    return pl.pallas_call(
        functools.partial(_na_kernel, rows=rows),
        grid=(batch, NA_HEADS),
        in_specs=[pl.BlockSpec((seq, NA_HD), lambda b, h: (row0 + b, cq + h)),
                  pl.BlockSpec((seq, NA_HD), lambda b, h: (row0 + b, ck + h)),
                  pl.BlockSpec((seq, NA_HD), lambda b, h: (row0 + b, cv + h)),
                  cache_spec, cache_spec,
                  pl.BlockSpec((1, NA_HD), lambda b, h: (0, 0)),
                  pl.BlockSpec((1, NA_HD), lambda b, h: (0, 0)),
                  pl.BlockSpec((1, 3, NA_QROWS * GRID_W, NA_UNION * GRID_W), lambda b, h: (h, 0, 0, 0))],
        out_specs=pl.BlockSpec((seq, NA_HD), lambda b, h: (b, h)),
        out_shape=jax.ShapeDtypeStruct((batch * seq, NA_W), F32),
        scratch_shapes=[pltpu.VMEM((seq, NA_HD), BF16)] * 3,
        compiler_params=_cparams("parallel", "parallel"),
        name="neighbourhood_attention",
    )(proj, proj, proj, cache_k, cache_v, gq, gk, table)


def na_bias_table(rpb, rows):
    nblk = rows // NA_QROWS
    col = np.arange(GRID_W)
    c_start = np.clip(col - NA_KW // 2, 0, GRID_W - NA_KW)
    col_in = (col[None, :] >= c_start[:, None]) & (col[None, :] < c_start[:, None] + NA_KW)
    col_idx = np.clip(col[None, :] - col[:, None] + NA_KW - 1, 0, 2 * NA_KW - 2)
    row_off, row_in = [], []
    for g in (0, 1, nblk - 1):
        u0 = np.clip(g * NA_QROWS - NA_KH // 2, 0, rows - NA_UNION)
        r = g * NA_QROWS + np.arange(NA_QROWS)[:, None]
        r_start = np.clip(r - NA_KH // 2, 0, rows - NA_KH)
        key_row = u0 + np.arange(NA_UNION)[None, :]
        row_in.append((key_row >= r_start) & (key_row < r_start + NA_KH))
        row_off.append(np.clip(key_row - r + NA_KH - 1, 0, 2 * NA_KH - 2))
    row_off, row_in = np.stack(row_off), np.stack(row_in)
    tab = rpb[:, row_off][..., col_idx]
    keep = row_in[None, :, :, :, None, None] & col_in[None, None, None, None]
    tab = jnp.where(keep, tab, -jnp.inf)
    tab = jnp.transpose(tab, (0, 1, 2, 4, 3, 5))
    return tab.reshape(rpb.shape[0], 3, NA_QROWS * GRID_W, NA_UNION * GRID_W).astype(F32)


def _filter_kernel(z_ref, f1w_ref, f1b_ref, f2w_ref, f2b_ref, f3w_ref, fr_ref, dl_ref, o_ref, *, seq):
    fr = fr_ref[...]
    h = jnp.sin(fr * (jnp.dot(z_ref[...], f1w_ref[...], preferred_element_type=F32, precision=HIGHEST)
                      + f1b_ref[...]))
    h = jnp.sin(fr * (jnp.dot(h, f2w_ref[...], preferred_element_type=F32, precision=HIGHEST)
                      + f2b_ref[...]))
    h = jnp.dot(h, f3w_ref[...], preferred_element_type=F32, precision=HIGHEST)
    t = lax.broadcasted_iota(jnp.int32, (seq, 1), 0).astype(F32)
    dist = jnp.abs(t - float(seq // 2)) * (2.0 / seq)
    h = h * jnp.exp(-dist * dl_ref[...])
    o_ref[...] = h * lax.rsqrt(jnp.sum(h * h, axis=0, keepdims=True) + NORM_EPS)


def hyena_filters(zpos, f1w, f1b, f2w, f2b, f3w, freq, deltas, *, seq):
    n = f3w.shape[1]
    tc = 512
    full = lambda a: pl.BlockSpec(a.shape, lambda j: (0,) * a.ndim)
    return pl.pallas_call(
        functools.partial(_filter_kernel, seq=seq),
        grid=(n // tc,),
        in_specs=[full(zpos), full(f1w), full(f1b), full(f2w), full(f2b),
                  pl.BlockSpec((HY_HIDDEN, tc), lambda j: (0, j)), full(freq),
                  pl.BlockSpec((1, tc), lambda j: (0, j))],
        out_specs=pl.BlockSpec((seq, tc), lambda j: (0, j)),
        out_shape=jax.ShapeDtypeStruct((seq, n), F32),
        compiler_params=_cparams("parallel"),
        name="hyena_filters",
    )(zpos, f1w, f1b, f2w, f2b, f3w, freq, deltas)


def _lane_tile(t, width):
    return jnp.concatenate([t] * (width // LANES), axis=1) if width > LANES else t


def _cdft(vals, sign):
    if len(vals) == 1:
        return vals
    if len(vals) == 2:
        (ar, ai), (br, bi) = vals
        return [(ar + br, ai + bi), (ar - br, ai - bi)]
    assert len(vals) == 4
    (x0r, x0i), (x1r, x1i), (x2r, x2i), (x3r, x3i) = vals
    a0r, a0i, a1r, a1i = x0r + x2r, x0i + x2i, x0r - x2r, x0i - x2i
    a2r, a2i, a3r, a3i = x1r + x3r, x1i + x3i, x1r - x3r, x1i - x3i
    wr, wi = -sign * a3i, sign * a3r
    return [(a0r + a2r, a0i + a2i), (a1r + wr, a1i + wi), (a0r - a2r, a0i - a2i), (a1r - wr, a1i - wi)]


def _fwd_spectrum(z_list, f_ref, tw_ref, rows, width):
    lp = f_ref.shape[1]
    f_cos = f_ref[rows, :]
    f_sin = f_ref[pl.ds(lp + rows.start, rows.size), :]
    xs = []
    for r, z in enumerate(z_list):
        xc = jnp.dot(f_cos, z, preferred_element_type=F32)
        xn = jnp.dot(f_sin, z, preferred_element_type=F32)
        if r == 0:
            xs.append((xc, -xn))
        else:
            c = _lane_tile(tw_ref[r, 0, rows, :], width)
            s = _lane_tile(tw_ref[r, 1, rows, :], width)
            xs.append((xc * c - xn * s, -(xn * c + xc * s)))
    return _cdft(xs, -1)


def _spectrum_kernel(*refs, radix, seq, kc, tc):
    h_refs, (f_ref, tw_ref, sh_ref, o_ref) = refs[:radix], refs[radix:]
    lp = seq // radix
    hs = [h[...].astype(BF16) for h in h_refs]
    for c0 in range(0, lp, kc):
        rows = pl.ds(c0, kc)
        cs = _lane_tile(sh_ref[0, rows, :], tc) * (1.0 / seq)
        sn = _lane_tile(sh_ref[1, rows, :], tc) * (1.0 / seq)
        for q, (re, im) in enumerate(_fwd_spectrum(hs, f_ref, tw_ref, rows, tc)):
            o_ref[0, q, rows, :] = re * cs - im * sn
            o_ref[1, q, rows, :] = re * sn + im * cs


def _whole(a):
    return pl.BlockSpec(a.shape, lambda *_: (0,) * a.ndim, pipeline_mode=pl.Buffered(1))


def filter_spectrum(filt, tabs, *, seq):
    n = filt.shape[1]
    radix, lp, tc = HY_RADIX, seq // HY_RADIX, HY_TC_FFT
    view = filt.reshape(lp, radix * n)
    f, _, tw, sh = tabs
    groups = [pl.BlockSpec((lp, tc), functools.partial(lambda j, r: (0, r * (n // tc) + j), r=r)) for r in range(radix)]
    return pl.pallas_call(
        functools.partial(_spectrum_kernel, radix=radix, seq=seq, kc=min(HY_KC, lp), tc=tc),
        grid=(n // tc,),
        in_specs=groups + [_whole(f), _whole(tw), _whole(sh)],
        out_specs=pl.BlockSpec((2, radix, lp, tc), lambda j: (0, 0, 0, j)),
        out_shape=jax.ShapeDtypeStruct((2, radix, lp, n), F32),
        compiler_params=_cparams("parallel"),
        name="filter_spectrum",
    )(*([view] * radix), f, tw, sh)


def _short_conv(u, w, bias, seq):
    t = lax.broadcasted_iota(jnp.int32, (seq, 1), 0)
    prev = jnp.where(t == 0, 0.0, pltpu.roll(u, 1, 0))
    nxt = jnp.where(t == seq - 1, 0.0, pltpu.roll(u, seq - 1, 0))
    return bias + prev * w[0:1, :] + u * w[1:2, :] + nxt * w[2:3, :]


def _hy_pre_kernel(u_ref, w_ref, b_ref, z_ref, *, seq):
    z_ref[...] = _short_conv(u_ref[...], w_ref[...], b_ref[...], seq).astype(BF16)


def _hy_gate_kernel(ux_ref, uz_ref, wx_ref, bx_ref, wz_ref, bz_ref, cv_ref, hb_ref, zo_ref, zb_ref, *, seq):
    x = _short_conv(ux_ref[...], wx_ref[...], bx_ref[...], seq)
    z = _short_conv(uz_ref[...], wz_ref[...], bz_ref[...], seq)
    out = x * (cv_ref[...] + hb_ref[...] * z)
    zo_ref[...] = out
    zb_ref[...] = out.astype(BF16)


def _hy_final_kernel(ux_ref, wx_ref, bx_ref, cv_ref, hb_ref, z_ref, zb_ref, *, seq):
    x = _short_conv(ux_ref[...], wx_ref[...], bx_ref[...], seq)
    zb_ref[...] = (x * (cv_ref[...] + hb_ref[...] * z_ref[...])).astype(BF16)


HY_ELEM_SLAB_BYTES = 2 * 1024 * 1024


def _hy_tc(seq):
    return max(LANES, min(HY_W, HY_ELEM_SLAB_BYTES // (4 * seq)))


def _hy_specs(seq, row0, col0):
    tc = _hy_tc(seq)
    cb, wb = col0 // tc, (col0 - OFF_HY) // tc
    u = pl.BlockSpec((seq, tc), lambda b, j: (row0 + b, cb + j))
    w = pl.BlockSpec((3, tc), lambda b, j: (0, wb + j))
    bb = pl.BlockSpec((1, tc), lambda b, j: (0, wb + j))
    act = pl.BlockSpec((seq, tc), lambda b, j: (b, j))
    hb = pl.BlockSpec((1, tc), lambda b, j: (0, j))
    return u, w, bb, act, hb


def hyena_pre(proj, conv_w, conv_b, *, batch, seq, row0):
    u, w, bb, act, _ = _hy_specs(seq, row0, OFF_HY + HY_ORDER * HY_W)
    return pl.pallas_call(
        functools.partial(_hy_pre_kernel, seq=seq),
        grid=(batch, HY_W // _hy_tc(seq)),
        in_specs=[u, w, bb], out_specs=act,
        out_shape=jax.ShapeDtypeStruct((batch * seq, HY_W), BF16),
        compiler_params=_cparams("parallel", "parallel"), name="hyena_pre",
    )(proj, conv_w, conv_b)


def hyena_gate(proj, conv_w, conv_b, conv, hy_bias, *, batch, seq, row0):
    ux, wx, bx, act, hb = _hy_specs(seq, row0, OFF_HY)
    uz, wz, bz, _, _ = _hy_specs(seq, row0, OFF_HY + HY_ORDER * HY_W)
    return pl.pallas_call(
        functools.partial(_hy_gate_kernel, seq=seq),
        grid=(batch, HY_W // _hy_tc(seq)),
        in_specs=[ux, uz, wx, bx, wz, bz, act, hb], out_specs=[act, act],
        out_shape=[jax.ShapeDtypeStruct((batch * seq, HY_W), F32),
                   jax.ShapeDtypeStruct((batch * seq, HY_W), BF16)],
        compiler_params=_cparams("parallel", "parallel"), name="hyena_gate",
    )(proj, proj, conv_w, conv_b, conv_w, conv_b, conv, hy_bias[0:1])


def hyena_final(proj, conv_w, conv_b, conv, hy_bias, z1, *, batch, seq, row0):
    ux, wx, bx, act, hb = _hy_specs(seq, row0, OFF_HY + HY_W)
    return pl.pallas_call(
        functools.partial(_hy_final_kernel, seq=seq),
        grid=(batch, HY_W // _hy_tc(seq)),
        in_specs=[ux, wx, bx, act, hb, act], out_specs=act,
        out_shape=jax.ShapeDtypeStruct((batch * seq, HY_W), BF16),
        compiler_params=_cparams("parallel", "parallel"), name="hyena_final",
    )(proj, conv_w, conv_b, conv, hy_bias[1:2], z1)


def _long_conv_kernel(*refs, radix, seq, kc, tc):
    z_refs, (f_ref, ft_ref, tw_ref, g_ref, o_ref, u_s) = refs[:radix], refs[radix:]
    lp = seq // radix
    r_out = pl.program_id(2)

    @pl.when(r_out == 0)
    def _():
        zs = [z[...] for z in z_refs]
        for c0 in range(0, lp, kc):
            rows = pl.ds(c0, kc)
            ys = []
            for q, (zr, zi) in enumerate(_fwd_spectrum(zs, f_ref, tw_ref, rows, tc)):
                gr, gi = g_ref[0, q, rows, :], g_ref[1, q, rows, :]
                ys.append((zr * gr - zi * gi, zr * gi + zi * gr))
            for r, (re, im) in enumerate(_cdft(ys, +1)):
                if r > 0:
                    c = _lane_tile(tw_ref[r, 0, rows, :], tc)
                    s = _lane_tile(tw_ref[r, 1, rows, :], tc)
                    re, im = re * c - im * s, re * s + im * c
                u_s[r, rows, :] = re.astype(BF16)
                u_s[r, pl.ds(lp + c0, kc), :] = (-im).astype(BF16)

    o_ref[...] = jnp.dot(ft_ref[...], u_s[r_out], preferred_element_type=F32)


def hyena_long_conv(z, g, tabs, *, batch, seq, order):
    c = z.shape[1]
    radix, lp, tc = HY_RADIX, seq // HY_RADIX, HY_TC_FFT
    f, ft, tw, _ = tabs
    ncb = c // tc
    goff = order * ncb
    view = z.reshape(batch * lp, radix * c)
    groups = [pl.BlockSpec((lp, tc), functools.partial(lambda j, b, ro, r: (b, r * ncb + j), r=r)) for r in range(radix)]
    out = pl.pallas_call(
        functools.partial(_long_conv_kernel, radix=radix, seq=seq, kc=min(HY_KC, lp), tc=tc),
        grid=(ncb, batch, radix),
        in_specs=groups + [_whole(f), _whole(ft), _whole(tw),
                           pl.BlockSpec((2, radix, lp, tc), lambda j, b, ro: (0, 0, 0, goff + j),
                                        pipeline_mode=pl.Buffered(1))],
        out_specs=pl.BlockSpec((lp, tc), lambda j, b, ro: (b, ro * ncb + j)),
        out_shape=jax.ShapeDtypeStruct((batch * lp, radix * c), F32),
        scratch_shapes=[pltpu.VMEM((radix, 2 * lp, tc), BF16)],
        compiler_params=_cparams("parallel", "parallel", "arbitrary"),
        name="hyena_long_conv",
    )(*([view] * radix), f, ft, tw, g)
    return out.reshape(batch * seq, c)


def dft_tables(seq):
    radix, lp = HY_RADIX, seq // HY_RADIX
    odd = 2 * jnp.arange(lp, dtype=jnp.int32) + 1

    def cs(num, den):
        ang = (num % (2 * den)).astype(F32) * (math.pi / den)
        return jnp.cos(ang), jnp.sin(ang)

    fc, fs = cs(odd[:, None] * jnp.arange(lp, dtype=jnp.int32)[None, :], 2 * lp)
    f = jnp.concatenate([fc, fs], axis=0).astype(BF16)
    tc_, ts_ = cs(odd[None, :] * jnp.arange(radix, dtype=jnp.int32)[:, None], 2 * seq)
    tw = jnp.broadcast_to(jnp.stack([tc_, ts_], axis=1)[..., None], (radix, 2, lp, LANES))
    sc, ss = cs(odd, 4)
    sh = jnp.broadcast_to(jnp.stack([sc, ss], axis=0)[..., None], (2, lp, LANES))
    return f, f.T, tw, sh


def position_features(seq):
    t = jnp.arange(seq, dtype=F32)
    tn = t / seq
    bands = jnp.arange(1, HY_BANDS + 1, dtype=F32)
    ang = 2.0 * jnp.pi * tn[:, None] * bands[None, :]
    z = jnp.concatenate([tn[:, None], jnp.cos(ang), jnp.sin(ang)], axis=-1)
    return jnp.pad(z, ((0, 0), (0, LANES - z.shape[1])))


def rope_tables(seq):
    n_freq = GLA_DK // 4
    pos = jnp.arange(seq)
    row = (pos // GRID_W).astype(F32)
    col = (pos % GRID_W).astype(F32)
    inv = ROPE_BASE ** (-jnp.arange(n_freq, dtype=F32) / n_freq)
    ang = jnp.concatenate([row[:, None] * inv, col[:, None] * inv], axis=-1)
    cos, sin = jnp.cos(ang), jnp.sin(ang)
    return jnp.concatenate([cos, cos], axis=-1), jnp.concatenate([-sin, sin], axis=-1)


def _merge_kernel(of_ref, ob_ref, gg_ref, gn_ref, nb_ref, hc_ref, ga_ref, gb_ref, gc_ref,
                  wa_ref, wb_ref, wc_ref, o_ref, a_s):
    @pl.when(pl.program_id(1) == 0)
    def _():
        for h in range(GLA_HEADS):
            cols = slice(h * GLA_DV, (h + 1) * GLA_DV)
            oa = of_ref[:, cols] + ob_ref[:, cols]
            oa = _head_norm(oa, gn_ref[...])
            gg = gg_ref[:, cols]
            a_s[:, cols] = (oa * (gg * jax.nn.sigmoid(gg))).astype(BF16)

    ya = jnp.dot(a_s[...], wa_ref[...], preferred_element_type=F32)
    yb = jnp.dot(nb_ref[...].astype(BF16), wb_ref[...], preferred_element_type=F32)
    yc = jnp.dot(hc_ref[...], wc_ref[...], preferred_element_type=F32)
    merged = (jax.nn.sigmoid(ga_ref[...]) * ya + jax.nn.sigmoid(gb_ref[...]) * yb
              + jax.nn.sigmoid(gc_ref[...]) * yc)
    o_ref[...] = merged.astype(o_ref.dtype)


def branch_merge(o_f, o_b, proj, gnorm, o_attn, o_hy, w_a, w_b, w_c, *, row0_blocks, tm, tn):
    t = o_f.shape[0]
    d = w_a.shape[1]
    assert OFF_GATES % tn == 0 and d % tn == 0
    act = lambda w: pl.BlockSpec((tm, w), lambda i, j: (i, 0))
    gate = lambda g: pl.BlockSpec((tm, tn), lambda i, j: (row0_blocks + i, (OFF_GATES + g * d) // tn + j))
    wspec = lambda w: pl.BlockSpec((w.shape[0], tn), lambda i, j: (0, j))
    return pl.pallas_call(
        _merge_kernel,
        grid=(t // tm, d // tn),
        in_specs=[act(VAL_W), act(VAL_W),
                  pl.BlockSpec((tm, VAL_W), lambda i, j: (row0_blocks + i, OFF_GG // VAL_W)),
                  pl.BlockSpec((1, GLA_DV), lambda i, j: (0, 0)),
                  act(NA_W), act(HY_W), gate(0), gate(1), gate(2),
                  wspec(w_a), wspec(w_b), wspec(w_c)],
        out_specs=pl.BlockSpec((tm, tn), lambda i, j: (i, j)),
        out_shape=jax.ShapeDtypeStruct((t, d), BF16),
        scratch_shapes=[pltpu.VMEM((tm, VAL_W), BF16)],
        compiler_params=_cparams("parallel", "arbitrary"),
        name="branch_merge",
    )(o_f, o_b, proj, gnorm, o_attn, o_hy, proj, proj, proj, w_a, w_b, w_c)


def _matmul_residual_kernel(a_ref, w_ref, g_ref, r_ref, o_ref):
    y = jnp.dot(a_ref[...], w_ref[...], preferred_element_type=F32)
    o_ref[...] = r_ref[...] + g_ref[0] * y


def matmul_residual(a, w, gate, resid, *, rows_per_mod, tm, tn):
    t, k = a.shape
    n = w.shape[1]
    return pl.pallas_call(
        _matmul_residual_kernel,
        grid=(t // tm, n // tn),
        in_specs=[pl.BlockSpec((tm, k), lambda i, j: (i, 0)),
                  pl.BlockSpec((k, tn), lambda i, j: (0, j)),
                  pl.BlockSpec((1, 1, tn), lambda i, j: ((i * tm) // rows_per_mod, 0, j)),
                  pl.BlockSpec((tm, tn), lambda i, j: (i, j))],
        out_specs=pl.BlockSpec((tm, tn), lambda i, j: (i, j)),
        out_shape=jax.ShapeDtypeStruct((t, n), F32),
        compiler_params=_cparams("parallel", "arbitrary"),
        name="matmul_residual",
    )(a, w, gate, resid)


def _trunk_layer(x, mods, p, consts, *, batch, seq, layer, latent):
    sh1, sc1, g1, sh2, sc2, g2 = mods
    t = x.shape[0]
    rows_per_mod = t // sh1.shape[0]
    tm = 512

    proj = norm_matmul(x, p["norm1_g"], sc1, sh1, p["w_in_main"], rows_per_mod=rows_per_mod,
                       tm=tm, tn=1536, relu2=False, out_dtype=F32)
    la = decay_gate(x, p["norm1_g"], sc1, sh1, p["w_in_ga"], p["wa2_bd"], p["ba2"],
                    rows_per_mod=rows_per_mod, tm=tm)

    rows_blk = min(seq, 512)
    rope = consts["rope"] if latent else None
    s0 = consts["state_gla"] if latent else None
    o_f, s_f = gla_direction(proj, la, rope, s0, batch=batch, seq=seq, reverse=False, layer=layer,
                             direction=0, rows_blk=rows_blk)
    o_b, s_b = gla_direction(proj, la, rope, s0, batch=batch, seq=seq, reverse=True, layer=layer,
                             direction=1, rows_blk=rows_blk)

    if latent:
        o_attn = neighbourhood_attention(proj, consts["cache_k"], consts["cache_v"], p["na_qnorm_g"],
                                         p["na_knorm_g"], p["na_table"], batch=batch, seq=seq,
                                         layer=layer, row0=0)
        new_ctx = None
    else:
        o_attn, kn, vn = ctx_attention(proj, p["na_qnorm_g"], p["na_knorm_g"], batch=batch, seq=seq)
        new_ctx = (jnp.stack([s_f, s_b], axis=1), kn, vn)

    key = "lat" if latent else "ctx"
    tabs = consts["dft_" + key]
    g_spec = p["spectrum_" + key]
    z0 = hyena_pre(proj, p["hy_conv_w"], p["hy_conv_b"], batch=batch, seq=seq, row0=0)
    cv0 = hyena_long_conv(z0, g_spec, tabs, batch=batch, seq=seq, order=0)
    z1, z1b = hyena_gate(proj, p["hy_conv_w"], p["hy_conv_b"], cv0, p["hy_bias"], batch=batch, seq=seq, row0=0)
    cv1 = hyena_long_conv(z1b, g_spec, tabs, batch=batch, seq=seq, order=1)
    o_hy = hyena_final(proj, p["hy_conv_w"], p["hy_conv_b"], cv1, p["hy_bias"], z1, batch=batch, seq=seq, row0=0)

    merged = branch_merge(o_f, o_b, proj, p["gla_norm_g"], o_attn, o_hy, p["w_br_a"], p["w_br_b"],
                          p["w_br_c"], row0_blocks=0, tm=256, tn=2048)
    x = matmul_residual(merged, p["w_out"], g1, x, rows_per_mod=rows_per_mod, tm=tm, tn=2048)

    u = norm_matmul(x, p["norm2_g"], sc2, sh2, p["w_mlp1"], rows_per_mod=rows_per_mod,
                    tm=tm, tn=2048, relu2=True, out_dtype=BF16)
    x = matmul_residual(u, p["w_mlp2"], g2, x, rows_per_mod=rows_per_mod, tm=tm, tn=512)
    return x, new_ctx


def _reorder_w_in(w_in):
    o_ga = 2 * KEY_W + 2 * VAL_W
    n_ga = 2 * GLA_LOWRANK
    o_hy = o_ga + n_ga + 3 * NA_W
    o_gates = o_hy + (HY_ORDER + 1) * HY_W
    main = jnp.concatenate([w_in[..., :o_ga], w_in[..., o_ga + n_ga:o_hy], w_in[..., o_gates:],
                            w_in[..., o_hy:o_gates]], axis=-1).astype(BF16)
    ga = jnp.pad(w_in[..., o_ga:o_ga + n_ga], ((0, 0), (0, LANES - n_ga))).astype(BF16)
    return main, ga


def kernel(x_prompt, x_sample, state_gla, cache_na_k, cache_na_v, c, c_ctx, w_mod, b_mod, norm1_g, norm2_g, w_in, gla_wa2, gla_ba2, gla_norm_g, na_qnorm_g, na_knorm_g, na_rpb, hy_conv_w, hy_conv_b, hy_f1_w, hy_f1_b, hy_f2_w, hy_f2_b, hy_f3_w, hy_freq, hy_bias, w_br_a, w_br_b, w_br_c, w_out, w_mlp1, w_mlp2):
    batch, seq, d = x_prompt.shape
    dec_batch, dec_seq, _ = x_sample.shape
    depth = w_mod.shape[0]

    bf = lambda w: w.astype(BF16)
    wa2_bd = jnp.zeros((depth, LANES, 2 * KEY_W), F32)
    wa2_bd = wa2_bd.at[:, :GLA_LOWRANK, :KEY_W].set(gla_wa2[:, 0])
    wa2_bd = wa2_bd.at[:, GLA_LOWRANK:2 * GLA_LOWRANK, KEY_W:].set(gla_wa2[:, 1])
    ba2 = gla_ba2.reshape(depth, 1, 2 * KEY_W)
    f1w = jnp.pad(hy_f1_w, ((0, 0), (0, LANES - hy_f1_w.shape[1]), (0, 0)))

    consts = {
        "rope": rope_tables(dec_seq),
        "state_gla": state_gla, "cache_k": cache_na_k, "cache_v": cache_na_v,
        "dft_ctx": dft_tables(seq), "dft_lat": dft_tables(dec_seq),
    }
    deltas = jnp.abs(jnp.linspace(math.log(HY_DECAY_TARGET) / HY_DECAY_PCT_LONG,
                                  math.log(HY_DECAY_TARGET) / HY_DECAY_PCT_SHORT, HY_W, dtype=F32))
    deltas = jnp.tile(deltas, HY_ORDER)[None, :]
    zpos = {"ctx": position_features(seq), "lat": position_features(dec_seq)}

    cond = jnp.zeros((16, d), F32).at[0].set(c_ctx).at[1:1 + dec_batch].set(c)

    xp = x_prompt.reshape(batch * seq, d)
    xs = x_sample.reshape(dec_batch * dec_seq, d)
    new_gla, new_k, new_v = [], [], []
    for l in range(depth):
        m = adaln(cond, w_mod, b_mod[l][None, :], l)
        mods_ctx = tuple(m[0:1, i * d:(i + 1) * d][:, None, :] for i in range(N_MOD))
        mods_lat = tuple(m[1:1 + dec_batch, i * d:(i + 1) * d][:, None, :] for i in range(N_MOD))
        w_in_main, w_in_ga = _reorder_w_in(w_in[l])
        p = dict(norm1_g=norm1_g[l][None], norm2_g=norm2_g[l][None], w_in_main=w_in_main, w_in_ga=w_in_ga,
                 wa2_bd=wa2_bd[l], ba2=ba2[l], gla_norm_g=gla_norm_g[l][None], na_qnorm_g=na_qnorm_g[l][None],
                 na_knorm_g=na_knorm_g[l][None], na_table=na_bias_table(na_rpb[l], dec_seq // GRID_W),
                 hy_conv_w=hy_conv_w[l], hy_conv_b=hy_conv_b[l][None], hy_bias=hy_bias[l],
                 w_br_a=bf(w_br_a[l]), w_br_b=bf(w_br_b[l]), w_br_c=bf(w_br_c[l]), w_out=bf(w_out[l]),
                 w_mlp1=bf(w_mlp1[l]), w_mlp2=bf(w_mlp2[l]))
        for key, length in (("ctx", seq), ("lat", dec_seq)):
            filt = hyena_filters(zpos[key], f1w[l], hy_f1_b[l][None], hy_f2_w[l], hy_f2_b[l][None],
                                 hy_f3_w[l], hy_freq[l][None], deltas, seq=length)
            p["spectrum_" + key] = filter_spectrum(filt, consts["dft_" + key], seq=length)
        xp, (s_gla, k_ctx, v_ctx) = _trunk_layer(xp, mods_ctx, p, consts, batch=batch, seq=seq, layer=l, latent=False)
        new_gla.append(s_gla)
        new_k.append(k_ctx)
        new_v.append(v_ctx)
        xs, _ = _trunk_layer(xs, mods_lat, p, consts, batch=dec_batch, seq=dec_seq, layer=l, latent=True)
    return (xp.reshape(batch, seq, d), xs.reshape(dec_batch, dec_seq, d),
            jnp.stack(new_gla, axis=1), jnp.stack(new_k, axis=1), jnp.stack(new_v, axis=1))
```

```python
import functools
import math

import jax
import jax.numpy as jnp
import numpy as np
from jax import lax
from jax.experimental import pallas as pl
from jax.experimental.pallas import tpu as pltpu

F32 = jnp.float32
BF16 = jnp.bfloat16
HIGHEST = lax.Precision.HIGHEST

GRID_W = 64
N_MOD = 6
NORM_EPS = 1e-6
GLA_HEADS = 4
GLA_DK = 128
GLA_DV = 256
GLA_LOWRANK = 16
GLA_TAU = 16.0
GLA_CHUNK = 64
GLA_SUB = 16
GLA_SAFE_EXP = 60.0
ROPE_BASE = 10000.0
NA_HEADS = 8
NA_HD = 128
NA_KH = 8
NA_KW = 16
NA_QROWS = 4
NA_UNION = NA_KH + NA_QROWS
NA_BLOCK_UNROLL = 2
HY_W = 1024
HY_ORDER = 2
HY_BANDS = 16
HY_HIDDEN = 64
HY_DECAY_TARGET = 1e-2
HY_DECAY_PCT_SHORT = 0.3
HY_DECAY_PCT_LONG = 1.5
HY_RADIX = 4
HY_RADIX_MIN_SEQ = 2048
HY_KC = 256
HY_TC_FFT = 256

V7X_VMEM_LIMIT_BYTES = 56 * 1024 * 1024
LANES = 128

KEY_W = GLA_HEADS * GLA_DK
VAL_W = GLA_HEADS * GLA_DV
NA_W = NA_HEADS * NA_HD
OFF_GQ = 0
OFF_GK = OFF_GQ + KEY_W
OFF_GV = OFF_GK + KEY_W
OFF_GG = OFF_GV + VAL_W
OFF_NQ = OFF_GG + VAL_W
OFF_NK = OFF_NQ + NA_W
OFF_NV = OFF_NK + NA_W
OFF_GATES = OFF_NV + NA_W
OFF_HY = OFF_GATES + 3 * 2048


def _cparams(*sem):
    return pltpu.CompilerParams(dimension_semantics=sem, vmem_limit_bytes=V7X_VMEM_LIMIT_BYTES)


def _modulated_norm(x, g, scale, shift):
    ms = jnp.mean(x * x, axis=-1, keepdims=True)
    y = x * lax.rsqrt(ms + NORM_EPS) * g
    return y * (1.0 + scale) + shift


def _adaln_kernel(c_ref, w_ref, b_ref, o_ref):
    c = c_ref[...]
    a = (c * jax.nn.sigmoid(c)).astype(BF16)
    o_ref[...] = jnp.dot(a, w_ref[...].astype(BF16), preferred_element_type=F32) + b_ref[...]


def adaln(cond, w_mod, b_mod, layer):
    m, d = cond.shape
    n = w_mod.shape[2]
    tn = 1024
    return pl.pallas_call(
        _adaln_kernel,
        grid=(n // tn,),
        in_specs=[pl.BlockSpec((m, d), lambda j: (0, 0)),
                  pl.BlockSpec((None, d, tn), lambda j: (layer, 0, j)),
                  pl.BlockSpec((1, tn), lambda j: (0, j))],
        out_specs=pl.BlockSpec((m, tn), lambda j: (0, j)),
        out_shape=jax.ShapeDtypeStruct((m, n), F32),
        compiler_params=_cparams("parallel"),
        name="adaln",
    )(cond, w_mod, b_mod)


def _log_decay(h, wga_ref, wa2_ref, ba2_ref):
    ga = jnp.dot(h, wga_ref[...], preferred_element_type=F32)
    pre = jnp.dot(ga, wa2_ref[...], preferred_element_type=F32, precision=HIGHEST) + ba2_ref[...]
    log_sig = jnp.minimum(pre, 0.0) - jnp.log(1.0 + jnp.exp(-jnp.abs(pre)))
    return log_sig * (1.0 / GLA_TAU)


def _norm_matmul_kernel(*refs, relu2, decay):
    x_ref, g_ref, sc_ref, sh_ref, w_ref = refs[:5]
    if decay:
        wga_ref, wa2_ref, ba2_ref, o_ref, la_ref, h_ref = refs[5:]
    else:
        o_ref, h_ref = refs[5:]

    @pl.when(pl.program_id(1) == 0)
    def _():
        h = _modulated_norm(x_ref[...], g_ref[...], sc_ref[0], sh_ref[0]).astype(BF16)
        h_ref[...] = h
        if decay:
            la_ref[...] = _log_decay(h, wga_ref, wa2_ref, ba2_ref)

    acc = jnp.dot(h_ref[...], w_ref[...], preferred_element_type=F32)
    if relu2:
        acc = jnp.square(jnp.maximum(acc, 0.0))
    o_ref[...] = acc.astype(o_ref.dtype)


def norm_matmul(x, g, scale, shift, w, *, rows_per_mod, tm, tn, relu2, out_dtype, decay_weights=None):
    t, d = x.shape
    n = w.shape[1]
    mod_map = lambda i, j: ((i * tm) // rows_per_mod, 0, 0)
    decay = decay_weights is not None
    in_specs = [pl.BlockSpec((tm, d), lambda i, j: (i, 0)),
                pl.BlockSpec((1, d), lambda i, j: (0, 0)),
                pl.BlockSpec((1, 1, d), mod_map),
                pl.BlockSpec((1, 1, d), mod_map),
                pl.BlockSpec((d, tn), lambda i, j: (0, j))]
    out_specs = pl.BlockSpec((tm, tn), lambda i, j: (i, j))
    out_shape = jax.ShapeDtypeStruct((t, n), out_dtype)
    args = [x, g, scale, shift, w]
    if decay:
        in_specs += [pl.BlockSpec(a.shape, lambda i, j: (0, 0)) for a in decay_weights]
        n_la = decay_weights[1].shape[1]
        out_specs = [out_specs, pl.BlockSpec((tm, n_la), lambda i, j: (i, 0))]
        out_shape = [out_shape, jax.ShapeDtypeStruct((t, n_la), F32)]
        args += list(decay_weights)
    return pl.pallas_call(
        functools.partial(_norm_matmul_kernel, relu2=relu2, decay=decay),
        grid=(t // tm, n // tn),
        in_specs=in_specs, out_specs=out_specs, out_shape=out_shape,
        scratch_shapes=[pltpu.VMEM((tm, d), BF16)],
        compiler_params=_cparams("parallel", "arbitrary"),
        name="norm_matmul_relu2" if relu2 else "norm_matmul",
    )(*args)


def _cumulative_log_decay(la, reverse):
    c = la.shape[0]
    row = lax.broadcasted_iota(jnp.int32, (c, c), 0)
    col = lax.broadcasted_iota(jnp.int32, (c, c), 1)
    tri = jnp.where((row <= col) if reverse else (row >= col), 1.0, 0.0).astype(BF16)
    hi = la.astype(BF16)
    rest = la - hi.astype(F32)
    mid = rest.astype(BF16)
    lo = (rest - mid.astype(F32)).astype(BF16)
    dot = lambda t: jnp.dot(tri, t, preferred_element_type=F32)
    return dot(hi) + dot(mid) + dot(lo)


def _gla_chunk_scores(q, k, v, b, *, reverse, exact):
    c = GLA_CHUNK
    n = GLA_SUB
    edge = b[0:1, :] if reverse else b[c - 1:c, :]
    qe = (q * jnp.exp(b)).astype(BF16)
    kd = (k * jnp.exp(edge - b)).astype(BF16)
    m = jnp.dot(v.T.astype(BF16), kd, preferred_element_type=F32)
    decay = jnp.exp(edge)

    lane = lax.broadcasted_iota(jnp.int32, (n, c), 1)
    sub_row = lax.broadcasted_iota(jnp.int32, (n, c), 0)
    blocks = []
    cap = 0.0 if exact else GLA_SAFE_EXP
    for i in range(c // n):
        lo, hi = i * n, (i + 1) * n
        q_i = q[lo:hi, :]
        b_i = b[lo:hi, :]
        has_border = (hi < c) if reverse else (lo > 0)
        if has_border:
            r = b[hi:hi + 1, :] if reverse else b[lo - 1:lo, :]
            lhs = (q_i * jnp.exp(b_i - r)).astype(BF16)
            rhs = (k * jnp.exp(jnp.minimum(r - b, cap))).astype(BF16)
        else:
            lhs = (q_i * jnp.exp(b_i)).astype(BF16)
            rhs = (k * jnp.exp(jnp.minimum(-b, cap))).astype(BF16)
        a_i = lax.dot_general(lhs, rhs, (((1,), (1,)), ((), ())), preferred_element_type=F32)
        if exact:
            a_diag = jnp.zeros((n, c), F32)
            for j in range(n):
                s = lo + j
                p = q_i * k[s:s + 1, :] * jnp.exp(b_i - b[s:s + 1, :])
                a_diag = jnp.where(lane == s, jnp.sum(p, axis=1, keepdims=True), a_diag)
            in_block = (lane >= lo) & (lane < hi)
            a_i = jnp.where(in_block, a_diag, a_i)
        causal = (lane >= sub_row + lo) if reverse else (lane <= sub_row + lo)
        blocks.append(jnp.where(causal, a_i, 0.0))
    a = jnp.concatenate(blocks, axis=0).astype(BF16)
    return qe, a, m, decay


def _gla_inter(qe, st):
    return lax.dot_general(qe, st.astype(BF16), (((1,), (1,)), ((), ())), preferred_element_type=F32)


def _gla_kernel(*refs, reverse, rope, has_s0, nchunk):
    it = iter(refs)
    q_ref, k_ref, v_ref, la_ref = next(it), next(it), next(it), next(it)
    cos_ref = sin_ref = s0_ref = None
    if rope:
        cos_ref, sin_ref = next(it), next(it)
    if has_s0:
        s0_ref = next(it)
    o_ref, sfin_ref, st_ref = next(it), next(it), next(it)

    blk = pl.program_id(2)

    @pl.when(blk == 0)
    def _():
        if has_s0:
            st_ref[...] = s0_ref[0, 0, 0, 0].T
        else:
            st_ref[...] = jnp.zeros(st_ref.shape, F32)

    def load_qk(rows):
        q = q_ref[rows, :] * (GLA_DK ** -0.5)
        k = k_ref[rows, :]
        if rope:
            cs, sn = cos_ref[rows, :], sin_ref[rows, :]
            q = q * cs + pltpu.roll(q, GLA_DK // 2, 1) * sn
            k = k * cs + pltpu.roll(k, GLA_DK // 2, 1) * sn
        return q, k

    def intra(a, rows):
        return jnp.dot(a, v_ref[rows, :].astype(BF16), preferred_element_type=F32)

    def run_exact():
        def body(step, carry):
            ci = (nchunk - 1 - step) if reverse else step
            rows = pl.ds(pl.multiple_of(ci * GLA_CHUNK, GLA_CHUNK), GLA_CHUNK)
            q, k = load_qk(rows)
            b = _cumulative_log_decay(la_ref[rows, :], reverse)
            qe, a, m, decay = _gla_chunk_scores(q, k, v_ref[rows, :], b, reverse=reverse, exact=True)
            o_ref[rows, :] = _gla_inter(qe, st_ref[...]) + intra(a, rows)
            st_ref[...] = st_ref[...] * decay + m
            return carry

        lax.fori_loop(0, nchunk, body, 0)

    def run_fast():
        order = list(range(nchunk))[::-1] if reverse else list(range(nchunk))
        rows = [pl.ds(ci * GLA_CHUNK, GLA_CHUNK) for ci in range(nchunk)]
        bs = [_cumulative_log_decay(la_ref[r, :], reverse) for r in rows]
        scores = [_gla_chunk_scores(*load_qk(r), v_ref[r, :], b, reverse=reverse, exact=False)
                  for r, b in zip(rows, bs)]
        o_intra = [intra(sc[1], r) for sc, r in zip(scores, rows)]
        st = st_ref[...]
        states = {}
        for ci in order:
            states[ci] = st
            st = st * scores[ci][3] + scores[ci][2]
        st_ref[...] = st
        for ci in range(nchunk):
            o_ref[rows[ci], :] = _gla_inter(scores[ci][0], states[ci]) + o_intra[ci]

    steep = jnp.min(la_ref[...]) < -(GLA_SAFE_EXP / GLA_SUB)
    pl.when(steep)(run_exact)
    pl.when(jnp.logical_not(steep))(run_fast)

    @pl.when(blk == pl.num_programs(2) - 1)
    def _():
        sfin_ref[0, 0] = st_ref[...].T


def gla_direction(proj, la, rope_tabs, s0, *, batch, seq, reverse, layer, direction, rows_blk):
    t = proj.shape[0]
    nblk = seq // rows_blk
    nchunk = rows_blk // GLA_CHUNK
    rope = rope_tabs is not None
    has_s0 = s0 is not None

    def rb(b, c):
        cc = (nblk - 1 - c) if reverse else c
        return b * nblk + cc

    kq, kv = OFF_GQ // GLA_DK, OFF_GV // GLA_DV
    kk = OFF_GK // GLA_DK
    kla = direction * GLA_HEADS
    in_specs = [pl.BlockSpec((rows_blk, GLA_DK), lambda b, h, c: (rb(b, c), kq + h)),
                pl.BlockSpec((rows_blk, GLA_DK), lambda b, h, c: (rb(b, c), kk + h)),
                pl.BlockSpec((rows_blk, GLA_DV), lambda b, h, c: (rb(b, c), kv + h)),
                pl.BlockSpec((rows_blk, GLA_DK), lambda b, h, c: (rb(b, c), kla + h))]
    args = [proj, proj, proj, la]
    if rope:
        tab_map = lambda b, h, c: ((nblk - 1 - c) if reverse else c, 0)
        in_specs += [pl.BlockSpec((rows_blk, GLA_DK), tab_map)] * 2
        args += list(rope_tabs)
    if has_s0:
        in_specs.append(pl.BlockSpec((1, 1, 1, 1, GLA_DK, GLA_DV), lambda b, h, c: (b, layer, direction, h, 0, 0)))
        args.append(s0)
    o, sfin = pl.pallas_call(
        functools.partial(_gla_kernel, reverse=reverse, rope=rope, has_s0=has_s0, nchunk=nchunk),
        grid=(batch, GLA_HEADS, nblk),
        in_specs=in_specs,
        out_specs=[pl.BlockSpec((rows_blk, GLA_DV), lambda b, h, c: (rb(b, c), h)),
                   pl.BlockSpec((1, 1, GLA_DK, GLA_DV), lambda b, h, c: (b, h, 0, 0))],
        out_shape=[jax.ShapeDtypeStruct((t, VAL_W), F32),
                   jax.ShapeDtypeStruct((batch, GLA_HEADS, GLA_DK, GLA_DV), F32)],
        scratch_shapes=[pltpu.VMEM((GLA_DV, GLA_DK), F32)],
        compiler_params=_cparams("parallel", "parallel", "arbitrary"),
        name="gla_bwd" if reverse else "gla_fwd",
    )(*args)
    return o, sfin


def _head_norm(x, g):
    return x * lax.rsqrt(jnp.mean(x * x, axis=-1, keepdims=True) + NORM_EPS) * g


def _ctx_attn_kernel(q_ref, k_ref, v_ref, gq_ref, gk_ref, o_ref, kn_ref, vn_ref):
    scale = NA_HD ** -0.5
    for h in range(NA_HEADS):
        cols = slice(h * NA_HD, (h + 1) * NA_HD)
        qn = _head_norm(q_ref[:, cols], gq_ref[...]) * scale
        kn = _head_norm(k_ref[:, cols], gk_ref[...])
        v = v_ref[:, cols]
        kn_ref[0, h] = kn
        vn_ref[0, h] = v
        s = lax.dot_general(qn.astype(BF16), kn.astype(BF16), (((1,), (1,)), ((), ())),
                            preferred_element_type=F32)
        e = jnp.exp(s - jnp.max(s, axis=-1, keepdims=True))
        p = e / jnp.sum(e, axis=-1, keepdims=True)
        o_ref[:, cols] = jnp.dot(p.astype(BF16), v.astype(BF16), preferred_element_type=F32)


def ctx_attention(proj, gq, gk, *, batch, seq):
    t = proj.shape[0]
    cq, ck, cv = OFF_NQ // NA_W, OFF_NK // NA_W, OFF_NV // NA_W
    cache_shape = jax.ShapeDtypeStruct((batch, NA_HEADS, seq, NA_HD), F32)
    cache_spec = pl.BlockSpec((1, NA_HEADS, seq, NA_HD), lambda b: (b, 0, 0, 0))
    return pl.pallas_call(
        _ctx_attn_kernel,
        grid=(batch,),
        in_specs=[pl.BlockSpec((seq, NA_W), lambda b: (b, cq)),
                  pl.BlockSpec((seq, NA_W), lambda b: (b, ck)),
                  pl.BlockSpec((seq, NA_W), lambda b: (b, cv)),
                  pl.BlockSpec((1, NA_HD), lambda b: (0, 0)),
                  pl.BlockSpec((1, NA_HD), lambda b: (0, 0))],
        out_specs=[pl.BlockSpec((seq, NA_W), lambda b: (b, 0)), cache_spec, cache_spec],
        out_shape=[jax.ShapeDtypeStruct((t, NA_W), F32), cache_shape, cache_shape],
        compiler_params=_cparams("parallel"),
        name="ctx_attention",
    )(proj, proj, proj, gq, gk)


def _na_kernel(q_ref, k_ref, v_ref, kc_ref, vc_ref, gq_ref, gk_ref, tab_ref, o_ref, qn_s, kn_s, v_s, *, rows):
    scale = NA_HD ** -0.5
    qn_s[...] = (_head_norm(q_ref[...], gq_ref[...]) * scale).astype(BF16)
    kn_s[...] = _head_norm(k_ref[...], gk_ref[...]).astype(BF16)
    v_s[...] = v_ref[...].astype(BF16)
    kctx = kc_ref[0, 0, 0].astype(BF16)
    vctx = vc_ref[0, 0, 0].astype(BF16)
    nblk = rows // NA_QROWS
    qlen, win = NA_QROWS * GRID_W, NA_UNION * GRID_W

    def scores(g):
        u0 = jnp.clip(g * NA_QROWS - NA_KH // 2, 0, rows - NA_UNION)
        cls = jnp.where(g == 0, 0, jnp.where(g == nblk - 1, 2, 1))
        qrows = pl.ds(pl.multiple_of(g * qlen, qlen), qlen)
        wrows = pl.ds(pl.multiple_of(u0 * GRID_W, GRID_W), win)
        qn = qn_s[qrows, :]
        s_loc = lax.dot_general(qn, kn_s[wrows, :], (((1,), (1,)), ((), ())), preferred_element_type=F32)
        s_ctx = lax.dot_general(qn, kctx, (((1,), (1,)), ((), ())), preferred_element_type=F32)
        return qrows, wrows, s_loc + tab_ref[0, cls], s_ctx

    def softmax(s_loc, s_ctx):
        m = jnp.maximum(jnp.max(s_loc, axis=-1, keepdims=True), jnp.max(s_ctx, axis=-1, keepdims=True))
        e_loc = jnp.exp(s_loc - m)
        e_ctx = jnp.exp(s_ctx - m)
        den = jnp.sum(e_loc, axis=-1, keepdims=True) + jnp.sum(e_ctx, axis=-1, keepdims=True)
        return e_loc.astype(BF16), e_ctx.astype(BF16), den

    def body(step, carry):
        blocks = [scores(step * NA_BLOCK_UNROLL + u) for u in range(NA_BLOCK_UNROLL)]
        probs = [softmax(s_loc, s_ctx) for _, _, s_loc, s_ctx in blocks]
        for (qrows, wrows, _, _), (e_loc, e_ctx, den) in zip(blocks, probs):
            acc = jnp.dot(e_loc, v_s[wrows, :], preferred_element_type=F32)
            acc = acc + jnp.dot(e_ctx, vctx, preferred_element_type=F32)
            o_ref[qrows, :] = acc / den
        return carry

    lax.fori_loop(0, nblk // NA_BLOCK_UNROLL, body, 0)


def neighbourhood_attention(proj, cache_k, cache_v, gq, gk, table, *, batch, seq, layer, row0):
    rows = seq // GRID_W
    assert rows % NA_QROWS == 0 and rows // NA_QROWS >= 3 and rows >= NA_UNION
    past = cache_k.shape[3]
    cq, ck, cv = OFF_NQ // NA_HD, OFF_NK // NA_HD, OFF_NV // NA_HD
    cache_spec = pl.BlockSpec((1, 1, 1, past, NA_HD), lambda b, h: (b, layer, h, 0, 0))
    return pl.pallas_call(
        functools.partial(_na_kernel, rows=rows),
        grid=(batch, NA_HEADS),
        in_specs=[pl.BlockSpec((seq, NA_HD), lambda b, h: (row0 + b, cq + h)),
                  pl.BlockSpec((seq, NA_HD), lambda b, h: (row0 + b, ck + h)),
                  pl.BlockSpec((seq, NA_HD), lambda b, h: (row0 + b, cv + h)),
                  cache_spec, cache_spec,
                  pl.BlockSpec((1, NA_HD), lambda b, h: (0, 0)),
                  pl.BlockSpec((1, NA_HD), lambda b, h: (0, 0)),
                  pl.BlockSpec((1, 3, NA_QROWS * GRID_W, NA_UNION * GRID_W), lambda b, h: (h, 0, 0, 0))],
        out_specs=pl.BlockSpec((seq, NA_HD), lambda b, h: (b, h)),
        out_shape=jax.ShapeDtypeStruct((batch * seq, NA_W), F32),
        scratch_shapes=[pltpu.VMEM((seq, NA_HD), BF16)] * 3,
        compiler_params=_cparams("parallel", "parallel"),
        name="neighbourhood_attention",
    )(proj, proj, proj, cache_k, cache_v, gq, gk, table)


def na_bias_table(rpb, rows):
    nblk = rows // NA_QROWS
    col = np.arange(GRID_W)
    c_start = np.clip(col - NA_KW // 2, 0, GRID_W - NA_KW)
    col_in = (col[None, :] >= c_start[:, None]) & (col[None, :] < c_start[:, None] + NA_KW)
    col_idx = np.clip(col[None, :] - col[:, None] + NA_KW - 1, 0, 2 * NA_KW - 2)
    row_off, row_in = [], []
    for g in (0, 1, nblk - 1):
        u0 = np.clip(g * NA_QROWS - NA_KH // 2, 0, rows - NA_UNION)
        r = g * NA_QROWS + np.arange(NA_QROWS)[:, None]
        r_start = np.clip(r - NA_KH // 2, 0, rows - NA_KH)
        key_row = u0 + np.arange(NA_UNION)[None, :]
        row_in.append((key_row >= r_start) & (key_row < r_start + NA_KH))
        row_off.append(np.clip(key_row - r + NA_KH - 1, 0, 2 * NA_KH - 2))
    row_off, row_in = np.stack(row_off), np.stack(row_in)
    n_off = 2 * NA_KH - 1
    tiles = jnp.where(col_in[None, None], rpb[:, :, col_idx], -jnp.inf)
    tiles = jnp.concatenate([tiles, jnp.full_like(tiles[:, :1], -jnp.inf)], axis=1)
    tab = tiles[:, np.where(row_in, row_off, n_off)]
    tab = jnp.transpose(tab, (0, 1, 2, 4, 3, 5))
    return tab.reshape(rpb.shape[0], 3, NA_QROWS * GRID_W, NA_UNION * GRID_W).astype(F32)


def _filter_kernel(z_ref, f1w_ref, f1b_ref, f2w_ref, f2b_ref, f3w_ref, fr_ref, dl_ref, o_ref, *, seq):
    fr = fr_ref[...]
    h = jnp.sin(fr * (jnp.dot(z_ref[...], f1w_ref[...], preferred_element_type=F32, precision=HIGHEST)
                      + f1b_ref[...]))
    h = jnp.sin(fr * (jnp.dot(h, f2w_ref[...], preferred_element_type=F32, precision=HIGHEST)
                      + f2b_ref[...]))
    h = jnp.dot(h, f3w_ref[...], preferred_element_type=F32, precision=HIGHEST)
    t = lax.broadcasted_iota(jnp.int32, (seq, 1), 0).astype(F32)
    dist = jnp.abs(t - float(seq // 2)) * (2.0 / seq)
    h = h * jnp.exp(-dist * dl_ref[...])
    o_ref[...] = h * lax.rsqrt(jnp.sum(h * h, axis=0, keepdims=True) + NORM_EPS)


def hyena_filters(zpos, f1w, f1b, f2w, f2b, f3w, freq, deltas, *, seq):
    n = f3w.shape[1]
    tc = 512
    full = lambda a: pl.BlockSpec(a.shape, lambda j: (0,) * a.ndim)
    return pl.pallas_call(
        functools.partial(_filter_kernel, seq=seq),
        grid=(n // tc,),
        in_specs=[full(zpos), full(f1w), full(f1b), full(f2w), full(f2b),
                  pl.BlockSpec((HY_HIDDEN, tc), lambda j: (0, j)), full(freq),
                  pl.BlockSpec((1, tc), lambda j: (0, j))],
        out_specs=pl.BlockSpec((seq, tc), lambda j: (0, j)),
        out_shape=jax.ShapeDtypeStruct((seq, n), F32),
        compiler_params=_cparams("parallel"),
        name="hyena_filters",
    )(zpos, f1w, f1b, f2w, f2b, f3w, freq, deltas)


def _lane_tile(t, width):
    return jnp.concatenate([t] * (width // LANES), axis=1) if width > LANES else t


def _cdft(vals, sign):
    if len(vals) == 1:
        return vals
    if len(vals) == 2:
        (ar, ai), (br, bi) = vals
        return [(ar + br, ai + bi), (ar - br, ai - bi)]
    assert len(vals) == 4
    (x0r, x0i), (x1r, x1i), (x2r, x2i), (x3r, x3i) = vals
    a0r, a0i, a1r, a1i = x0r + x2r, x0i + x2i, x0r - x2r, x0i - x2i
    a2r, a2i, a3r, a3i = x1r + x3r, x1i + x3i, x1r - x3r, x1i - x3i
    wr, wi = -sign * a3i, sign * a3r
    return [(a0r + a2r, a0i + a2i), (a1r + wr, a1i + wi), (a0r - a2r, a0i - a2i), (a1r - wr, a1i - wi)]


def _fwd_spectrum(z_list, f_ref, tw_ref, rows, width):
    lp = f_ref.shape[1]
    f_cos = f_ref[rows, :]
    f_sin = f_ref[pl.ds(lp + rows.start, rows.size), :]
    xs = []
    for r, z in enumerate(z_list):
        xc = jnp.dot(f_cos, z, preferred_element_type=F32)
        xn = jnp.dot(f_sin, z, preferred_element_type=F32)
        if r == 0:
            xs.append((xc, -xn))
        else:
            c = _lane_tile(tw_ref[r, 0, rows, :], width)
            s = _lane_tile(tw_ref[r, 1, rows, :], width)
            xs.append((xc * c - xn * s, -(xn * c + xc * s)))
    return _cdft(xs, -1)


def _spectrum_kernel(*refs, radix, seq, kc, tc):
    h_refs, (f_ref, tw_ref, sh_ref, o_ref) = refs[:radix], refs[radix:]
    lp = seq // radix
    hs = [h[...].astype(BF16) for h in h_refs]
    for c0 in range(0, lp, kc):
        rows = pl.ds(c0, kc)
        cs = _lane_tile(sh_ref[0, rows, :], tc) * (1.0 / seq)
        sn = _lane_tile(sh_ref[1, rows, :], tc) * (1.0 / seq)
        for q, (re, im) in enumerate(_fwd_spectrum(hs, f_ref, tw_ref, rows, tc)):
            o_ref[0, q, rows, :] = re * cs - im * sn
            o_ref[1, q, rows, :] = re * sn + im * cs


def _whole(a):
    return pl.BlockSpec(a.shape, lambda *_: (0,) * a.ndim, pipeline_mode=pl.Buffered(1))


def filter_spectrum(filt, tabs, *, seq):
    n = filt.shape[1]
    radix, tc = _hy_radix(seq), _hy_fft_tc(seq)
    lp = seq // radix
    view = filt.reshape(lp, radix * n)
    f, _, tw, sh = tabs
    groups = [pl.BlockSpec((lp, tc), functools.partial(lambda j, r: (0, r * (n // tc) + j), r=r)) for r in range(radix)]
    return pl.pallas_call(
        functools.partial(_spectrum_kernel, radix=radix, seq=seq, kc=min(HY_KC, lp), tc=tc),
        grid=(n // tc,),
        in_specs=groups + [_whole(f), _whole(tw), _whole(sh)],
        out_specs=pl.BlockSpec((2, radix, lp, tc), lambda j: (0, 0, 0, j)),
        out_shape=jax.ShapeDtypeStruct((2, radix, lp, n), F32),
        compiler_params=_cparams("parallel"),
        name="filter_spectrum",
    )(*([view] * radix), f, tw, sh)


def _short_conv(u, w, bias, seq):
    t = lax.broadcasted_iota(jnp.int32, (seq, 1), 0)
    prev = jnp.where(t == 0, 0.0, pltpu.roll(u, 1, 0))
    nxt = jnp.where(t == seq - 1, 0.0, pltpu.roll(u, seq - 1, 0))
    return bias + prev * w[0:1, :] + u * w[1:2, :] + nxt * w[2:3, :]


def _hy_radix(seq):
    return HY_RADIX if seq >= HY_RADIX_MIN_SEQ else 1


def _split_rows(val, out_refs, tmp_ref):
    radix = len(out_refs)
    if radix == 1:
        out_refs[0][...] = val.astype(BF16)
        return
    tmp_ref[...] = val
    for r in range(radix):
        out_refs[r][...] = tmp_ref[pl.ds(r, val.shape[0] // radix, stride=radix), :].astype(BF16)


def _merge_rows(in_refs, tmp_ref):
    radix = len(in_refs)
    if radix == 1:
        return in_refs[0][...]
    for r in range(radix):
        tmp_ref[pl.ds(r, in_refs[r].shape[0], stride=radix), :] = in_refs[r][...]
    return tmp_ref[...]


def _hy_pre_kernel(*refs, seq, radix):
    (u_ref, w_ref, b_ref), z_refs, tmp = refs[:3], refs[3:3 + radix], refs[3 + radix:]
    z = _short_conv(u_ref[...], w_ref[...], b_ref[...], seq)
    _split_rows(z, z_refs, tmp[0] if tmp else None)


def _hy_gate_kernel(*refs, seq, radix):
    ux_ref, uz_ref, wx_ref, bx_ref, wz_ref, bz_ref = refs[:6]
    cv_refs, hb_ref, zo_ref = refs[6:6 + radix], refs[6 + radix], refs[7 + radix]
    zb_refs, tmp = refs[8 + radix:8 + 2 * radix], refs[8 + 2 * radix:]
    tmp_ref = tmp[0] if tmp else None
    x = _short_conv(ux_ref[...], wx_ref[...], bx_ref[...], seq)
    z = _short_conv(uz_ref[...], wz_ref[...], bz_ref[...], seq)
    out = x * (_merge_rows(cv_refs, tmp_ref) + hb_ref[...] * z)
    zo_ref[...] = out
    _split_rows(out, zb_refs, tmp_ref)


def _hy_final_kernel(*refs, seq, radix):
    (ux_ref, wx_ref, bx_ref), cv_refs = refs[:3], refs[3:3 + radix]
    hb_ref, z_ref, zb_ref = refs[3 + radix:6 + radix]
    tmp = refs[6 + radix:]
    x = _short_conv(ux_ref[...], wx_ref[...], bx_ref[...], seq)
    cv = _merge_rows(cv_refs, tmp[0] if tmp else None)
    zb_ref[...] = (x * (cv + hb_ref[...] * z_ref[...])).astype(BF16)


HY_ELEM_SLAB_BYTES = 2 * 1024 * 1024


def _hy_tc(seq):
    if _hy_radix(seq) > 1:
        return LANES
    return max(LANES, min(HY_W, HY_ELEM_SLAB_BYTES // (4 * seq)))


def _hy_specs(seq, row0, col0):
    tc = _hy_tc(seq)
    cb, wb = col0 // tc, (col0 - OFF_HY) // tc
    u = pl.BlockSpec((seq, tc), lambda b, j: (row0 + b, cb + j))
    w = pl.BlockSpec((3, tc), lambda b, j: (0, wb + j))
    bb = pl.BlockSpec((1, tc), lambda b, j: (0, wb + j))
    act = pl.BlockSpec((seq, tc), lambda b, j: (b, j))
    hb = pl.BlockSpec((1, tc), lambda b, j: (0, j))
    part = pl.BlockSpec((seq // _hy_radix(seq), tc), lambda b, j: (b, j))
    return u, w, bb, act, hb, part


def _hy_elementwise_call(kernel, name, seq, batch, in_specs, out_specs, out_shape, args):
    radix, tc = _hy_radix(seq), _hy_tc(seq)
    assert radix == 1 or tc == LANES
    return pl.pallas_call(
        functools.partial(kernel, seq=seq, radix=radix),
        grid=(batch, HY_W // tc),
        in_specs=in_specs, out_specs=out_specs, out_shape=out_shape,
        scratch_shapes=[pltpu.VMEM((seq, tc), F32)] if radix > 1 else [],
        compiler_params=_cparams("parallel", "parallel"), name=name,
    )(*args)


def hyena_pre(proj, conv_w, conv_b, *, batch, seq, row0):
    radix = _hy_radix(seq)
    u, w, bb, _, _, part = _hy_specs(seq, row0, OFF_HY + HY_ORDER * HY_W)
    parts = [jax.ShapeDtypeStruct((batch * seq // radix, HY_W), BF16)] * radix
    return _hy_elementwise_call(_hy_pre_kernel, "hyena_pre", seq, batch, [u, w, bb], [part] * radix, parts,
                                (proj, conv_w, conv_b))


def hyena_gate(proj, conv_w, conv_b, conv, hy_bias, *, batch, seq, row0):
    radix = _hy_radix(seq)
    ux, wx, bx, act, hb, part = _hy_specs(seq, row0, OFF_HY)
    uz, wz, bz, _, _, _ = _hy_specs(seq, row0, OFF_HY + HY_ORDER * HY_W)
    parts = [jax.ShapeDtypeStruct((batch * seq // radix, HY_W), BF16)] * radix
    outs = _hy_elementwise_call(
        _hy_gate_kernel, "hyena_gate", seq, batch,
        [ux, uz, wx, bx, wz, bz] + [part] * radix + [hb], [act] + [part] * radix,
        [jax.ShapeDtypeStruct((batch * seq, HY_W), F32)] + parts,
        (proj, proj, conv_w, conv_b, conv_w, conv_b, *conv, hy_bias[0:1]))
    return outs[0], outs[1:]


def hyena_final(proj, conv_w, conv_b, conv, hy_bias, z1, *, batch, seq, row0):
    radix = _hy_radix(seq)
    ux, wx, bx, act, hb, part = _hy_specs(seq, row0, OFF_HY + HY_W)
    return _hy_elementwise_call(
        _hy_final_kernel, "hyena_final", seq, batch, [ux, wx, bx] + [part] * radix + [hb, act], act,
        jax.ShapeDtypeStruct((batch * seq, HY_W), BF16), (proj, conv_w, conv_b, *conv, hy_bias[1:2], z1))


def _long_conv_kernel(*refs, radix, seq, kc, tc):
    z_refs, (f_ref, ft_ref, tw_ref, g_ref) = refs[:radix], refs[radix:radix + 4]
    o_refs, u_s = refs[radix + 4:2 * radix + 4], refs[2 * radix + 4]
    lp = seq // radix
    r_out = pl.program_id(2)

    @pl.when(r_out == 0)
    def _():
        zs = [z[...] for z in z_refs]
        for c0 in range(0, lp, kc):
            rows = pl.ds(c0, kc)
            ys = []
            for q, (zr, zi) in enumerate(_fwd_spectrum(zs, f_ref, tw_ref, rows, tc)):
                gr, gi = g_ref[0, q, rows, :], g_ref[1, q, rows, :]
                ys.append((zr * gr - zi * gi, zr * gi + zi * gr))
            for r, (re, im) in enumerate(_cdft(ys, +1)):
                if r > 0:
                    c = _lane_tile(tw_ref[r, 0, rows, :], tc)
                    s = _lane_tile(tw_ref[r, 1, rows, :], tc)
                    re, im = re * c - im * s, re * s + im * c
                u_s[r, rows, :] = re.astype(BF16)
                u_s[r, pl.ds(lp + c0, kc), :] = (-im).astype(BF16)

    for r in range(radix):
        @pl.when(r_out == r)
        def _(r=r):
            o_refs[r][...] = jnp.dot(ft_ref[...], u_s[r], preferred_element_type=F32)


def _hy_fft_tc(seq):
    return HY_TC_FFT if _hy_radix(seq) > 1 else 2 * HY_TC_FFT


def hyena_long_conv(z_parts, g, tabs, *, batch, seq, order):
    c = z_parts[0].shape[1]
    radix, tc = _hy_radix(seq), _hy_fft_tc(seq)
    lp = seq // radix
    f, ft, tw, _ = tabs
    ncb = c // tc
    goff = order * ncb
    part = pl.BlockSpec((lp, tc), lambda j, b, ro: (b, j))
    return pl.pallas_call(
        functools.partial(_long_conv_kernel, radix=radix, seq=seq, kc=min(HY_KC, lp), tc=tc),
        grid=(ncb, batch, radix),
        in_specs=[part] * radix + [_whole(f), _whole(ft), _whole(tw),
                                   pl.BlockSpec((2, radix, lp, tc), lambda j, b, ro: (0, 0, 0, goff + j),
                                                pipeline_mode=pl.Buffered(1))],
        out_specs=[part] * radix,
        out_shape=[jax.ShapeDtypeStruct((batch * lp, c), F32)] * radix,
        scratch_shapes=[pltpu.VMEM((radix, 2 * lp, tc), BF16)],
        compiler_params=_cparams("parallel", "parallel", "arbitrary"),
        name="hyena_long_conv",
    )(*z_parts, f, ft, tw, g)


def dft_tables(seq):
    radix = _hy_radix(seq)
    lp = seq // radix
    odd =2 * jnp.arange(lp, dtype=jnp.int32) + 1

    def cs(num, den):
        ang = (num % (2 * den)).astype(F32) * (math.pi / den)
        return jnp.cos(ang), jnp.sin(ang)

    fc, fs = cs(odd[:, None] * jnp.arange(lp, dtype=jnp.int32)[None, :], 2 * lp)
    f = jnp.concatenate([fc, fs], axis=0).astype(BF16)
    tc_, ts_ = cs(odd[None, :] * jnp.arange(radix, dtype=jnp.int32)[:, None], 2 * seq)
    tw = jnp.broadcast_to(jnp.stack([tc_, ts_], axis=1)[..., None], (radix, 2, lp, LANES))
    sc, ss = cs(odd, 4)
    sh = jnp.broadcast_to(jnp.stack([sc, ss], axis=0)[..., None], (2, lp, LANES))
    return f, f.T, tw, sh


def position_features(seq):
    t = jnp.arange(seq, dtype=F32)
    tn = t / seq
    bands = jnp.arange(1, HY_BANDS + 1, dtype=F32)
    ang = 2.0 * jnp.pi * tn[:, None] * bands[None, :]
    z = jnp.concatenate([tn[:, None], jnp.cos(ang), jnp.sin(ang)], axis=-1)
    return jnp.pad(z, ((0, 0), (0, LANES - z.shape[1])))


def rope_tables(seq):
    n_freq = GLA_DK // 4
    pos = jnp.arange(seq)
    row = (pos // GRID_W).astype(F32)
    col = (pos % GRID_W).astype(F32)
    inv = ROPE_BASE ** (-jnp.arange(n_freq, dtype=F32) / n_freq)
    ang = jnp.concatenate([row[:, None] * inv, col[:, None] * inv], axis=-1)
    cos, sin = jnp.cos(ang), jnp.sin(ang)
    return jnp.concatenate([cos, cos], axis=-1), jnp.concatenate([-sin, sin], axis=-1)


def _merge_kernel(of_ref, ob_ref, gg_ref, gn_ref, nb_ref, hc_ref, ga_ref, gb_ref, gc_ref,
                  wa_ref, wb_ref, wc_ref, o_ref, a_s):
    @pl.when(pl.program_id(1) == 0)
    def _():
        for h in range(GLA_HEADS):
            cols = slice(h * GLA_DV, (h + 1) * GLA_DV)
            oa = of_ref[:, cols] + ob_ref[:, cols]
            oa = _head_norm(oa, gn_ref[...])
            gg = gg_ref[:, cols]
            a_s[:, cols] = (oa * (gg * jax.nn.sigmoid(gg))).astype(BF16)

    ya = jnp.dot(a_s[...], wa_ref[...], preferred_element_type=F32)
    yb = jnp.dot(nb_ref[...].astype(BF16), wb_ref[...], preferred_element_type=F32)
    yc = jnp.dot(hc_ref[...], wc_ref[...], preferred_element_type=F32)
    merged = (jax.nn.sigmoid(ga_ref[...]) * ya + jax.nn.sigmoid(gb_ref[...]) * yb
              + jax.nn.sigmoid(gc_ref[...]) * yc)
    o_ref[...] = merged.astype(o_ref.dtype)


def branch_merge(o_f, o_b, proj, gnorm, o_attn, o_hy, w_a, w_b, w_c, *, row0_blocks, tm, tn):
    t = o_f.shape[0]
    d = w_a.shape[1]
    assert OFF_GATES % tn == 0 and d % tn == 0
    act = lambda w: pl.BlockSpec((tm, w), lambda i, j: (i, 0))
    gate = lambda g: pl.BlockSpec((tm, tn), lambda i, j: (row0_blocks + i, (OFF_GATES + g * d) // tn + j))
    wspec = lambda w: pl.BlockSpec((w.shape[0], tn), lambda i, j: (0, j))
    return pl.pallas_call(
        _merge_kernel,
        grid=(t // tm, d // tn),
        in_specs=[act(VAL_W), act(VAL_W),
                  pl.BlockSpec((tm, VAL_W), lambda i, j: (row0_blocks + i, OFF_GG // VAL_W)),
                  pl.BlockSpec((1, GLA_DV), lambda i, j: (0, 0)),
                  act(NA_W), act(HY_W), gate(0), gate(1), gate(2),
                  wspec(w_a), wspec(w_b), wspec(w_c)],
        out_specs=pl.BlockSpec((tm, tn), lambda i, j: (i, j)),
        out_shape=jax.ShapeDtypeStruct((t, d), BF16),
        scratch_shapes=[pltpu.VMEM((tm, VAL_W), BF16)],
        compiler_params=_cparams("parallel", "arbitrary"),
        name="branch_merge",
    )(o_f, o_b, proj, gnorm, o_attn, o_hy, proj, proj, proj, w_a, w_b, w_c)


def _matmul_residual_kernel(a_ref, w_ref, g_ref, r_ref, o_ref):
    y = jnp.dot(a_ref[...], w_ref[...], preferred_element_type=F32)
    o_ref[...] = r_ref[...] + g_ref[0] * y


def matmul_residual(a, w, gate, resid, *, rows_per_mod, tm, tn):
    t, k = a.shape
    n = w.shape[1]
    return pl.pallas_call(
        _matmul_residual_kernel,
        grid=(t // tm, n // tn),
        in_specs=[pl.BlockSpec((tm, k), lambda i, j: (i, 0)),
                  pl.BlockSpec((k, tn), lambda i, j: (0, j)),
                  pl.BlockSpec((1, 1, tn), lambda i, j: ((i * tm) // rows_per_mod, 0, j)),
                  pl.BlockSpec((tm, tn), lambda i, j: (i, j))],
        out_specs=pl.BlockSpec((tm, tn), lambda i, j: (i, j)),
        out_shape=jax.ShapeDtypeStruct((t, n), F32),
        compiler_params=_cparams("parallel", "arbitrary"),
        name="matmul_residual",
    )(a, w, gate, resid)


def _trunk_layer(x, mods, p, consts, *, batch, seq, layer, latent):
    sh1, sc1, g1, sh2, sc2, g2 = mods
    t = x.shape[0]
    rows_per_mod = t // sh1.shape[0]
    tm = 512

    proj, la = norm_matmul(x, p["norm1_g"], sc1, sh1, p["w_in_main"], rows_per_mod=rows_per_mod,
                           tm=tm, tn=1536, relu2=False, out_dtype=F32,
                           decay_weights=(p["w_in_ga"], p["wa2_bd"], p["ba2"]))

    rows_blk = min(seq, 512)
    rope = consts["rope"] if latent else None
    s0 = consts["state_gla"] if latent else None
    o_f, s_f = gla_direction(proj, la, rope, s0, batch=batch, seq=seq, reverse=False, layer=layer,
                             direction=0, rows_blk=rows_blk)
    o_b, s_b = gla_direction(proj, la, rope, s0, batch=batch, seq=seq, reverse=True, layer=layer,
                             direction=1, rows_blk=rows_blk)

    if latent:
        o_attn = neighbourhood_attention(proj, consts["cache_k"], consts["cache_v"], p["na_qnorm_g"],
                                         p["na_knorm_g"], p["na_table"], batch=batch, seq=seq,
                                         layer=layer, row0=0)
        new_ctx = None
    else:
        o_attn, kn, vn = ctx_attention(proj, p["na_qnorm_g"], p["na_knorm_g"], batch=batch, seq=seq)
        new_ctx = (jnp.stack([s_f, s_b], axis=1), kn, vn)

    key = "lat" if latent else "ctx"
    tabs = consts["dft_" + key]
    g_spec = p["spectrum_" + key]
    z0 = hyena_pre(proj, p["hy_conv_w"], p["hy_conv_b"], batch=batch, seq=seq, row0=0)
    cv0 = hyena_long_conv(z0, g_spec, tabs, batch=batch, seq=seq, order=0)
    z1, z1b = hyena_gate(proj, p["hy_conv_w"], p["hy_conv_b"], cv0, p["hy_bias"], batch=batch, seq=seq, row0=0)
    cv1 = hyena_long_conv(z1b, g_spec, tabs, batch=batch, seq=seq, order=1)
    o_hy = hyena_final(proj, p["hy_conv_w"], p["hy_conv_b"], cv1, p["hy_bias"], z1, batch=batch, seq=seq, row0=0)

    merged = branch_merge(o_f, o_b, proj, p["gla_norm_g"], o_attn, o_hy, p["w_br_a"], p["w_br_b"],
                          p["w_br_c"], row0_blocks=0, tm=256, tn=2048)
    x = matmul_residual(merged, p["w_out"], g1, x, rows_per_mod=rows_per_mod, tm=tm, tn=2048)

    u = norm_matmul(x, p["norm2_g"], sc2, sh2, p["w_mlp1"], rows_per_mod=rows_per_mod,
                    tm=tm, tn=2048, relu2=True, out_dtype=BF16)
    x = matmul_residual(u, p["w_mlp2"], g2, x, rows_per_mod=rows_per_mod, tm=tm, tn=512)
    return x, new_ctx


def _reorder_w_in(w_in):
    o_ga = 2 * KEY_W + 2 * VAL_W
    n_ga = 2 * GLA_LOWRANK
    o_hy = o_ga + n_ga + 3 * NA_W
    o_gates = o_hy + (HY_ORDER + 1) * HY_W
    main = jnp.concatenate([w_in[..., :o_ga], w_in[..., o_ga + n_ga:o_hy], w_in[..., o_gates:],
                            w_in[..., o_hy:o_gates]], axis=-1).astype(BF16)
    ga = jnp.pad(w_in[..., o_ga:o_ga + n_ga], ((0, 0), (0, LANES - n_ga))).astype(BF16)
    return main, ga


def kernel(x_prompt, x_sample, state_gla, cache_na_k, cache_na_v, c, c_ctx, w_mod, b_mod, norm1_g, norm2_g, w_in, gla_wa2, gla_ba2, gla_norm_g, na_qnorm_g, na_knorm_g, na_rpb, hy_conv_w, hy_conv_b, hy_f1_w, hy_f1_b, hy_f2_w, hy_f2_b, hy_f3_w, hy_freq, hy_bias, w_br_a, w_br_b, w_br_c, w_out, w_mlp1, w_mlp2):
    batch, seq, d = x_prompt.shape
    dec_batch, dec_seq, _ = x_sample.shape
    depth = w_mod.shape[0]

    bf = lambda w: w.astype(BF16)
    wa2_bd = jnp.zeros((depth, LANES, 2 * KEY_W), F32)
    wa2_bd = wa2_bd.at[:, :GLA_LOWRANK, :KEY_W].set(gla_wa2[:, 0])
    wa2_bd = wa2_bd.at[:, GLA_LOWRANK:2 * GLA_LOWRANK, KEY_W:].set(gla_wa2[:, 1])
    ba2 = gla_ba2.reshape(depth, 1, 2 * KEY_W)
    f1w = jnp.pad(hy_f1_w, ((0, 0), (0, LANES - hy_f1_w.shape[1]), (0, 0)))

    consts = {
        "rope": rope_tables(dec_seq),
        "state_gla": state_gla, "cache_k": cache_na_k, "cache_v": cache_na_v,
        "dft_ctx": dft_tables(seq), "dft_lat": dft_tables(dec_seq),
    }
    deltas = jnp.abs(jnp.linspace(math.log(HY_DECAY_TARGET) / HY_DECAY_PCT_LONG,
                                  math.log(HY_DECAY_TARGET) / HY_DECAY_PCT_SHORT, HY_W, dtype=F32))
    deltas = jnp.tile(deltas, HY_ORDER)[None, :]
    zpos = {"ctx": position_features(seq), "lat": position_features(dec_seq)}

    cond = jnp.zeros((16, d), F32).at[0].set(c_ctx).at[1:1 + dec_batch].set(c)

    xp = x_prompt.reshape(batch * seq, d)
    xs = x_sample.reshape(dec_batch * dec_seq, d)
    new_gla, new_k, new_v = [], [], []
    for l in range(depth):
        m = adaln(cond, w_mod, b_mod[l][None, :], l)
        mods_ctx = tuple(m[0:1, i * d:(i + 1) * d][:, None, :] for i in range(N_MOD))
        mods_lat = tuple(m[1:1 + dec_batch, i * d:(i + 1) * d][:, None, :] for i in range(N_MOD))
        w_in_main, w_in_ga = _reorder_w_in(w_in[l])
        p = dict(norm1_g=norm1_g[l][None], norm2_g=norm2_g[l][None], w_in_main=w_in_main, w_in_ga=w_in_ga,
                 wa2_bd=wa2_bd[l], ba2=ba2[l], gla_norm_g=gla_norm_g[l][None], na_qnorm_g=na_qnorm_g[l][None],
                 na_knorm_g=na_knorm_g[l][None], na_table=na_bias_table(na_rpb[l], dec_seq // GRID_W),
                 hy_conv_w=hy_conv_w[l], hy_conv_b=hy_conv_b[l][None], hy_bias=hy_bias[l],
                 w_br_a=bf(w_br_a[l]), w_br_b=bf(w_br_b[l]), w_br_c=bf(w_br_c[l]), w_out=bf(w_out[l]),
                 w_mlp1=bf(w_mlp1[l]), w_mlp2=bf(w_mlp2[l]))
        for key, length in (("ctx", seq), ("lat", dec_seq)):
            filt = hyena_filters(zpos[key], f1w[l], hy_f1_b[l][None], hy_f2_w[l], hy_f2_b[l][None],
                                 hy_f3_w[l], hy_freq[l][None], deltas, seq=length)
            p["spectrum_" + key] = filter_spectrum(filt, consts["dft_" + key], seq=length)
        xp, (s_gla, k_ctx, v_ctx) = _trunk_layer(xp, mods_ctx, p, consts, batch=batch, seq=seq, layer=l, latent=False)
        new_gla.append(s_gla)
        new_k.append(k_ctx)
        new_v.append(v_ctx)
        xs, _ = _trunk_layer(xs, mods_lat, p, consts, batch=dec_batch, seq=dec_seq, layer=l, latent=True)
    return (xp.reshape(batch, seq, d), xs.reshape(dec_batch, dec_seq, d),
            jnp.stack(new_gla, axis=1), jnp.stack(new_k, axis=1), jnp.stack(new_v, axis=1))
```

```python
import functools
import math

import jax
import jax.numpy as jnp
import numpy as np
from jax import lax
from jax.experimental import pallas as pl
from jax.experimental.pallas import tpu as pltpu

F32 = jnp.float32
BF16 = jnp.bfloat16
HIGHEST = lax.Precision.HIGHEST

GRID_W = 64
N_MOD = 6
NORM_EPS = 1e-6
GLA_HEADS = 4
GLA_DK = 128
GLA_DV = 256
GLA_LOWRANK = 16
GLA_TAU = 16.0
GLA_CHUNK = 64
GLA_SUB = 16
GLA_HEADS_PER_STEP = 2
GLA_SAFE_EXP = 60.0
ROPE_BASE = 10000.0
NA_HEADS = 8
NA_HD = 128
NA_KH = 8
NA_KW = 16
NA_QROWS = 4
NA_UNION = NA_KH + NA_QROWS
NA_BLOCK_UNROLL = 2
HY_W = 1024
HY_ORDER = 2
HY_BANDS = 16
HY_HIDDEN = 64
HY_DECAY_TARGET = 1e-2
HY_DECAY_PCT_SHORT = 0.3
HY_DECAY_PCT_LONG = 1.5
HY_RADIX = 8
HY_RADIX_MIN_SEQ = 2048
HY_KC = 256
HY_TC_FFT = 256

V7X_VMEM_LIMIT_BYTES = 56 * 1024 * 1024
LANES = 128

KEY_W = GLA_HEADS * GLA_DK
VAL_W = GLA_HEADS * GLA_DV
NA_W = NA_HEADS * NA_HD
OFF_GQ = 0
OFF_GK = OFF_GQ + KEY_W
OFF_GV = OFF_GK + KEY_W
OFF_GG = OFF_GV + VAL_W
OFF_NQ = OFF_GG + VAL_W
OFF_NK = OFF_NQ + NA_W
OFF_NV = OFF_NK + NA_W
OFF_GATES = OFF_NV + NA_W
OFF_HY = OFF_GATES + 3 * 2048


def _cparams(*sem):
    return pltpu.CompilerParams(dimension_semantics=sem, vmem_limit_bytes=V7X_VMEM_LIMIT_BYTES)


def _modulated_norm(x, g, scale, shift):
    ms = jnp.mean(x * x, axis=-1, keepdims=True)
    y = x * lax.rsqrt(ms + NORM_EPS) * g
    return y * (1.0 + scale) + shift


def _adaln_kernel(c_ref, w_ref, b_ref, o_ref):
    c = c_ref[...]
    a = (c * jax.nn.sigmoid(c)).astype(BF16)
    o_ref[...] = jnp.dot(a, w_ref[...].astype(BF16), preferred_element_type=F32) + b_ref[...]


def adaln(cond, w_mod, b_mod, layer):
    m, d = cond.shape
    n = w_mod.shape[2]
    tn = 1024
    return pl.pallas_call(
        _adaln_kernel,
        grid=(n // tn,),
        in_specs=[pl.BlockSpec((m, d), lambda j: (0, 0)),
                  pl.BlockSpec((None, d, tn), lambda j: (layer, 0, j)),
                  pl.BlockSpec((1, tn), lambda j: (0, j))],
        out_specs=pl.BlockSpec((m, tn), lambda j: (0, j)),
        out_shape=jax.ShapeDtypeStruct((m, n), F32),
        compiler_params=_cparams("parallel"),
        name="adaln",
    )(cond, w_mod, b_mod)


def _log_decay(h, wga_ref, wa2_ref, ba2_ref):
    ga = jnp.dot(h, wga_ref[...], preferred_element_type=F32)
    pre = jnp.dot(ga.astype(BF16), wa2_ref[...], preferred_element_type=F32) + ba2_ref[...]
    log_sig = jnp.minimum(pre, 0.0) - jnp.log(1.0 + jnp.exp(-jnp.abs(pre)))
    return log_sig * (1.0 / GLA_TAU)


def _norm_matmul_kernel(*refs, relu2, decay):
    x_ref, g_ref, sc_ref, sh_ref, w_ref = refs[:5]
    if decay:
        wga_ref, wa2_ref, ba2_ref, o_ref, la_ref, h_ref = refs[5:]
    else:
        o_ref, h_ref = refs[5:]

    @pl.when(pl.program_id(1) == 0)
    def _():
        h = _modulated_norm(x_ref[...], g_ref[...], sc_ref[0], sh_ref[0]).astype(BF16)
        h_ref[...] = h
        if decay:
            la_ref[...] = _log_decay(h, wga_ref, wa2_ref, ba2_ref)

    acc = jnp.dot(h_ref[...], w_ref[...], preferred_element_type=F32)
    if relu2:
        acc = jnp.square(jnp.maximum(acc, 0.0))
    o_ref[...] = acc.astype(o_ref.dtype)


def norm_matmul(x, g, scale, shift, w, *, rows_per_mod, tm, tn, relu2, out_dtype, decay_weights=None):
    t, d = x.shape
    n = w.shape[1]
    mod_map = lambda i, j: ((i * tm) // rows_per_mod, 0, 0)
    decay = decay_weights is not None
    in_specs = [pl.BlockSpec((tm, d), lambda i, j: (i, 0)),
                pl.BlockSpec((1, d), lambda i, j: (0, 0)),
                pl.BlockSpec((1, 1, d), mod_map),
                pl.BlockSpec((1, 1, d), mod_map),
                pl.BlockSpec((d, tn), lambda i, j: (0, j))]
    out_specs = pl.BlockSpec((tm, tn), lambda i, j: (i, j))
    out_shape = jax.ShapeDtypeStruct((t, n), out_dtype)
    args = [x, g, scale, shift, w]
    if decay:
        in_specs += [pl.BlockSpec(a.shape, lambda i, j: (0, 0)) for a in decay_weights]
        n_la = decay_weights[1].shape[1]
        out_specs = [out_specs, pl.BlockSpec((tm, n_la), lambda i, j: (i, 0))]
        out_shape = [out_shape, jax.ShapeDtypeStruct((t, n_la), F32)]
        args += list(decay_weights)
    return pl.pallas_call(
        functools.partial(_norm_matmul_kernel, relu2=relu2, decay=decay),
        grid=(t // tm, n // tn),
        in_specs=in_specs, out_specs=out_specs, out_shape=out_shape,
        scratch_shapes=[pltpu.VMEM((tm, d), BF16)],
        compiler_params=_cparams("parallel", "arbitrary"),
        name="norm_matmul_relu2" if relu2 else "norm_matmul",
    )(*args)


def _cumulative_log_decay(la, reverse):
    c = la.shape[0]
    row = lax.broadcasted_iota(jnp.int32, (c, c), 0)
    col = lax.broadcasted_iota(jnp.int32, (c, c), 1)
    tri = jnp.where((row <= col) if reverse else (row >= col), 1.0, 0.0).astype(BF16)
    hi = la.astype(BF16)
    rest = la - hi.astype(F32)
    mid = rest.astype(BF16)
    lo = (rest - mid.astype(F32)).astype(BF16)
    dot = lambda t: jnp.dot(tri, t, preferred_element_type=F32)
    return dot(hi) + dot(mid) + dot(lo)


def _gla_chunk_scores(q, k, v, b, *, reverse, exact):
    c = GLA_CHUNK
    n = GLA_SUB
    edge = b[0:1, :] if reverse else b[c - 1:c, :]
    qe = (q * jnp.exp(b)).astype(BF16)
    kd = (k * jnp.exp(edge - b)).astype(BF16)
    m = jnp.dot(v.T.astype(BF16), kd, preferred_element_type=F32)
    decay = jnp.exp(edge)

    lane = lax.broadcasted_iota(jnp.int32, (n, c), 1)
    sub_row = lax.broadcasted_iota(jnp.int32, (n, c), 0)
    blocks = []
    cap = 0.0 if exact else GLA_SAFE_EXP
    for i in range(c // n):
        lo, hi = i * n, (i + 1) * n
        q_i = q[lo:hi, :]
        b_i = b[lo:hi, :]
        has_border = (hi < c) if reverse else (lo > 0)
        if has_border:
            r = b[hi:hi + 1, :] if reverse else b[lo - 1:lo, :]
            lhs = (q_i * jnp.exp(b_i - r)).astype(BF16)
            rhs = (k * jnp.exp(jnp.minimum(r - b, cap))).astype(BF16)
        else:
            lhs = (q_i * jnp.exp(b_i)).astype(BF16)
            rhs = (k * jnp.exp(jnp.minimum(-b, cap))).astype(BF16)
        a_i = lax.dot_general(lhs, rhs, (((1,), (1,)), ((), ())), preferred_element_type=F32)
        if exact:
            a_diag = jnp.zeros((n, c), F32)
            for j in range(n):
                s = lo + j
                p = q_i * k[s:s + 1, :] * jnp.exp(b_i - b[s:s + 1, :])
                a_diag = jnp.where(lane == s, jnp.sum(p, axis=1, keepdims=True), a_diag)
            in_block = (lane >= lo) & (lane < hi)
            a_i = jnp.where(in_block, a_diag, a_i)
        causal = (lane >= sub_row + lo) if reverse else (lane <= sub_row + lo)
        blocks.append(jnp.where(causal, a_i, 0.0))
    a = jnp.concatenate(blocks, axis=0).astype(BF16)
    return qe, a, m, decay


def _gla_inter(qe, st):
    return lax.dot_general(qe, st.astype(BF16), (((1,), (1,)), ((), ())), preferred_element_type=F32)


def _gla_kernel(*refs, reverse, rope, has_s0, nchunk, heads):
    it = iter(refs)
    q_ref, k_ref, v_ref, la_ref = next(it), next(it), next(it), next(it)
    cos_ref = sin_ref = s0_ref = None
    if rope:
        cos_ref, sin_ref = next(it), next(it)
    if has_s0:
        s0_ref = next(it)
    o_ref, sfin_ref, st_ref = next(it), next(it), next(it)

    blk = pl.program_id(2)

    @pl.when(blk == 0)
    def _():
        for hh in range(heads):
            st_ref[hh] = s0_ref[0, 0, 0, hh].T if has_s0 else jnp.zeros(st_ref.shape[1:], F32)

    kcols = lambda hh: slice(hh * GLA_DK, (hh + 1) * GLA_DK)
    vcols = lambda hh: slice(hh * GLA_DV, (hh + 1) * GLA_DV)

    def load_qk(rows, hh):
        q = q_ref[rows, kcols(hh)] * (GLA_DK ** -0.5)
        k = k_ref[rows, kcols(hh)]
        if rope:
            cs, sn = cos_ref[rows, :], sin_ref[rows, :]
            q = q * cs + pltpu.roll(q, GLA_DK // 2, 1) * sn
            k = k * cs + pltpu.roll(k, GLA_DK // 2, 1) * sn
        return q, k

    def intra(a, rows, hh):
        return jnp.dot(a, v_ref[rows, vcols(hh)].astype(BF16), preferred_element_type=F32)

    def run_exact():
        def body(step, carry):
            ci = (nchunk - 1 - step) if reverse else step
            rows = pl.ds(pl.multiple_of(ci * GLA_CHUNK, GLA_CHUNK), GLA_CHUNK)
            for hh in range(heads):
                q, k = load_qk(rows, hh)
                b = _cumulative_log_decay(la_ref[rows, kcols(hh)], reverse)
                qe, a, m, decay = _gla_chunk_scores(q, k, v_ref[rows, vcols(hh)], b, reverse=reverse, exact=True)
                o_ref[rows, vcols(hh)] = _gla_inter(qe, st_ref[hh]) + intra(a, rows, hh)
                st_ref[hh] = st_ref[hh] * decay + m
            return carry

        lax.fori_loop(0, nchunk, body, 0)

    def run_fast():
        order = list(range(nchunk))[::-1] if reverse else list(range(nchunk))
        rows = [pl.ds(ci * GLA_CHUNK, GLA_CHUNK) for ci in range(nchunk)]
        units = [(hh, ci) for hh in range(heads) for ci in range(nchunk)]
        bs = {u: _cumulative_log_decay(la_ref[rows[u[1]], kcols(u[0])], reverse) for u in units}
        scores = {u: _gla_chunk_scores(*load_qk(rows[u[1]], u[0]), v_ref[rows[u[1]], vcols(u[0])], bs[u],
                                       reverse=reverse, exact=False) for u in units}
        o_intra = {u: intra(scores[u][1], rows[u[1]], u[0]) for u in units}
        states = {}
        for hh in range(heads):
            st = st_ref[hh]
            for ci in order:
                states[hh, ci] = st
                st = st * scores[hh, ci][3] + scores[hh, ci][2]
            st_ref[hh] = st
        for hh, ci in units:
            o_ref[rows[ci], vcols(hh)] = _gla_inter(scores[hh, ci][0], states[hh, ci]) + o_intra[hh, ci]

    steep = jnp.min(la_ref[...]) < -(GLA_SAFE_EXP / GLA_SUB)
    pl.when(steep)(run_exact)
    pl.when(jnp.logical_not(steep))(run_fast)

    @pl.when(blk == pl.num_programs(2) - 1)
    def _():
        for hh in range(heads):
            sfin_ref[0, hh] = st_ref[hh].T


def gla_direction(proj, la, rope_tabs, s0, *, batch, seq, reverse, layer, direction, rows_blk):
    t = proj.shape[0]
    nblk = seq // rows_blk
    nchunk = rows_blk // GLA_CHUNK
    rope = rope_tabs is not None
    has_s0 = s0 is not None

    def rb(b, c):
        cc = (nblk - 1 - c) if reverse else c
        return b * nblk + cc

    hp = GLA_HEADS_PER_STEP
    kw, vw = hp * GLA_DK, hp * GLA_DV
    assert OFF_GQ % kw == 0 and OFF_GK % kw == 0 and OFF_GV % vw == 0 and KEY_W % kw == 0
    kq, kk, kv = OFF_GQ // kw, OFF_GK // kw, OFF_GV // vw
    kla = direction * (KEY_W // kw)
    in_specs = [pl.BlockSpec((rows_blk, kw), lambda b, h, c: (rb(b, c), kq + h)),
                pl.BlockSpec((rows_blk, kw), lambda b, h, c: (rb(b, c), kk + h)),
                pl.BlockSpec((rows_blk, vw), lambda b, h, c: (rb(b, c), kv + h)),
                pl.BlockSpec((rows_blk, kw), lambda b, h, c: (rb(b, c), kla + h))]
    args = [proj, proj, proj, la]
    if rope:
        tab_map = lambda b, h, c: ((nblk - 1 - c) if reverse else c, 0)
        in_specs += [pl.BlockSpec((rows_blk, GLA_DK), tab_map)] * 2
        args += list(rope_tabs)
    if has_s0:
        in_specs.append(pl.BlockSpec((1, 1, 1, hp, GLA_DK, GLA_DV), lambda b, h, c: (b, layer, direction, h, 0, 0)))
        args.append(s0)
    o, sfin = pl.pallas_call(
        functools.partial(_gla_kernel, reverse=reverse, rope=rope, has_s0=has_s0, nchunk=nchunk, heads=hp),
        grid=(batch, GLA_HEADS // hp, nblk),
        in_specs=in_specs,
        out_specs=[pl.BlockSpec((rows_blk, vw), lambda b, h, c: (rb(b, c), h)),
                   pl.BlockSpec((1, hp, GLA_DK, GLA_DV), lambda b, h, c: (b, h, 0, 0))],
        out_shape=[jax.ShapeDtypeStruct((t, VAL_W), F32),
                   jax.ShapeDtypeStruct((batch, GLA_HEADS, GLA_DK, GLA_DV), F32)],
        scratch_shapes=[pltpu.VMEM((hp, GLA_DV, GLA_DK), F32)],
        compiler_params=_cparams("parallel", "parallel", "arbitrary"),
        name="gla_bwd" if reverse else "gla_fwd",
    )(*args)
    return o, sfin


def _head_norm(x, g):
    return x * lax.rsqrt(jnp.mean(x * x, axis=-1, keepdims=True) + NORM_EPS) * g


def _ctx_attn_kernel(q_ref, k_ref, v_ref, gq_ref, gk_ref, o_ref, kn_ref, vn_ref):
    scale = NA_HD ** -0.5
    for h in range(NA_HEADS):
        cols = slice(h * NA_HD, (h + 1) * NA_HD)
        qn = _head_norm(q_ref[:, cols], gq_ref[...]) * scale
        kn = _head_norm(k_ref[:, cols], gk_ref[...])
        v = v_ref[:, cols]
        kn_ref[0, h] = kn
        vn_ref[0, h] = v
        s = lax.dot_general(qn.astype(BF16), kn.astype(BF16), (((1,), (1,)), ((), ())),
                            preferred_element_type=F32)
        e = jnp.exp(s - jnp.max(s, axis=-1, keepdims=True))
        p = e / jnp.sum(e, axis=-1, keepdims=True)
        o_ref[:, cols] = jnp.dot(p.astype(BF16), v.astype(BF16), preferred_element_type=F32)


def ctx_attention(proj, gq, gk, *, batch, seq):
    t = proj.shape[0]
    cq, ck, cv = OFF_NQ // NA_W, OFF_NK // NA_W, OFF_NV // NA_W
    cache_shape = jax.ShapeDtypeStruct((batch, NA_HEADS, seq, NA_HD), F32)
    cache_spec = pl.BlockSpec((1, NA_HEADS, seq, NA_HD), lambda b: (b, 0, 0, 0))
    return pl.pallas_call(
        _ctx_attn_kernel,
        grid=(batch,),
        in_specs=[pl.BlockSpec((seq, NA_W), lambda b: (b, cq)),
                  pl.BlockSpec((seq, NA_W), lambda b: (b, ck)),
                  pl.BlockSpec((seq, NA_W), lambda b: (b, cv)),
                  pl.BlockSpec((1, NA_HD), lambda b: (0, 0)),
                  pl.BlockSpec((1, NA_HD), lambda b: (0, 0))],
        out_specs=[pl.BlockSpec((seq, NA_W), lambda b: (b, 0)), cache_spec, cache_spec],
        out_shape=[jax.ShapeDtypeStruct((t, NA_W), F32), cache_shape, cache_shape],
        compiler_params=_cparams("parallel"),
        name="ctx_attention",
    )(proj, proj, proj, gq, gk)


def _na_kernel(q_ref, k_ref, v_ref, kc_ref, vc_ref, gq_ref, gk_ref, tab_ref, o_ref, qn_s, kn_s, v_s, *, rows):
    scale = NA_HD ** -0.5
    qn_s[...] = (_head_norm(q_ref[...], gq_ref[...]) * scale).astype(BF16)
    kn_s[...] = _head_norm(k_ref[...], gk_ref[...]).astype(BF16)
    v_s[...] = v_ref[...].astype(BF16)
    kctx = kc_ref[0, 0, 0].astype(BF16)
    vctx = vc_ref[0, 0, 0].astype(BF16)
    nblk = rows // NA_QROWS
    qlen, win = NA_QROWS * GRID_W, NA_UNION * GRID_W

    def scores(g):
        u0 = jnp.clip(g * NA_QROWS - NA_KH // 2, 0, rows - NA_UNION)
        cls = jnp.where(g == 0, 0, jnp.where(g == nblk - 1, 2, 1))
        qrows = pl.ds(pl.multiple_of(g * qlen, qlen), qlen)
        wrows = pl.ds(pl.multiple_of(u0 * GRID_W, GRID_W), win)
        qn = qn_s[qrows, :]
        s_loc = lax.dot_general(qn, kn_s[wrows, :], (((1,), (1,)), ((), ())), preferred_element_type=F32)
        s_ctx = lax.dot_general(qn, kctx, (((1,), (1,)), ((), ())), preferred_element_type=F32)
        return qrows, wrows, s_loc + tab_ref[0, cls], s_ctx

    def softmax(s_loc, s_ctx):
        m = jnp.maximum(jnp.max(s_loc, axis=-1, keepdims=True), jnp.max(s_ctx, axis=-1, keepdims=True))
        e_loc = jnp.exp(s_loc - m)
        e_ctx = jnp.exp(s_ctx - m)
        den = jnp.sum(e_loc, axis=-1, keepdims=True) + jnp.sum(e_ctx, axis=-1, keepdims=True)
        return e_loc.astype(BF16), e_ctx.astype(BF16), den

    def body(step, carry):
        blocks = [scores(step * NA_BLOCK_UNROLL + u) for u in range(NA_BLOCK_UNROLL)]
        probs = [softmax(s_loc, s_ctx) for _, _, s_loc, s_ctx in blocks]
        for (qrows, wrows, _, _), (e_loc, e_ctx, den) in zip(blocks, probs):
            acc = jnp.dot(e_loc, v_s[wrows, :], preferred_element_type=F32)
            acc = acc + jnp.dot(e_ctx, vctx, preferred_element_type=F32)
            o_ref[qrows, :] = acc / den
        return carry

    lax.fori_loop(0, nblk // NA_BLOCK_UNROLL, body, 0)


def neighbourhood_attention(proj, cache_k, cache_v, gq, gk, table, *, batch, seq, layer, row0):
    rows = seq // GRID_W
    assert rows % NA_QROWS == 0 and rows // NA_QROWS >= 3 and rows >= NA_UNION
    past = cache_k.shape[3]
    cq, ck, cv = OFF_NQ // NA_HD, OFF_NK // NA_HD, OFF_NV // NA_HD
    cache_spec = pl.BlockSpec((1, 1, 1, past, NA_HD), lambda b, h: (b, layer, h, 0, 0))
    return pl.pallas_call(
        functools.partial(_na_kernel, rows=rows),
        grid=(batch, NA_HEADS),
        in_specs=[pl.BlockSpec((seq, NA_HD), lambda b, h: (row0 + b, cq + h)),
                  pl.BlockSpec((seq, NA_HD), lambda b, h: (row0 + b, ck + h)),
                  pl.BlockSpec((seq, NA_HD), lambda b, h: (row0 + b, cv + h)),
                  cache_spec, cache_spec,
                  pl.BlockSpec((1, NA_HD), lambda b, h: (0, 0)),
                  pl.BlockSpec((1, NA_HD), lambda b, h: (0, 0)),
                  pl.BlockSpec((1, 3, NA_QROWS * GRID_W, NA_UNION * GRID_W), lambda b, h: (h, 0, 0, 0))],
        out_specs=pl.BlockSpec((seq, NA_HD), lambda b, h: (b, h)),
        out_shape=jax.ShapeDtypeStruct((batch * seq, NA_W), F32),
        scratch_shapes=[pltpu.VMEM((seq, NA_HD), BF16)] * 3,
        compiler_params=_cparams("parallel", "parallel"),
        name="neighbourhood_attention",
    )(proj, proj, proj, cache_k, cache_v, gq, gk, table)


def na_bias_table(rpb, rows):
    nblk = rows // NA_QROWS
    col = np.arange(GRID_W)
    c_start = np.clip(col - NA_KW // 2, 0, GRID_W - NA_KW)
    col_in = (col[None, :] >= c_start[:, None]) & (col[None, :] < c_start[:, None] + NA_KW)
    col_idx = np.clip(col[None, :] - col[:, None] + NA_KW - 1, 0, 2 * NA_KW - 2)
    row_off, row_in = [], []
    for g in (0, 1, nblk - 1):
        u0 = np.clip(g * NA_QROWS - NA_KH // 2, 0, rows - NA_UNION)
        r = g * NA_QROWS + np.arange(NA_QROWS)[:, None]
        r_start = np.clip(r - NA_KH // 2, 0, rows - NA_KH)
        key_row = u0 + np.arange(NA_UNION)[None, :]
        row_in.append((key_row >= r_start) & (key_row < r_start + NA_KH))
        row_off.append(np.clip(key_row - r + NA_KH - 1, 0, 2 * NA_KH - 2))
    row_off, row_in = np.stack(row_off), np.stack(row_in)
    n_off = 2 * NA_KH - 1
    tiles = jnp.where(col_in[None, None], rpb[:, :, col_idx], -jnp.inf)
    tiles = jnp.concatenate([tiles, jnp.full_like(tiles[:, :1], -jnp.inf)], axis=1)
    tab = tiles[:, np.where(row_in, row_off, n_off)]
    tab = jnp.transpose(tab, (0, 1, 2, 4, 3, 5))
    return tab.reshape(rpb.shape[0], 3, NA_QROWS * GRID_W, NA_UNION * GRID_W).astype(F32)


def _filter_kernel(z_ref, f1w_ref, f1b_ref, f2w_ref, f2b_ref, f3w_ref, fr_ref, dl_ref, o_ref, *, seq):
    fr = fr_ref[...]
    h = jnp.sin(fr * (jnp.dot(z_ref[...], f1w_ref[...], preferred_element_type=F32, precision=HIGHEST)
                      + f1b_ref[...]))
    h = jnp.sin(fr * (jnp.dot(h, f2w_ref[...], preferred_element_type=F32, precision=HIGHEST)
                      + f2b_ref[...]))
    h = jnp.dot(h, f3w_ref[...], preferred_element_type=F32, precision=HIGHEST)
    t = lax.broadcasted_iota(jnp.int32, (seq, 1), 0).astype(F32)
    dist = jnp.abs(t - float(seq // 2)) * (2.0 / seq)
    h = h * jnp.exp(-dist * dl_ref[...])
    o_ref[...] = h * lax.rsqrt(jnp.sum(h * h, axis=0, keepdims=True) + NORM_EPS)


def hyena_filters(zpos, f1w, f1b, f2w, f2b, f3w, freq, deltas, *, seq):
    n = f3w.shape[1]
    tc = 512
    full = lambda a: pl.BlockSpec(a.shape, lambda j: (0,) * a.ndim)
    return pl.pallas_call(
        functools.partial(_filter_kernel, seq=seq),
        grid=(n // tc,),
        in_specs=[full(zpos), full(f1w), full(f1b), full(f2w), full(f2b),
                  pl.BlockSpec((HY_HIDDEN, tc), lambda j: (0, j)), full(freq),
                  pl.BlockSpec((1, tc), lambda j: (0, j))],
        out_specs=pl.BlockSpec((seq, tc), lambda j: (0, j)),
        out_shape=jax.ShapeDtypeStruct((seq, n), F32),
        compiler_params=_cparams("parallel"),
        name="hyena_filters",
    )(zpos, f1w, f1b, f2w, f2b, f3w, freq, deltas)


def _lane_tile(t, width):
    return jnp.concatenate([t] * (width // LANES), axis=1) if width > LANES else t


def _cdft(vals, sign):
    if len(vals) == 1:
        return vals
    if len(vals) == 2:
        (ar, ai), (br, bi) = vals
        return [(ar + br, ai + bi), (ar - br, ai - bi)]
    if len(vals) == 8:
        ev, od = _cdft(vals[0::2], sign), _cdft(vals[1::2], sign)
        h = math.sqrt(0.5)
        (o0r, o0i), (o1r, o1i), (o2r, o2i), (o3r, o3i) = od
        tw = [(o0r, o0i),
              ((o1r - sign * o1i) * h, (o1i + sign * o1r) * h),
              (-sign * o2i, sign * o2r),
              ((-o3r - sign * o3i) * h, (sign * o3r - o3i) * h)]
        return ([(er + tr, ei + ti) for (er, ei), (tr, ti) in zip(ev, tw)]
                + [(er - tr, ei - ti) for (er, ei), (tr, ti) in zip(ev, tw)])
    assert len(vals) == 4
    (x0r, x0i), (x1r, x1i), (x2r, x2i), (x3r, x3i) = vals
    a0r, a0i, a1r, a1i = x0r + x2r, x0i + x2i, x0r - x2r, x0i - x2i
    a2r, a2i, a3r, a3i = x1r + x3r, x1i + x3i, x1r - x3r, x1i - x3i
    wr, wi = -sign * a3i, sign * a3r
    return [(a0r + a2r, a0i + a2i), (a1r + wr, a1i + wi), (a0r - a2r, a0i - a2i), (a1r - wr, a1i - wi)]


def _fwd_spectrum(z_list, f_ref, tw_ref, rows, width):
    lp = f_ref.shape[1]
    f_cos = f_ref[rows, :]
    f_sin = f_ref[pl.ds(lp + rows.start, rows.size), :]
    xs = []
    for r, z in enumerate(z_list):
        xc = jnp.dot(f_cos, z, preferred_element_type=F32)
        xn = jnp.dot(f_sin, z, preferred_element_type=F32)
        if r == 0:
            xs.append((xc, -xn))
        else:
            c = _lane_tile(tw_ref[r, 0, rows, :], width)
            s = _lane_tile(tw_ref[r, 1, rows, :], width)
            xs.append((xc * c - xn * s, -(xn * c + xc * s)))
    return _cdft(xs, -1)


def _spectrum_kernel(*refs, radix, seq, kc, tc):
    h_refs, (f_ref, tw_ref, sh_ref, o_ref) = refs[:radix], refs[radix:]
    lp = seq // radix
    hs = [h[...].astype(BF16) for h in h_refs]
    for c0 in range(0, lp, kc):
        rows = pl.ds(c0, kc)
        cs = _lane_tile(sh_ref[0, rows, :], tc) * (1.0 / seq)
        sn = _lane_tile(sh_ref[1, rows, :], tc) * (1.0 / seq)
        for q, (re, im) in enumerate(_fwd_spectrum(hs, f_ref, tw_ref, rows, tc)):
            o_ref[0, q, rows, :] = re * cs - im * sn
            o_ref[1, q, rows, :] = re * sn + im * cs


def _whole(a):
    return pl.BlockSpec(a.shape, lambda *_: (0,) * a.ndim, pipeline_mode=pl.Buffered(1))


def filter_spectrum(filt, tabs, *, seq):
    n = filt.shape[1]
    radix, tc = _hy_radix(seq), _hy_fft_tc(seq)
    lp = seq // radix
    view = filt.reshape(lp, radix * n)
    f, _, tw, sh = tabs
    groups = [pl.BlockSpec((lp, tc), functools.partial(lambda j, r: (0, r * (n // tc) + j), r=r)) for r in range(radix)]
    return pl.pallas_call(
        functools.partial(_spectrum_kernel, radix=radix, seq=seq, kc=min(HY_KC, lp), tc=tc),
        grid=(n // tc,),
        in_specs=groups + [_whole(f), _whole(tw), _whole(sh)],
        out_specs=pl.BlockSpec((2, radix, lp, tc), lambda j: (0, 0, 0, j)),
        out_shape=jax.ShapeDtypeStruct((2, radix, lp, n), F32),
        compiler_params=_cparams("parallel"),
        name="filter_spectrum",
    )(*([view] * radix), f, tw, sh)


def _short_conv(u, w, bias, seq):
    t = lax.broadcasted_iota(jnp.int32, (seq, 1), 0)
    prev = jnp.where(t == 0, 0.0, pltpu.roll(u, 1, 0))
    nxt = jnp.where(t == seq - 1, 0.0, pltpu.roll(u, seq - 1, 0))
    return bias + prev * w[0:1, :] + u * w[1:2, :] + nxt * w[2:3, :]


def _hy_radix(seq):
    return HY_RADIX if seq >= HY_RADIX_MIN_SEQ else 1


def _split_rows(val, out_refs, tmp_ref):
    radix = len(out_refs)
    if radix == 1:
        out_refs[0][...] = val.astype(BF16)
        return
    tmp_ref[...] = val
    for r in range(radix):
        out_refs[r][...] = tmp_ref[pl.ds(r, val.shape[0] // radix, stride=radix), :].astype(BF16)


def _merge_rows(in_refs, tmp_ref):
    radix = len(in_refs)
    if radix == 1:
        return in_refs[0][...]
    for r in range(radix):
        tmp_ref[pl.ds(r, in_refs[r].shape[0], stride=radix), :] = in_refs[r][...]
    return tmp_ref[...]


def _hy_pre_kernel(*refs, seq, radix):
    (u_ref, w_ref, b_ref), z_refs, tmp = refs[:3], refs[3:3 + radix], refs[3 + radix:]
    z = _short_conv(u_ref[...], w_ref[...], b_ref[...], seq)
    _split_rows(z, z_refs, tmp[0] if tmp else None)


def _hy_gate_kernel(*refs, seq, radix):
    ux_ref, uz_ref, wx_ref, bx_ref, wz_ref, bz_ref = refs[:6]
    cv_refs, hb_ref, zo_ref = refs[6:6 + radix], refs[6 + radix], refs[7 + radix]
    zb_refs, tmp = refs[8 + radix:8 + 2 * radix], refs[8 + 2 * radix:]
    tmp_ref = tmp[0] if tmp else None
    x = _short_conv(ux_ref[...], wx_ref[...], bx_ref[...], seq)
    z = _short_conv(uz_ref[...], wz_ref[...], bz_ref[...], seq)
    out = x * (_merge_rows(cv_refs, tmp_ref) + hb_ref[...] * z)
    zo_ref[...] = out
    _split_rows(out, zb_refs, tmp_ref)


def _hy_final_kernel(*refs, seq, radix):
    (ux_ref, wx_ref, bx_ref), cv_refs = refs[:3], refs[3:3 + radix]
    hb_ref, z_ref, zb_ref = refs[3 + radix:6 + radix]
    tmp = refs[6 + radix:]
    x = _short_conv(ux_ref[...], wx_ref[...], bx_ref[...], seq)
    cv = _merge_rows(cv_refs, tmp[0] if tmp else None)
    zb_ref[...] = (x * (cv + hb_ref[...] * z_ref[...])).astype(BF16)


HY_ELEM_SLAB_BYTES = 2 * 1024 * 1024


def _hy_tc(seq):
    if _hy_radix(seq) > 1:
        return LANES
    return max(LANES, min(HY_W, HY_ELEM_SLAB_BYTES // (4 * seq)))


def _hy_specs(seq, row0, col0):
    tc = _hy_tc(seq)
    cb, wb = col0 // tc, (col0 - OFF_HY) // tc
    u = pl.BlockSpec((seq, tc), lambda b, j: (row0 + b, cb + j))
    w = pl.BlockSpec((3, tc), lambda b, j: (0, wb + j))
    bb = pl.BlockSpec((1, tc), lambda b, j: (0, wb + j))
    act = pl.BlockSpec((seq, tc), lambda b, j: (b, j))
    hb = pl.BlockSpec((1, tc), lambda b, j: (0, j))
    part = pl.BlockSpec((seq // _hy_radix(seq), tc), lambda b, j: (b, j))
    return u, w, bb, act, hb, part


def _hy_elementwise_call(kernel, name, seq, batch, in_specs, out_specs, out_shape, args):
    radix, tc = _hy_radix(seq), _hy_tc(seq)
    assert radix == 1 or tc == LANES
    return pl.pallas_call(
        functools.partial(kernel, seq=seq, radix=radix),
        grid=(batch, HY_W // tc),
        in_specs=in_specs, out_specs=out_specs, out_shape=out_shape,
        scratch_shapes=[pltpu.VMEM((seq, tc), F32)] if radix > 1 else [],
        compiler_params=_cparams("parallel", "parallel"), name=name,
    )(*args)


def hyena_pre(proj, conv_w, conv_b, *, batch, seq, row0):
    radix = _hy_radix(seq)
    u, w, bb, _, _, part = _hy_specs(seq, row0, OFF_HY + HY_ORDER * HY_W)
    parts = [jax.ShapeDtypeStruct((batch * seq // radix, HY_W), BF16)] * radix
    return _hy_elementwise_call(_hy_pre_kernel, "hyena_pre", seq, batch, [u, w, bb], [part] * radix, parts,
                                (proj, conv_w, conv_b))


def hyena_gate(proj, conv_w, conv_b, conv, hy_bias, *, batch, seq, row0):
    radix = _hy_radix(seq)
    ux, wx, bx, act, hb, part = _hy_specs(seq, row0, OFF_HY)
    uz, wz, bz, _, _, _ = _hy_specs(seq, row0, OFF_HY + HY_ORDER * HY_W)
    parts = [jax.ShapeDtypeStruct((batch * seq // radix, HY_W), BF16)] * radix
    outs = _hy_elementwise_call(
        _hy_gate_kernel, "hyena_gate", seq, batch,
        [ux, uz, wx, bx, wz, bz] + [part] * radix + [hb], [act] + [part] * radix,
        [jax.ShapeDtypeStruct((batch * seq, HY_W), F32)] + parts,
        (proj, proj, conv_w, conv_b, conv_w, conv_b, *conv, hy_bias[0:1]))
    return outs[0], outs[1:]


def hyena_final(proj, conv_w, conv_b, conv, hy_bias, z1, *, batch, seq, row0):
    radix = _hy_radix(seq)
    ux, wx, bx, act, hb, part = _hy_specs(seq, row0, OFF_HY + HY_W)
    return _hy_elementwise_call(
        _hy_final_kernel, "hyena_final", seq, batch, [ux, wx, bx] + [part] * radix + [hb, act], act,
        jax.ShapeDtypeStruct((batch * seq, HY_W), BF16), (proj, conv_w, conv_b, *conv, hy_bias[1:2], z1))


def _long_conv_kernel(*refs, radix, seq, kc, tc):
    z_refs, (f_ref, ft_ref, tw_ref, g_ref) = refs[:radix], refs[radix:radix + 4]
    o_refs, u_s = refs[radix + 4:2 * radix + 4], refs[2 * radix + 4]
    lp = seq // radix
    r_out = pl.program_id(2)

    @pl.when(r_out == 0)
    def _():
        zs = [z[...] for z in z_refs]
        for c0 in range(0, lp, kc):
            rows = pl.ds(c0, kc)
            ys = []
            for q, (zr, zi) in enumerate(_fwd_spectrum(zs, f_ref, tw_ref, rows, tc)):
                gr, gi = g_ref[0, q, rows, :], g_ref[1, q, rows, :]
                ys.append((zr * gr - zi * gi, zr * gi + zi * gr))
            for r, (re, im) in enumerate(_cdft(ys, +1)):
                if r > 0:
                    c = _lane_tile(tw_ref[r, 0, rows, :], tc)
                    s = _lane_tile(tw_ref[r, 1, rows, :], tc)
                    re, im = re * c - im * s, re * s + im * c
                u_s[r, rows, :] = re.astype(BF16)
                u_s[r, pl.ds(lp + c0, kc), :] = (-im).astype(BF16)

    for r in range(radix):
        @pl.when(r_out == r)
        def _(r=r):
            o_refs[r][...] = jnp.dot(ft_ref[...], u_s[r], preferred_element_type=F32)


def _hy_fft_tc(seq):
    return HY_TC_FFT if _hy_radix(seq) > 1 else 2 * HY_TC_FFT


def hyena_long_conv(z_parts, g, tabs, *, batch, seq, order):
    c = z_parts[0].shape[1]
    radix, tc = _hy_radix(seq), _hy_fft_tc(seq)
    lp = seq // radix
    f, ft, tw, _ = tabs
    ncb = c // tc
    goff = order * ncb
    part = pl.BlockSpec((lp, tc), lambda j, b, ro: (b, j))
    return pl.pallas_call(
        functools.partial(_long_conv_kernel, radix=radix, seq=seq, kc=min(HY_KC, lp), tc=tc),
        grid=(ncb, batch, radix),
        in_specs=[part] * radix + [_whole(f), _whole(ft), _whole(tw),
                                   pl.BlockSpec((2, radix, lp, tc), lambda j, b, ro: (0, 0, 0, goff + j),
                                                pipeline_mode=pl.Buffered(1))],
        out_specs=[part] * radix,
        out_shape=[jax.ShapeDtypeStruct((batch * lp, c), F32)] * radix,
        scratch_shapes=[pltpu.VMEM((radix, 2 * lp, tc), BF16)],
        compiler_params=_cparams("parallel", "parallel", "arbitrary"),
        name="hyena_long_conv",
    )(*z_parts, f, ft, tw, g)


def dft_tables(seq):
    radix = _hy_radix(seq)
    lp = seq // radix
    odd =2 * jnp.arange(lp, dtype=jnp.int32) + 1

    def cs(num, den):
        ang = (num % (2 * den)).astype(F32) * (math.pi / den)
        return jnp.cos(ang), jnp.sin(ang)

    fc, fs = cs(odd[:, None] * jnp.arange(lp, dtype=jnp.int32)[None, :], 2 * lp)
    f = jnp.concatenate([fc, fs], axis=0).astype(BF16)
    tc_, ts_ = cs(odd[None, :] * jnp.arange(radix, dtype=jnp.int32)[:, None], 2 * seq)
    tw = jnp.broadcast_to(jnp.stack([tc_, ts_], axis=1)[..., None], (radix, 2, lp, LANES))
    sc, ss = cs(odd, 4)
    sh = jnp.broadcast_to(jnp.stack([sc, ss], axis=0)[..., None], (2, lp, LANES))
    return f, f.T, tw, sh


def position_features(seq):
    t = jnp.arange(seq, dtype=F32)
    tn = t / seq
    bands = jnp.arange(1, HY_BANDS + 1, dtype=F32)
    ang = 2.0 * jnp.pi * tn[:, None] * bands[None, :]
    z = jnp.concatenate([tn[:, None], jnp.cos(ang), jnp.sin(ang)], axis=-1)
    return jnp.pad(z, ((0, 0), (0, LANES - z.shape[1])))


def rope_tables(seq):
    n_freq = GLA_DK // 4
    pos = jnp.arange(seq)
    row = (pos // GRID_W).astype(F32)
    col = (pos % GRID_W).astype(F32)
    inv = ROPE_BASE ** (-jnp.arange(n_freq, dtype=F32) / n_freq)
    ang = jnp.concatenate([row[:, None] * inv, col[:, None] * inv], axis=-1)
    cos, sin = jnp.cos(ang), jnp.sin(ang)
    return jnp.concatenate([cos, cos], axis=-1), jnp.concatenate([-sin, sin], axis=-1)


def _merge_kernel(of_ref, ob_ref, gg_ref, gn_ref, nb_ref, hc_ref, ga_ref, gb_ref, gc_ref,
                  wa_ref, wb_ref, wc_ref, o_ref, a_s):
    @pl.when(pl.program_id(1) == 0)
    def _():
        for h in range(GLA_HEADS):
            cols = slice(h * GLA_DV, (h + 1) * GLA_DV)
            oa = of_ref[:, cols] + ob_ref[:, cols]
            oa = _head_norm(oa, gn_ref[...])
            gg = gg_ref[:, cols]
            a_s[:, cols] = (oa * (gg * jax.nn.sigmoid(gg))).astype(BF16)

    ya = jnp.dot(a_s[...], wa_ref[...], preferred_element_type=F32)
    yb = jnp.dot(nb_ref[...].astype(BF16), wb_ref[...], preferred_element_type=F32)
    yc = jnp.dot(hc_ref[...], wc_ref[...], preferred_element_type=F32)
    merged = (jax.nn.sigmoid(ga_ref[...]) * ya + jax.nn.sigmoid(gb_ref[...]) * yb
              + jax.nn.sigmoid(gc_ref[...]) * yc)
    o_ref[...] = merged.astype(o_ref.dtype)


def branch_merge(o_f, o_b, proj, gnorm, o_attn, o_hy, w_a, w_b, w_c, *, row0_blocks, tm, tn):
    t = o_f.shape[0]
    d = w_a.shape[1]
    assert OFF_GATES % tn == 0 and d % tn == 0
    act = lambda w: pl.BlockSpec((tm, w), lambda i, j: (i, 0))
    gate = lambda g: pl.BlockSpec((tm, tn), lambda i, j: (row0_blocks + i, (OFF_GATES + g * d) // tn + j))
    wspec = lambda w: pl.BlockSpec((w.shape[0], tn), lambda i, j: (0, j))
    return pl.pallas_call(
        _merge_kernel,
        grid=(t // tm, d // tn),
        in_specs=[act(VAL_W), act(VAL_W),
                  pl.BlockSpec((tm, VAL_W), lambda i, j: (row0_blocks + i, OFF_GG // VAL_W)),
                  pl.BlockSpec((1, GLA_DV), lambda i, j: (0, 0)),
                  act(NA_W), act(HY_W), gate(0), gate(1), gate(2),
                  wspec(w_a), wspec(w_b), wspec(w_c)],
        out_specs=pl.BlockSpec((tm, tn), lambda i, j: (i, j)),
        out_shape=jax.ShapeDtypeStruct((t, d), BF16),
        scratch_shapes=[pltpu.VMEM((tm, VAL_W), BF16)],
        compiler_params=_cparams("parallel", "arbitrary"),
        name="branch_merge",
    )(o_f, o_b, proj, gnorm, o_attn, o_hy, proj, proj, proj, w_a, w_b, w_c)


def _matmul_residual_kernel(a_ref, w_ref, g_ref, r_ref, o_ref):
    y = jnp.dot(a_ref[...], w_ref[...], preferred_element_type=F32)
    o_ref[...] = r_ref[...] + g_ref[0] * y


def matmul_residual(a, w, gate, resid, *, rows_per_mod, tm, tn):
    t, k = a.shape
    n = w.shape[1]
    return pl.pallas_call(
        _matmul_residual_kernel,
        grid=(t // tm, n // tn),
        in_specs=[pl.BlockSpec((tm, k), lambda i, j: (i, 0)),
                  pl.BlockSpec((k, tn), lambda i, j: (0, j)),
                  pl.BlockSpec((1, 1, tn), lambda i, j: ((i * tm) // rows_per_mod, 0, j)),
                  pl.BlockSpec((tm, tn), lambda i, j: (i, j))],
        out_specs=pl.BlockSpec((tm, tn), lambda i, j: (i, j)),
        out_shape=jax.ShapeDtypeStruct((t, n), F32),
        compiler_params=_cparams("parallel", "arbitrary"),
        name="matmul_residual",
    )(a, w, gate, resid)


def _trunk_layer(x, mods, p, consts, *, batch, seq, layer, latent):
    sh1, sc1, g1, sh2, sc2, g2 = mods
    t = x.shape[0]
    rows_per_mod = t // sh1.shape[0]
    tm = 512

    proj, la = norm_matmul(x, p["norm1_g"], sc1, sh1, p["w_in_main"], rows_per_mod=rows_per_mod,
                           tm=tm, tn=1536, relu2=False, out_dtype=F32,
                           decay_weights=(p["w_in_ga"], p["wa2_bd"], p["ba2"]))

    rows_blk = min(seq, 512)
    rope = consts["rope"] if latent else None
    s0 = consts["state_gla"] if latent else None
    o_f, s_f = gla_direction(proj, la, rope, s0, batch=batch, seq=seq, reverse=False, layer=layer,
                             direction=0, rows_blk=rows_blk)
    o_b, s_b = gla_direction(proj, la, rope, s0, batch=batch, seq=seq, reverse=True, layer=layer,
                             direction=1, rows_blk=rows_blk)

    if latent:
        o_attn = neighbourhood_attention(proj, consts["cache_k"], consts["cache_v"], p["na_qnorm_g"],
                                         p["na_knorm_g"], p["na_table"], batch=batch, seq=seq,
                                         layer=layer, row0=0)
        new_ctx = None
    else:
        o_attn, kn, vn = ctx_attention(proj, p["na_qnorm_g"], p["na_knorm_g"], batch=batch, seq=seq)
        new_ctx = (jnp.stack([s_f, s_b], axis=1), kn, vn)

    key = "lat" if latent else "ctx"
    tabs = consts["dft_" + key]
    g_spec = p["spectrum_" + key]
    z0 = hyena_pre(proj, p["hy_conv_w"], p["hy_conv_b"], batch=batch, seq=seq, row0=0)
    cv0 = hyena_long_conv(z0, g_spec, tabs, batch=batch, seq=seq, order=0)
    z1, z1b = hyena_gate(proj, p["hy_conv_w"], p["hy_conv_b"], cv0, p["hy_bias"], batch=batch, seq=seq, row0=0)
    cv1 = hyena_long_conv(z1b, g_spec, tabs, batch=batch, seq=seq, order=1)
    o_hy = hyena_final(proj, p["hy_conv_w"], p["hy_conv_b"], cv1, p["hy_bias"], z1, batch=batch, seq=seq, row0=0)

    merged = branch_merge(o_f, o_b, proj, p["gla_norm_g"], o_attn, o_hy, p["w_br_a"], p["w_br_b"],
                          p["w_br_c"], row0_blocks=0, tm=256, tn=2048)
    x = matmul_residual(merged, p["w_out"], g1, x, rows_per_mod=rows_per_mod, tm=tm, tn=2048)

    u = norm_matmul(x, p["norm2_g"], sc2, sh2, p["w_mlp1"], rows_per_mod=rows_per_mod,
                    tm=tm, tn=2048, relu2=True, out_dtype=BF16)
    x = matmul_residual(u, p["w_mlp2"], g2, x, rows_per_mod=rows_per_mod, tm=tm, tn=512)
    return x, new_ctx


def _reorder_w_in(w_in):
    o_ga = 2 * KEY_W + 2 * VAL_W
    n_ga = 2 * GLA_LOWRANK
    o_hy = o_ga + n_ga + 3 * NA_W
    o_gates = o_hy + (HY_ORDER + 1) * HY_W
    main = jnp.concatenate([w_in[..., :o_ga], w_in[..., o_ga + n_ga:o_hy], w_in[..., o_gates:],
                            w_in[..., o_hy:o_gates]], axis=-1).astype(BF16)
    ga = jnp.pad(w_in[..., o_ga:o_ga + n_ga], ((0, 0), (0, LANES - n_ga))).astype(BF16)
    return main, ga


def kernel(x_prompt, x_sample, state_gla, cache_na_k, cache_na_v, c, c_ctx, w_mod, b_mod, norm1_g, norm2_g, w_in, gla_wa2, gla_ba2, gla_norm_g, na_qnorm_g, na_knorm_g, na_rpb, hy_conv_w, hy_conv_b, hy_f1_w, hy_f1_b, hy_f2_w, hy_f2_b, hy_f3_w, hy_freq, hy_bias, w_br_a, w_br_b, w_br_c, w_out, w_mlp1, w_mlp2):
    batch, seq, d = x_prompt.shape
    dec_batch, dec_seq, _ = x_sample.shape
    depth = w_mod.shape[0]

    bf = lambda w: w.astype(BF16)
    wa2_bd = jnp.zeros((depth, LANES, 2 * KEY_W), F32)
    wa2_bd = wa2_bd.at[:, :GLA_LOWRANK, :KEY_W].set(gla_wa2[:, 0])
    wa2_bd = wa2_bd.at[:, GLA_LOWRANK:2 * GLA_LOWRANK, KEY_W:].set(gla_wa2[:, 1])
    ba2 = gla_ba2.reshape(depth, 1, 2 * KEY_W)
    f1w = jnp.pad(hy_f1_w, ((0, 0), (0, LANES - hy_f1_w.shape[1]), (0, 0)))

    consts = {
        "rope": rope_tables(dec_seq),
        "state_gla": state_gla, "cache_k": cache_na_k, "cache_v": cache_na_v,
        "dft_ctx": dft_tables(seq), "dft_lat": dft_tables(dec_seq),
    }
    deltas = jnp.abs(jnp.linspace(math.log(HY_DECAY_TARGET) / HY_DECAY_PCT_LONG,
                                  math.log(HY_DECAY_TARGET) / HY_DECAY_PCT_SHORT, HY_W, dtype=F32))
    deltas = jnp.tile(deltas, HY_ORDER)[None, :]
    zpos = {"ctx": position_features(seq), "lat": position_features(dec_seq)}

    cond = jnp.zeros((16, d), F32).at[0].set(c_ctx).at[1:1 + dec_batch].set(c)

    xp = x_prompt.reshape(batch * seq, d)
    xs = x_sample.reshape(dec_batch * dec_seq, d)
    new_gla, new_k, new_v = [], [], []
    for l in range(depth):
        m = adaln(cond, w_mod, b_mod[l][None, :], l)
        mods_ctx = tuple(m[0:1, i * d:(i + 1) * d][:, None, :] for i in range(N_MOD))
        mods_lat = tuple(m[1:1 + dec_batch, i * d:(i + 1) * d][:, None, :] for i in range(N_MOD))
        w_in_main, w_in_ga = _reorder_w_in(w_in[l])
        p = dict(norm1_g=norm1_g[l][None], norm2_g=norm2_g[l][None], w_in_main=w_in_main, w_in_ga=w_in_ga,
                 wa2_bd=bf(wa2_bd[l]), ba2=ba2[l], gla_norm_g=gla_norm_g[l][None], na_qnorm_g=na_qnorm_g[l][None],
                 na_knorm_g=na_knorm_g[l][None], na_table=na_bias_table(na_rpb[l], dec_seq // GRID_W),
                 hy_conv_w=hy_conv_w[l], hy_conv_b=hy_conv_b[l][None], hy_bias=hy_bias[l],
                 w_br_a=bf(w_br_a[l]), w_br_b=bf(w_br_b[l]), w_br_c=bf(w_br_c[l]), w_out=bf(w_out[l]),
                 w_mlp1=bf(w_mlp1[l]), w_mlp2=bf(w_mlp2[l]))
        for key, length in (("ctx", seq), ("lat", dec_seq)):
            filt = hyena_filters(zpos[key], f1w[l], hy_f1_b[l][None], hy_f2_w[l], hy_f2_b[l][None],
                                 hy_f3_w[l], hy_freq[l][None], deltas, seq=length)
            p["spectrum_" + key] = filter_spectrum(filt, consts["dft_" + key], seq=length)
        xp, (s_gla, k_ctx, v_ctx) = _trunk_layer(xp, mods_ctx, p, consts, batch=batch, seq=seq, layer=l, latent=False)
        new_gla.append(s_gla)
        new_k.append(k_ctx)
        new_v.append(v_ctx)
        xs, _ = _trunk_layer(xs, mods_lat, p, consts, batch=dec_batch, seq=dec_seq, layer=l, latent=True)
    return (xp.reshape(batch, seq, d), xs.reshape(dec_batch, dec_seq, d),
            jnp.stack(new_gla, axis=1), jnp.stack(new_k, axis=1), jnp.stack(new_v, axis=1))
```

```python
import functools
import math

import jax
import jax.numpy as jnp
import numpy as np
from jax import lax
from jax.experimental import pallas as pl
from jax.experimental.pallas import tpu as pltpu

F32 = jnp.float32
BF16 = jnp.bfloat16
HIGHEST = lax.Precision.HIGHEST

GRID_W = 64
N_MOD = 6
NORM_EPS = 1e-6
GLA_HEADS = 4
GLA_DK = 128
GLA_DV = 256
GLA_LOWRANK = 16
GLA_TAU = 16.0
GLA_CHUNK = 64
GLA_SUB = 16
GLA_HEADS_PER_STEP = 2
GLA_SAFE_EXP = 60.0
ROPE_BASE = 10000.0
NA_HEADS = 8
NA_HD = 128
NA_KH = 8
NA_KW = 16
NA_QROWS = 4
NA_UNION = NA_KH + NA_QROWS
NA_BLOCK_UNROLL = 2
HY_W = 1024
HY_ORDER = 2
HY_BANDS = 16
HY_HIDDEN = 64
HY_DECAY_TARGET = 1e-2
HY_DECAY_PCT_SHORT = 0.3
HY_DECAY_PCT_LONG = 1.5
HY_RADIX = 8
HY_RADIX_MIN_SEQ = 2048
HY_KC = 256
HY_TC_FFT = 256

V7X_VMEM_LIMIT_BYTES = 56 * 1024 * 1024
LANES = 128

KEY_W = GLA_HEADS * GLA_DK
VAL_W = GLA_HEADS * GLA_DV
NA_W = NA_HEADS * NA_HD
OFF_GQ = 0
OFF_GK = OFF_GQ + KEY_W
OFF_GV = OFF_GK + KEY_W
OFF_GG = OFF_GV + VAL_W
OFF_NQ = OFF_GG + VAL_W
OFF_NK = OFF_NQ + NA_W
OFF_NV = OFF_NK + NA_W
OFF_GATES = OFF_NV + NA_W
OFF_HY = OFF_GATES + 3 * 2048


def _cparams(*sem):
    return pltpu.CompilerParams(dimension_semantics=sem, vmem_limit_bytes=V7X_VMEM_LIMIT_BYTES)


def _modulated_norm(x, g, scale, shift):
    ms = jnp.mean(x * x, axis=-1, keepdims=True)
    y = x * lax.rsqrt(ms + NORM_EPS) * g
    return y * (1.0 + scale) + shift


def _adaln_kernel(c_ref, w_ref, b_ref, o_ref):
    c = c_ref[...]
    a = (c * jax.nn.sigmoid(c)).astype(BF16)
    o_ref[...] = jnp.dot(a, w_ref[...].astype(BF16), preferred_element_type=F32) + b_ref[...]


def adaln(cond, w_mod, b_mod, layer):
    m, d = cond.shape
    n = w_mod.shape[2]
    tn = 1024
    return pl.pallas_call(
        _adaln_kernel,
        grid=(n // tn,),
        in_specs=[pl.BlockSpec((m, d), lambda j: (0, 0)),
                  pl.BlockSpec((None, d, tn), lambda j: (layer, 0, j)),
                  pl.BlockSpec((1, tn), lambda j: (0, j))],
        out_specs=pl.BlockSpec((m, tn), lambda j: (0, j)),
        out_shape=jax.ShapeDtypeStruct((m, n), F32),
        compiler_params=_cparams("parallel"),
        name="adaln",
    )(cond, w_mod, b_mod)


def _log_decay(h, wga_ref, wa2_ref, ba2_ref):
    ga = jnp.dot(h, wga_ref[...], preferred_element_type=F32)
    pre = jnp.dot(ga.astype(BF16), wa2_ref[...], preferred_element_type=F32) + ba2_ref[...]
    log_sig = jnp.minimum(pre, 0.0) - jnp.log(1.0 + jnp.exp(-jnp.abs(pre)))
    return log_sig * (1.0 / GLA_TAU)


def _norm_matmul_kernel(*refs, relu2, decay):
    x_ref, g_ref, sc_ref, sh_ref, w_ref = refs[:5]
    if decay:
        wga_ref, wa2_ref, ba2_ref, o_ref, la_ref, h_ref = refs[5:]
    else:
        o_ref, h_ref = refs[5:]

    @pl.when(pl.program_id(1) == 0)
    def _():
        h = _modulated_norm(x_ref[...], g_ref[...], sc_ref[0], sh_ref[0]).astype(BF16)
        h_ref[...] = h
        if decay:
            la_ref[...] = _log_decay(h, wga_ref, wa2_ref, ba2_ref)

    acc = jnp.dot(h_ref[...], w_ref[...], preferred_element_type=F32)
    if relu2:
        acc = jnp.square(jnp.maximum(acc, 0.0))
    o_ref[...] = acc.astype(o_ref.dtype)


def norm_matmul(x, g, scale, shift, w, *, rows_per_mod, tm, tn, relu2, out_dtype, decay_weights=None):
    t, d = x.shape
    n = w.shape[1]
    mod_map = lambda i, j: ((i * tm) // rows_per_mod, 0, 0)
    decay = decay_weights is not None
    in_specs = [pl.BlockSpec((tm, d), lambda i, j: (i, 0)),
                pl.BlockSpec((1, d), lambda i, j: (0, 0)),
                pl.BlockSpec((1, 1, d), mod_map),
                pl.BlockSpec((1, 1, d), mod_map),
                pl.BlockSpec((d, tn), lambda i, j: (0, j))]
    out_specs = pl.BlockSpec((tm, tn), lambda i, j: (i, j))
    out_shape = jax.ShapeDtypeStruct((t, n), out_dtype)
    args = [x, g, scale, shift, w]
    if decay:
        in_specs += [pl.BlockSpec(a.shape, lambda i, j: (0, 0)) for a in decay_weights]
        n_la = decay_weights[1].shape[1]
        out_specs = [out_specs, pl.BlockSpec((tm, n_la), lambda i, j: (i, 0))]
        out_shape = [out_shape, jax.ShapeDtypeStruct((t, n_la), F32)]
        args += list(decay_weights)
    return pl.pallas_call(
        functools.partial(_norm_matmul_kernel, relu2=relu2, decay=decay),
        grid=(t // tm, n // tn),
        in_specs=in_specs, out_specs=out_specs, out_shape=out_shape,
        scratch_shapes=[pltpu.VMEM((tm, d), BF16)],
        compiler_params=_cparams("parallel", "arbitrary"),
        name="norm_matmul_relu2" if relu2 else "norm_matmul",
    )(*args)


def _cumulative_log_decay(la, reverse):
    c = la.shape[0]
    row = lax.broadcasted_iota(jnp.int32, (c, c), 0)
    col = lax.broadcasted_iota(jnp.int32, (c, c), 1)
    tri = jnp.where((row <= col) if reverse else (row >= col), 1.0, 0.0).astype(BF16)
    hi = la.astype(BF16)
    rest = la - hi.astype(F32)
    mid = rest.astype(BF16)
    lo = (rest - mid.astype(F32)).astype(BF16)
    dot = lambda t: jnp.dot(tri, t, preferred_element_type=F32)
    return dot(hi) + dot(mid) + dot(lo)


def _gla_chunk_scores(q, k, v, b, *, reverse, exact):
    c = GLA_CHUNK
    n = GLA_SUB
    edge = b[0:1, :] if reverse else b[c - 1:c, :]
    qe = (q * jnp.exp(b)).astype(BF16)
    kd = (k * jnp.exp(edge - b)).astype(BF16)
    m = jnp.dot(v.T.astype(BF16), kd, preferred_element_type=F32)
    decay = jnp.exp(edge)

    lane = lax.broadcasted_iota(jnp.int32, (n, c), 1)
    sub_row = lax.broadcasted_iota(jnp.int32, (n, c), 0)
    blocks = []
    cap = 0.0 if exact else GLA_SAFE_EXP
    for i in range(c // n):
        lo, hi = i * n, (i + 1) * n
        q_i = q[lo:hi, :]
        b_i = b[lo:hi, :]
        has_border = (hi < c) if reverse else (lo > 0)
        if has_border:
            r = b[hi:hi + 1, :] if reverse else b[lo - 1:lo, :]
            lhs = (q_i * jnp.exp(b_i - r)).astype(BF16)
            rhs = (k * jnp.exp(jnp.minimum(r - b, cap))).astype(BF16)
        else:
            lhs = (q_i * jnp.exp(b_i)).astype(BF16)
            rhs = (k * jnp.exp(jnp.minimum(-b, cap))).astype(BF16)
        a_i = lax.dot_general(lhs, rhs, (((1,), (1,)), ((), ())), preferred_element_type=F32)
        if exact:
            a_diag = jnp.zeros((n, c), F32)
            for j in range(n):
                s = lo + j
                p = q_i * k[s:s + 1, :] * jnp.exp(b_i - b[s:s + 1, :])
                a_diag = jnp.where(lane == s, jnp.sum(p, axis=1, keepdims=True), a_diag)
            in_block = (lane >= lo) & (lane < hi)
            a_i = jnp.where(in_block, a_diag, a_i)
        causal = (lane >= sub_row + lo) if reverse else (lane <= sub_row + lo)
        blocks.append(jnp.where(causal, a_i, 0.0))
    a = jnp.concatenate(blocks, axis=0).astype(BF16)
    return qe, a, m, decay


def _gla_inter(qe, st):
    return lax.dot_general(qe, st.astype(BF16), (((1,), (1,)), ((), ())), preferred_element_type=F32)


def _gla_kernel(*refs, reverse, rope, has_s0, nchunk, heads):
    it = iter(refs)
    q_ref, k_ref, v_ref, la_ref = next(it), next(it), next(it), next(it)
    cos_ref = sin_ref = s0_ref = None
    if rope:
        cos_ref, sin_ref = next(it), next(it)
    if has_s0:
        s0_ref = next(it)
    o_ref, sfin_ref, st_ref = next(it), next(it), next(it)

    blk = pl.program_id(2)

    @pl.when(blk == 0)
    def _():
        for hh in range(heads):
            st_ref[hh] = s0_ref[0, 0, 0, hh].T if has_s0 else jnp.zeros(st_ref.shape[1:], F32)

    kcols = lambda hh: slice(hh * GLA_DK, (hh + 1) * GLA_DK)
    vcols = lambda hh: slice(hh * GLA_DV, (hh + 1) * GLA_DV)

    def load_qk(rows, hh):
        q = q_ref[rows, kcols(hh)] * (GLA_DK ** -0.5)
        k = k_ref[rows, kcols(hh)]
        if rope:
            cs, sn = cos_ref[rows, :], sin_ref[rows, :]
            q = q * cs + pltpu.roll(q, GLA_DK // 2, 1) * sn
            k = k * cs + pltpu.roll(k, GLA_DK // 2, 1) * sn
        return q, k

    def intra(a, rows, hh):
        return jnp.dot(a, v_ref[rows, vcols(hh)].astype(BF16), preferred_element_type=F32)

    def run_exact():
        def body(step, carry):
            ci = (nchunk - 1 - step) if reverse else step
            rows = pl.ds(pl.multiple_of(ci * GLA_CHUNK, GLA_CHUNK), GLA_CHUNK)
            for hh in range(heads):
                q, k = load_qk(rows, hh)
                b = _cumulative_log_decay(la_ref[rows, kcols(hh)], reverse)
                qe, a, m, decay = _gla_chunk_scores(q, k, v_ref[rows, vcols(hh)], b, reverse=reverse, exact=True)
                o_ref[rows, vcols(hh)] = _gla_inter(qe, st_ref[hh]) + intra(a, rows, hh)
                st_ref[hh] = st_ref[hh] * decay + m
            return carry

        lax.fori_loop(0, nchunk, body, 0)

    def run_fast():
        order = list(range(nchunk))[::-1] if reverse else list(range(nchunk))
        rows = [pl.ds(ci * GLA_CHUNK, GLA_CHUNK) for ci in range(nchunk)]
        units = [(hh, ci) for hh in range(heads) for ci in range(nchunk)]
        bs = {u: _cumulative_log_decay(la_ref[rows[u[1]], kcols(u[0])], reverse) for u in units}
        scores = {u: _gla_chunk_scores(*load_qk(rows[u[1]], u[0]), v_ref[rows[u[1]], vcols(u[0])], bs[u],
                                       reverse=reverse, exact=False) for u in units}
        o_intra = {u: intra(scores[u][1], rows[u[1]], u[0]) for u in units}
        states = {}
        for hh in range(heads):
            st = st_ref[hh]
            for ci in order:
                states[hh, ci] = st
                st = st * scores[hh, ci][3] + scores[hh, ci][2]
            st_ref[hh] = st
        for hh, ci in units:
            o_ref[rows[ci], vcols(hh)] = _gla_inter(scores[hh, ci][0], states[hh, ci]) + o_intra[hh, ci]

    steep = jnp.min(la_ref[...]) < -(GLA_SAFE_EXP / GLA_SUB)
    pl.when(steep)(run_exact)
    pl.when(jnp.logical_not(steep))(run_fast)

    @pl.when(blk == pl.num_programs(2) - 1)
    def _():
        for hh in range(heads):
            sfin_ref[0, hh] = st_ref[hh].T


def gla_direction(proj, la, rope_tabs, s0, *, batch, seq, reverse, layer, direction, rows_blk):
    t = proj.shape[0]
    nblk = seq // rows_blk
    nchunk = rows_blk // GLA_CHUNK
    rope = rope_tabs is not None
    has_s0 = s0 is not None

    def rb(b, c):
        cc = (nblk - 1 - c) if reverse else c
        return b * nblk + cc

    hp = GLA_HEADS_PER_STEP
    kw, vw = hp * GLA_DK, hp * GLA_DV
    assert OFF_GQ % kw == 0 and OFF_GK % kw == 0 and OFF_GV % vw == 0 and KEY_W % kw == 0
    kq, kk, kv = OFF_GQ // kw, OFF_GK // kw, OFF_GV // vw
    kla = direction * (KEY_W // kw)
    in_specs = [pl.BlockSpec((rows_blk, kw), lambda b, h, c: (rb(b, c), kq + h)),
                pl.BlockSpec((rows_blk, kw), lambda b, h, c: (rb(b, c), kk + h)),
                pl.BlockSpec((rows_blk, vw), lambda b, h, c: (rb(b, c), kv + h)),
                pl.BlockSpec((rows_blk, kw), lambda b, h, c: (rb(b, c), kla + h))]
    args = [proj, proj, proj, la]
    if rope:
        tab_map = lambda b, h, c: ((nblk - 1 - c) if reverse else c, 0)
        in_specs += [pl.BlockSpec((rows_blk, GLA_DK), tab_map)] * 2
        args += list(rope_tabs)
    if has_s0:
        in_specs.append(pl.BlockSpec((1, 1, 1, hp, GLA_DK, GLA_DV), lambda b, h, c: (b, layer, direction, h, 0, 0)))
        args.append(s0)
    o, sfin = pl.pallas_call(
        functools.partial(_gla_kernel, reverse=reverse, rope=rope, has_s0=has_s0, nchunk=nchunk, heads=hp),
        grid=(batch, GLA_HEADS // hp, nblk),
        in_specs=in_specs,
        out_specs=[pl.BlockSpec((rows_blk, vw), lambda b, h, c: (rb(b, c), h)),
                   pl.BlockSpec((1, hp, GLA_DK, GLA_DV), lambda b, h, c: (b, h, 0, 0))],
        out_shape=[jax.ShapeDtypeStruct((t, VAL_W), F32),
                   jax.ShapeDtypeStruct((batch, GLA_HEADS, GLA_DK, GLA_DV), F32)],
        scratch_shapes=[pltpu.VMEM((hp, GLA_DV, GLA_DK), F32)],
        compiler_params=_cparams("parallel", "parallel", "arbitrary"),
        name="gla_bwd" if reverse else "gla_fwd",
    )(*args)
    return o, sfin


def _head_norm(x, g):
    return x * lax.rsqrt(jnp.mean(x * x, axis=-1, keepdims=True) + NORM_EPS) * g


def _head_norm_mxu(x, g):
    n = x.shape[1]
    mean_sq = jnp.dot((x * x).astype(BF16), jnp.full((n, n), 1.0 / n, BF16), preferred_element_type=F32)
    return x * lax.rsqrt(mean_sq + NORM_EPS) * g


def _ctx_attn_kernel(q_ref, k_ref, v_ref, gq_ref, gk_ref, o_ref, kn_ref, vn_ref):
    scale = NA_HD ** -0.5
    for h in range(NA_HEADS):
        cols = slice(h * NA_HD, (h + 1) * NA_HD)
        qn = _head_norm(q_ref[:, cols], gq_ref[...]) * scale
        kn = _head_norm(k_ref[:, cols], gk_ref[...])
        v = v_ref[:, cols]
        kn_ref[0, h] = kn
        vn_ref[0, h] = v
        s = lax.dot_general(qn.astype(BF16), kn.astype(BF16), (((1,), (1,)), ((), ())),
                            preferred_element_type=F32)
        e = jnp.exp(s - jnp.max(s, axis=-1, keepdims=True))
        p = e / jnp.sum(e, axis=-1, keepdims=True)
        o_ref[:, cols] = jnp.dot(p.astype(BF16), v.astype(BF16), preferred_element_type=F32)


def ctx_attention(proj, gq, gk, *, batch, seq):
    t = proj.shape[0]
    cq, ck, cv = OFF_NQ // NA_W, OFF_NK // NA_W, OFF_NV // NA_W
    cache_shape = jax.ShapeDtypeStruct((batch, NA_HEADS, seq, NA_HD), F32)
    cache_spec = pl.BlockSpec((1, NA_HEADS, seq, NA_HD), lambda b: (b, 0, 0, 0))
    return pl.pallas_call(
        _ctx_attn_kernel,
        grid=(batch,),
        in_specs=[pl.BlockSpec((seq, NA_W), lambda b: (b, cq)),
                  pl.BlockSpec((seq, NA_W), lambda b: (b, ck)),
                  pl.BlockSpec((seq, NA_W), lambda b: (b, cv)),
                  pl.BlockSpec((1, NA_HD), lambda b: (0, 0)),
                  pl.BlockSpec((1, NA_HD), lambda b: (0, 0))],
        out_specs=[pl.BlockSpec((seq, NA_W), lambda b: (b, 0)), cache_spec, cache_spec],
        out_shape=[jax.ShapeDtypeStruct((t, NA_W), F32), cache_shape, cache_shape],
        compiler_params=_cparams("parallel"),
        name="ctx_attention",
    )(proj, proj, proj, gq, gk)


def _na_kernel(q_ref, k_ref, v_ref, kc_ref, vc_ref, gq_ref, gk_ref, tab_ref, o_ref, qn_s, kn_s, v_s, *, rows):
    scale = NA_HD ** -0.5
    qn_s[...] = (_head_norm_mxu(q_ref[...], gq_ref[...]) * scale).astype(BF16)
    kn_s[...] = _head_norm_mxu(k_ref[...], gk_ref[...]).astype(BF16)
    v_s[...] = v_ref[...].astype(BF16)
    kctx = kc_ref[0, 0, 0].astype(BF16)
    vctx = vc_ref[0, 0, 0].astype(BF16)
    nblk = rows // NA_QROWS
    qlen, win = NA_QROWS * GRID_W, NA_UNION * GRID_W

    def scores(g):
        u0 = jnp.clip(g * NA_QROWS - NA_KH // 2, 0, rows - NA_UNION)
        cls = jnp.where(g == 0, 0, jnp.where(g == nblk - 1, 2, 1))
        qrows = pl.ds(pl.multiple_of(g * qlen, qlen), qlen)
        wrows = pl.ds(pl.multiple_of(u0 * GRID_W, GRID_W), win)
        qn = qn_s[qrows, :]
        s_loc = lax.dot_general(qn, kn_s[wrows, :], (((1,), (1,)), ((), ())), preferred_element_type=F32)
        s_ctx = lax.dot_general(qn, kctx, (((1,), (1,)), ((), ())), preferred_element_type=F32)
        return qrows, wrows, s_loc + tab_ref[0, cls], s_ctx

    def softmax(s_loc, s_ctx):
        m = jnp.maximum(jnp.max(s_loc, axis=-1, keepdims=True), jnp.max(s_ctx, axis=-1, keepdims=True))
        e_loc = jnp.exp(s_loc - m)
        e_ctx = jnp.exp(s_ctx - m)
        den = jnp.sum(e_loc, axis=-1, keepdims=True) + jnp.sum(e_ctx, axis=-1, keepdims=True)
        return e_loc.astype(BF16), e_ctx.astype(BF16), den

    def body(step, carry):
        blocks = [scores(step * NA_BLOCK_UNROLL + u) for u in range(NA_BLOCK_UNROLL)]
        probs = [softmax(s_loc, s_ctx) for _, _, s_loc, s_ctx in blocks]
        for (qrows, wrows, _, _), (e_loc, e_ctx, den) in zip(blocks, probs):
            acc = jnp.dot(e_loc, v_s[wrows, :], preferred_element_type=F32)
            acc = acc + jnp.dot(e_ctx, vctx, preferred_element_type=F32)
            o_ref[qrows, :] = acc / den
        return carry

    lax.fori_loop(0, nblk // NA_BLOCK_UNROLL, body, 0)


def neighbourhood_attention(proj, cache_k, cache_v, gq, gk, table, *, batch, seq, layer, row0):
    rows = seq // GRID_W
    assert rows % NA_QROWS == 0 and rows // NA_QROWS >= 3 and rows >= NA_UNION
    past = cache_k.shape[3]
    cq, ck, cv = OFF_NQ // NA_HD, OFF_NK // NA_HD, OFF_NV // NA_HD
    cache_spec = pl.BlockSpec((1, 1, 1, past, NA_HD), lambda b, h: (b, layer, h, 0, 0))
    return pl.pallas_call(
        functools.partial(_na_kernel, rows=rows),
        grid=(batch, NA_HEADS),
        in_specs=[pl.BlockSpec((seq, NA_HD), lambda b, h: (row0 + b, cq + h)),
                  pl.BlockSpec((seq, NA_HD), lambda b, h: (row0 + b, ck + h)),
                  pl.BlockSpec((seq, NA_HD), lambda b, h: (row0 + b, cv + h)),
                  cache_spec, cache_spec,
                  pl.BlockSpec((1, NA_HD), lambda b, h: (0, 0)),
                  pl.BlockSpec((1, NA_HD), lambda b, h: (0, 0)),
                  pl.BlockSpec((1, 3, NA_QROWS * GRID_W, NA_UNION * GRID_W), lambda b, h: (h, 0, 0, 0))],
        out_specs=pl.BlockSpec((seq, NA_HD), lambda b, h: (b, h)),
        out_shape=jax.ShapeDtypeStruct((batch * seq, NA_W), F32),
        scratch_shapes=[pltpu.VMEM((seq, NA_HD), BF16)] * 3,
        compiler_params=_cparams("parallel", "parallel"),
        name="neighbourhood_attention",
    )(proj, proj, proj, cache_k, cache_v, gq, gk, table)


def na_bias_table(rpb, rows):
    nblk = rows // NA_QROWS
    col = np.arange(GRID_W)
    c_start = np.clip(col - NA_KW // 2, 0, GRID_W - NA_KW)
    col_in = (col[None, :] >= c_start[:, None]) & (col[None, :] < c_start[:, None] + NA_KW)
    col_idx = np.clip(col[None, :] - col[:, None] + NA_KW - 1, 0, 2 * NA_KW - 2)
    row_off, row_in = [], []
    for g in (0, 1, nblk - 1):
        u0 = np.clip(g * NA_QROWS - NA_KH // 2, 0, rows - NA_UNION)
        r = g * NA_QROWS + np.arange(NA_QROWS)[:, None]
        r_start = np.clip(r - NA_KH // 2, 0, rows - NA_KH)
        key_row = u0 + np.arange(NA_UNION)[None, :]
        row_in.append((key_row >= r_start) & (key_row < r_start + NA_KH))
        row_off.append(np.clip(key_row - r + NA_KH - 1, 0, 2 * NA_KH - 2))
    row_off, row_in = np.stack(row_off), np.stack(row_in)
    n_off = 2 * NA_KH - 1
    tiles = jnp.where(col_in[None, None], rpb[:, :, col_idx], -jnp.inf)
    tiles = jnp.concatenate([tiles, jnp.full_like(tiles[:, :1], -jnp.inf)], axis=1)
    tab = tiles[:, np.where(row_in, row_off, n_off)]
    tab = jnp.transpose(tab, (0, 1, 2, 4, 3, 5))
    return tab.reshape(rpb.shape[0], 3, NA_QROWS * GRID_W, NA_UNION * GRID_W).astype(F32)


def _filter_kernel(z_ref, f1w_ref, f1b_ref, f2w_ref, f2b_ref, f3w_ref, fr_ref, dl_ref, o_ref, *, seq):
    fr = fr_ref[...]
    h = jnp.sin(fr * (jnp.dot(z_ref[...], f1w_ref[...], preferred_element_type=F32, precision=HIGHEST)
                      + f1b_ref[...]))
    h = jnp.sin(fr * (jnp.dot(h, f2w_ref[...], preferred_element_type=F32, precision=HIGHEST)
                      + f2b_ref[...]))
    h = jnp.dot(h, f3w_ref[...], preferred_element_type=F32, precision=HIGHEST)
    t = lax.broadcasted_iota(jnp.int32, (seq, 1), 0).astype(F32)
    dist = jnp.abs(t - float(seq // 2)) * (2.0 / seq)
    h = h * jnp.exp(-dist * dl_ref[...])
    o_ref[...] = h * lax.rsqrt(jnp.sum(h * h, axis=0, keepdims=True) + NORM_EPS)


def hyena_filters(zpos, f1w, f1b, f2w, f2b, f3w, freq, deltas, *, seq):
    n = f3w.shape[1]
    tc = 512
    full = lambda a: pl.BlockSpec(a.shape, lambda j: (0,) * a.ndim)
    return pl.pallas_call(
        functools.partial(_filter_kernel, seq=seq),
        grid=(n // tc,),
        in_specs=[full(zpos), full(f1w), full(f1b), full(f2w), full(f2b),
                  pl.BlockSpec((HY_HIDDEN, tc), lambda j: (0, j)), full(freq),
                  pl.BlockSpec((1, tc), lambda j: (0, j))],
        out_specs=pl.BlockSpec((seq, tc), lambda j: (0, j)),
        out_shape=jax.ShapeDtypeStruct((seq, n), F32),
        compiler_params=_cparams("parallel"),
        name="hyena_filters",
    )(zpos, f1w, f1b, f2w, f2b, f3w, freq, deltas)


def _lane_tile(t, width):
    return jnp.concatenate([t] * (width // LANES), axis=1) if width > LANES else t


def _cdft(vals, sign):
    if len(vals) == 1:
        return vals
    if len(vals) == 2:
        (ar, ai), (br, bi) = vals
        return [(ar + br, ai + bi), (ar - br, ai - bi)]
    if len(vals) == 8:
        ev, od = _cdft(vals[0::2], sign), _cdft(vals[1::2], sign)
        h = math.sqrt(0.5)
        (o0r, o0i), (o1r, o1i), (o2r, o2i), (o3r, o3i) = od
        tw = [(o0r, o0i),
              ((o1r - sign * o1i) * h, (o1i + sign * o1r) * h),
              (-sign * o2i, sign * o2r),
              ((-o3r - sign * o3i) * h, (sign * o3r - o3i) * h)]
        return ([(er + tr, ei + ti) for (er, ei), (tr, ti) in zip(ev, tw)]
                + [(er - tr, ei - ti) for (er, ei), (tr, ti) in zip(ev, tw)])
    assert len(vals) == 4
    (x0r, x0i), (x1r, x1i), (x2r, x2i), (x3r, x3i) = vals
    a0r, a0i, a1r, a1i = x0r + x2r, x0i + x2i, x0r - x2r, x0i - x2i
    a2r, a2i, a3r, a3i = x1r + x3r, x1i + x3i, x1r - x3r, x1i - x3i
    wr, wi = -sign * a3i, sign * a3r
    return [(a0r + a2r, a0i + a2i), (a1r + wr, a1i + wi), (a0r - a2r, a0i - a2i), (a1r - wr, a1i - wi)]


def _fwd_spectrum(z_list, f_ref, rows):
    lp = f_ref.shape[2]
    xs = []
    for r, z in enumerate(z_list):
        xc = jnp.dot(f_ref[r, rows, :], z, preferred_element_type=F32)
        xn = jnp.dot(f_ref[r, pl.ds(lp + rows.start, rows.size), :], z, preferred_element_type=F32)
        xs.append((xc, -xn))
    return _cdft(xs, -1)


def _spectrum_kernel(*refs, radix, seq, kc, tc):
    h_refs, (f_ref, sh_ref, o_ref) = refs[:radix], refs[radix:]
    lp = seq // radix
    hs = [h[...].astype(BF16) for h in h_refs]
    for c0 in range(0, lp, kc):
        rows = pl.ds(c0, kc)
        cs = _lane_tile(sh_ref[0, rows, :], tc) * (1.0 / seq)
        sn = _lane_tile(sh_ref[1, rows, :], tc) * (1.0 / seq)
        for q, (re, im) in enumerate(_fwd_spectrum(hs, f_ref, rows)):
            o_ref[0, q, rows, :] = re * cs - im * sn
            o_ref[1, q, rows, :] = re * sn + im * cs


def _whole(a):
    return pl.BlockSpec(a.shape, lambda *_: (0,) * a.ndim, pipeline_mode=pl.Buffered(1))


def filter_spectrum(filt, tabs, *, seq):
    n = filt.shape[1]
    radix, tc = _hy_radix(seq), _hy_fft_tc(seq)
    lp = seq // radix
    view = filt.reshape(lp, radix * n)
    f, _, sh = tabs
    groups = [pl.BlockSpec((lp, tc), functools.partial(lambda j, r: (0, r * (n // tc) + j), r=r)) for r in range(radix)]
    return pl.pallas_call(
        functools.partial(_spectrum_kernel, radix=radix, seq=seq, kc=min(HY_KC, lp), tc=tc),
        grid=(n // tc,),
        in_specs=groups + [_whole(f), _whole(sh)],
        out_specs=pl.BlockSpec((2, radix, lp, tc), lambda j: (0, 0, 0, j)),
        out_shape=jax.ShapeDtypeStruct((2, radix, lp, n), F32),
        compiler_params=_cparams("parallel"),
        name="filter_spectrum",
    )(*([view] * radix), f, sh)


def _short_conv(u, w, bias, seq):
    t = lax.broadcasted_iota(jnp.int32, (seq, 1), 0)
    prev = jnp.where(t == 0, 0.0, pltpu.roll(u, 1, 0))
    nxt = jnp.where(t == seq - 1, 0.0, pltpu.roll(u, seq - 1, 0))
    return bias + prev * w[0:1, :] + u * w[1:2, :] + nxt * w[2:3, :]


def _hy_radix(seq):
    return HY_RADIX if seq >= HY_RADIX_MIN_SEQ else 1


def _split_rows(val, out_refs, tmp_ref):
    radix = len(out_refs)
    if radix == 1:
        out_refs[0][...] = val.astype(BF16)
        return
    tmp_ref[...] = val
    for r in range(radix):
        out_refs[r][...] = tmp_ref[pl.ds(r, val.shape[0] // radix, stride=radix), :].astype(BF16)


def _merge_rows(in_refs, tmp_ref):
    radix = len(in_refs)
    if radix == 1:
        return in_refs[0][...]
    for r in range(radix):
        tmp_ref[pl.ds(r, in_refs[r].shape[0], stride=radix), :] = in_refs[r][...]
    return tmp_ref[...]


def _hy_pre_kernel(*refs, seq, radix):
    (u_ref, w_ref, b_ref), z_refs, tmp = refs[:3], refs[3:3 + radix], refs[3 + radix:]
    z = _short_conv(u_ref[...], w_ref[...], b_ref[...], seq)
    _split_rows(z, z_refs, tmp[0] if tmp else None)


def _hy_gate_kernel(*refs, seq, radix):
    ux_ref, uz_ref, wx_ref, bx_ref, wz_ref, bz_ref = refs[:6]
    cv_refs, hb_ref, zo_ref = refs[6:6 + radix], refs[6 + radix], refs[7 + radix]
    zb_refs, tmp = refs[8 + radix:8 + 2 * radix], refs[8 + 2 * radix:]
    tmp_ref = tmp[0] if tmp else None
    x = _short_conv(ux_ref[...], wx_ref[...], bx_ref[...], seq)
    z = _short_conv(uz_ref[...], wz_ref[...], bz_ref[...], seq)
    out = x * (_merge_rows(cv_refs, tmp_ref) + hb_ref[...] * z)
    zo_ref[...] = out
    _split_rows(out, zb_refs, tmp_ref)


def _hy_final_kernel(*refs, seq, radix):
    (ux_ref, wx_ref, bx_ref), cv_refs = refs[:3], refs[3:3 + radix]
    hb_ref, z_ref, zb_ref = refs[3 + radix:6 + radix]
    tmp = refs[6 + radix:]
    x = _short_conv(ux_ref[...], wx_ref[...], bx_ref[...], seq)
    cv = _merge_rows(cv_refs, tmp[0] if tmp else None)
    zb_ref[...] = (x * (cv + hb_ref[...] * z_ref[...])).astype(BF16)


HY_ELEM_SLAB_BYTES = 2 * 1024 * 1024


def _hy_tc(seq):
    if _hy_radix(seq) > 1:
        return LANES
    return max(LANES, min(HY_W, HY_ELEM_SLAB_BYTES // (4 * seq)))


def _hy_specs(seq, row0, col0):
    tc = _hy_tc(seq)
    cb, wb = col0 // tc, (col0 - OFF_HY) // tc
    u = pl.BlockSpec((seq, tc), lambda b, j: (row0 + b, cb + j))
    w = pl.BlockSpec((3, tc), lambda b, j: (0, wb + j))
    bb = pl.BlockSpec((1, tc), lambda b, j: (0, wb + j))
    act = pl.BlockSpec((seq, tc), lambda b, j: (b, j))
    hb = pl.BlockSpec((1, tc), lambda b, j: (0, j))
    part = pl.BlockSpec((seq // _hy_radix(seq), tc), lambda b, j: (b, j))
    return u, w, bb, act, hb, part


def _hy_elementwise_call(kernel, name, seq, batch, in_specs, out_specs, out_shape, args):
    radix, tc = _hy_radix(seq), _hy_tc(seq)
    assert radix == 1 or tc == LANES
    return pl.pallas_call(
        functools.partial(kernel, seq=seq, radix=radix),
        grid=(batch, HY_W // tc),
        in_specs=in_specs, out_specs=out_specs, out_shape=out_shape,
        scratch_shapes=[pltpu.VMEM((seq, tc), F32)] if radix > 1 else [],
        compiler_params=_cparams("parallel", "parallel"), name=name,
    )(*args)


def hyena_pre(proj, conv_w, conv_b, *, batch, seq, row0):
    radix = _hy_radix(seq)
    u, w, bb, _, _, part = _hy_specs(seq, row0, OFF_HY + HY_ORDER * HY_W)
    parts = [jax.ShapeDtypeStruct((batch * seq // radix, HY_W), BF16)] * radix
    return _hy_elementwise_call(_hy_pre_kernel, "hyena_pre", seq, batch, [u, w, bb], [part] * radix, parts,
                                (proj, conv_w, conv_b))


def hyena_gate(proj, conv_w, conv_b, conv, hy_bias, *, batch, seq, row0):
    radix = _hy_radix(seq)
    ux, wx, bx, act, hb, part = _hy_specs(seq, row0, OFF_HY)
    uz, wz, bz, _, _, _ = _hy_specs(seq, row0, OFF_HY + HY_ORDER * HY_W)
    parts = [jax.ShapeDtypeStruct((batch * seq // radix, HY_W), BF16)] * radix
    outs = _hy_elementwise_call(
        _hy_gate_kernel, "hyena_gate", seq, batch,
        [ux, uz, wx, bx, wz, bz] + [part] * radix + [hb], [act] + [part] * radix,
        [jax.ShapeDtypeStruct((batch * seq, HY_W), F32)] + parts,
        (proj, proj, conv_w, conv_b, conv_w, conv_b, *conv, hy_bias[0:1]))
    return outs[0], outs[1:]


def hyena_final(proj, conv_w, conv_b, conv, hy_bias, z1, *, batch, seq, row0):
    radix = _hy_radix(seq)
    ux, wx, bx, act, hb, part = _hy_specs(seq, row0, OFF_HY + HY_W)
    return _hy_elementwise_call(
        _hy_final_kernel, "hyena_final", seq, batch, [ux, wx, bx] + [part] * radix + [hb, act], act,
        jax.ShapeDtypeStruct((batch * seq, HY_W), BF16), (proj, conv_w, conv_b, *conv, hy_bias[1:2], z1))


def _long_conv_kernel(*refs, radix, seq, kc, tc):
    z_refs, (f_ref, ft_ref, g_ref) = refs[:radix], refs[radix:radix + 3]
    o_refs, u_s = refs[radix + 3:2 * radix + 3], refs[2 * radix + 3]
    lp = seq // radix
    r_out = pl.program_id(2)

    @pl.when(r_out == 0)
    def _():
        zs = [z[...] for z in z_refs]
        for c0 in range(0, lp, kc):
            rows = pl.ds(c0, kc)
            ys = []
            for q, (zr, zi) in enumerate(_fwd_spectrum(zs, f_ref, rows)):
                gr, gi = g_ref[0, q, rows, :], g_ref[1, q, rows, :]
                ys.append((zr * gr - zi * gi, zr * gi + zi * gr))
            for r, (re, im) in enumerate(_cdft(ys, +1)):
                u_s[r, rows, :] = re.astype(BF16)
                u_s[r, pl.ds(lp + c0, kc), :] = (-im).astype(BF16)

    for r in range(radix):
        @pl.when(r_out == r)
        def _(r=r):
            o_refs[r][...] = jnp.dot(ft_ref[r], u_s[r], preferred_element_type=F32)


def _hy_fft_tc(seq):
    return HY_TC_FFT if _hy_radix(seq) > 1 else 2 * HY_TC_FFT


def hyena_long_conv(z_parts, g, tabs, *, batch, seq, order):
    c = z_parts[0].shape[1]
    radix, tc = _hy_radix(seq), _hy_fft_tc(seq)
    lp = seq // radix
    f, ft, _ = tabs
    ncb = c // tc
    goff = order * ncb
    part = pl.BlockSpec((lp, tc), lambda j, b, ro: (b, j))
    return pl.pallas_call(
        functools.partial(_long_conv_kernel, radix=radix, seq=seq, kc=min(HY_KC, lp), tc=tc),
        grid=(ncb, batch, radix),
        in_specs=[part] * radix + [_whole(f), _whole(ft),
                                   pl.BlockSpec((2, radix, lp, tc), lambda j, b, ro: (0, 0, 0, goff + j),
                                                pipeline_mode=pl.Buffered(1))],
        out_specs=[part] * radix,
        out_shape=[jax.ShapeDtypeStruct((batch * lp, c), F32)] * radix,
        scratch_shapes=[pltpu.VMEM((radix, 2 * lp, tc), BF16)],
        compiler_params=_cparams("parallel", "parallel", "arbitrary"),
        name="hyena_long_conv",
    )(*z_parts, f, ft, g)


def dft_tables(seq):
    radix = _hy_radix(seq)
    lp = seq // radix
    odd = 2 * jnp.arange(lp, dtype=jnp.int32) + 1

    def cs(num, den):
        ang = (num % (2 * den)).astype(F32) * (math.pi / den)
        return jnp.cos(ang), jnp.sin(ang)

    m = radix * jnp.arange(lp, dtype=jnp.int32)[None, None, :] + jnp.arange(radix, dtype=jnp.int32)[:, None, None]
    fc, fs = cs(odd[None, :, None] * m, 2 * seq)
    f = jnp.concatenate([fc, fs], axis=1).astype(BF16)
    sc, ss = cs(odd, 4)
    sh = jnp.broadcast_to(jnp.stack([sc, ss], axis=0)[..., None], (2, lp, LANES))
    return f, jnp.transpose(f, (0, 2, 1)), sh


def position_features(seq):
    t = jnp.arange(seq, dtype=F32)
    tn = t / seq
    bands = jnp.arange(1, HY_BANDS + 1, dtype=F32)
    ang = 2.0 * jnp.pi * tn[:, None] * bands[None, :]
    z = jnp.concatenate([tn[:, None], jnp.cos(ang), jnp.sin(ang)], axis=-1)
    return jnp.pad(z, ((0, 0), (0, LANES - z.shape[1])))


def rope_tables(seq):
    n_freq = GLA_DK // 4
    pos = jnp.arange(seq)
    row = (pos // GRID_W).astype(F32)
    col = (pos % GRID_W).astype(F32)
    inv = ROPE_BASE ** (-jnp.arange(n_freq, dtype=F32) / n_freq)
    ang = jnp.concatenate([row[:, None] * inv, col[:, None] * inv], axis=-1)
    cos, sin = jnp.cos(ang), jnp.sin(ang)
    return jnp.concatenate([cos, cos], axis=-1), jnp.concatenate([-sin, sin], axis=-1)


def _merge_kernel(of_ref, ob_ref, gg_ref, gn_ref, nb_ref, hc_ref, ga_ref, gb_ref, gc_ref,
                  wa_ref, wb_ref, wc_ref, o_ref, a_s):
    @pl.when(pl.program_id(1) == 0)
    def _():
        for h in range(GLA_HEADS):
            cols = slice(h * GLA_DV, (h + 1) * GLA_DV)
            oa = of_ref[:, cols] + ob_ref[:, cols]
            oa = _head_norm(oa, gn_ref[...])
            gg = gg_ref[:, cols]
            a_s[:, cols] = (oa * (gg * jax.nn.sigmoid(gg))).astype(BF16)

    ya = jnp.dot(a_s[...], wa_ref[...], preferred_element_type=F32)
    yb = jnp.dot(nb_ref[...].astype(BF16), wb_ref[...], preferred_element_type=F32)
    yc = jnp.dot(hc_ref[...], wc_ref[...], preferred_element_type=F32)
    merged = (jax.nn.sigmoid(ga_ref[...]) * ya + jax.nn.sigmoid(gb_ref[...]) * yb
              + jax.nn.sigmoid(gc_ref[...]) * yc)
    o_ref[...] = merged.astype(o_ref.dtype)


def branch_merge(o_f, o_b, proj, gnorm, o_attn, o_hy, w_a, w_b, w_c, *, row0_blocks, tm, tn):
    t = o_f.shape[0]
    d = w_a.shape[1]
    assert OFF_GATES % tn == 0 and d % tn == 0
    act = lambda w: pl.BlockSpec((tm, w), lambda i, j: (i, 0))
    gate = lambda g: pl.BlockSpec((tm, tn), lambda i, j: (row0_blocks + i, (OFF_GATES + g * d) // tn + j))
    wspec = lambda w: pl.BlockSpec((w.shape[0], tn), lambda i, j: (0, j))
    return pl.pallas_call(
        _merge_kernel,
        grid=(t // tm, d // tn),
        in_specs=[act(VAL_W), act(VAL_W),
                  pl.BlockSpec((tm, VAL_W), lambda i, j: (row0_blocks + i, OFF_GG // VAL_W)),
                  pl.BlockSpec((1, GLA_DV), lambda i, j: (0, 0)),
                  act(NA_W), act(HY_W), gate(0), gate(1), gate(2),
                  wspec(w_a), wspec(w_b), wspec(w_c)],
        out_specs=pl.BlockSpec((tm, tn), lambda i, j: (i, j)),
        out_shape=jax.ShapeDtypeStruct((t, d), BF16),
        scratch_shapes=[pltpu.VMEM((tm, VAL_W), BF16)],
        compiler_params=_cparams("parallel", "arbitrary"),
        name="branch_merge",
    )(o_f, o_b, proj, gnorm, o_attn, o_hy, proj, proj, proj, w_a, w_b, w_c)


def _matmul_residual_kernel(a_ref, w_ref, g_ref, r_ref, o_ref):
    y = jnp.dot(a_ref[...], w_ref[...], preferred_element_type=F32)
    o_ref[...] = r_ref[...] + g_ref[0] * y


def matmul_residual(a, w, gate, resid, *, rows_per_mod, tm, tn):
    t, k = a.shape
    n = w.shape[1]
    return pl.pallas_call(
        _matmul_residual_kernel,
        grid=(t // tm, n // tn),
        in_specs=[pl.BlockSpec((tm, k), lambda i, j: (i, 0)),
                  pl.BlockSpec((k, tn), lambda i, j: (0, j)),
                  pl.BlockSpec((1, 1, tn), lambda i, j: ((i * tm) // rows_per_mod, 0, j)),
                  pl.BlockSpec((tm, tn), lambda i, j: (i, j))],
        out_specs=pl.BlockSpec((tm, tn), lambda i, j: (i, j)),
        out_shape=jax.ShapeDtypeStruct((t, n), F32),
        compiler_params=_cparams("parallel", "arbitrary"),
        name="matmul_residual",
    )(a, w, gate, resid)


def _trunk_layer(x, mods, p, consts, *, batch, seq, layer, latent):
    sh1, sc1, g1, sh2, sc2, g2 = mods
    t = x.shape[0]
    rows_per_mod = t // sh1.shape[0]
    tm = 512

    proj, la = norm_matmul(x, p["norm1_g"], sc1, sh1, p["w_in_main"], rows_per_mod=rows_per_mod,
                           tm=tm, tn=1536, relu2=False, out_dtype=F32,
                           decay_weights=(p["w_in_ga"], p["wa2_bd"], p["ba2"]))

    rows_blk = min(seq, 512)
    rope = consts["rope"] if latent else None
    s0 = consts["state_gla"] if latent else None
    o_f, s_f = gla_direction(proj, la, rope, s0, batch=batch, seq=seq, reverse=False, layer=layer,
                             direction=0, rows_blk=rows_blk)
    o_b, s_b = gla_direction(proj, la, rope, s0, batch=batch, seq=seq, reverse=True, layer=layer,
                             direction=1, rows_blk=rows_blk)

    if latent:
        o_attn = neighbourhood_attention(proj, consts["cache_k"], consts["cache_v"], p["na_qnorm_g"],
                                         p["na_knorm_g"], p["na_table"], batch=batch, seq=seq,
                                         layer=layer, row0=0)
        new_ctx = None
    else:
        o_attn, kn, vn = ctx_attention(proj, p["na_qnorm_g"], p["na_knorm_g"], batch=batch, seq=seq)
        new_ctx = (jnp.stack([s_f, s_b], axis=1), kn, vn)

    key = "lat" if latent else "ctx"
    tabs = consts["dft_" + key]
    g_spec = p["spectrum_" + key]
    z0 = hyena_pre(proj, p["hy_conv_w"], p["hy_conv_b"], batch=batch, seq=seq, row0=0)
    cv0 = hyena_long_conv(z0, g_spec, tabs, batch=batch, seq=seq, order=0)
    z1, z1b = hyena_gate(proj, p["hy_conv_w"], p["hy_conv_b"], cv0, p["hy_bias"], batch=batch, seq=seq, row0=0)
    cv1 = hyena_long_conv(z1b, g_spec, tabs, batch=batch, seq=seq, order=1)
    o_hy = hyena_final(proj, p["hy_conv_w"], p["hy_conv_b"], cv1, p["hy_bias"], z1, batch=batch, seq=seq, row0=0)

    merged = branch_merge(o_f, o_b, proj, p["gla_norm_g"], o_attn, o_hy, p["w_br_a"], p["w_br_b"],
                          p["w_br_c"], row0_blocks=0, tm=256, tn=2048)
    x = matmul_residual(merged, p["w_out"], g1, x, rows_per_mod=rows_per_mod, tm=tm, tn=2048)

    u = norm_matmul(x, p["norm2_g"], sc2, sh2, p["w_mlp1"], rows_per_mod=rows_per_mod,
                    tm=tm, tn=2048, relu2=True, out_dtype=BF16)
    x = matmul_residual(u, p["w_mlp2"], g2, x, rows_per_mod=rows_per_mod, tm=tm, tn=512)
    return x, new_ctx


def _reorder_w_in(w_in):
    o_ga = 2 * KEY_W + 2 * VAL_W
    n_ga = 2 * GLA_LOWRANK
    o_hy = o_ga + n_ga + 3 * NA_W
    o_gates = o_hy + (HY_ORDER + 1) * HY_W
    main = jnp.concatenate([w_in[..., :o_ga], w_in[..., o_ga + n_ga:o_hy], w_in[..., o_gates:],
                            w_in[..., o_hy:o_gates]], axis=-1).astype(BF16)
    ga = jnp.pad(w_in[..., o_ga:o_ga + n_ga], ((0, 0), (0, LANES - n_ga))).astype(BF16)
    return main, ga


def kernel(x_prompt, x_sample, state_gla, cache_na_k, cache_na_v, c, c_ctx, w_mod, b_mod, norm1_g, norm2_g, w_in, gla_wa2, gla_ba2, gla_norm_g, na_qnorm_g, na_knorm_g, na_rpb, hy_conv_w, hy_conv_b, hy_f1_w, hy_f1_b, hy_f2_w, hy_f2_b, hy_f3_w, hy_freq, hy_bias, w_br_a, w_br_b, w_br_c, w_out, w_mlp1, w_mlp2):
    batch, seq, d = x_prompt.shape
    dec_batch, dec_seq, _ = x_sample.shape
    depth = w_mod.shape[0]

    bf = lambda w: w.astype(BF16)
    wa2_bd = jnp.zeros((depth, LANES, 2 * KEY_W), F32)
    wa2_bd = wa2_bd.at[:, :GLA_LOWRANK, :KEY_W].set(gla_wa2[:, 0])
    wa2_bd = wa2_bd.at[:, GLA_LOWRANK:2 * GLA_LOWRANK, KEY_W:].set(gla_wa2[:, 1])
    ba2 = gla_ba2.reshape(depth, 1, 2 * KEY_W)
    f1w = jnp.pad(hy_f1_w, ((0, 0), (0, LANES - hy_f1_w.shape[1]), (0, 0)))

    consts = {
        "rope": rope_tables(dec_seq),
        "state_gla": state_gla, "cache_k": cache_na_k, "cache_v": cache_na_v,
        "dft_ctx": dft_tables(seq), "dft_lat": dft_tables(dec_seq),
    }
    deltas = jnp.abs(jnp.linspace(math.log(HY_DECAY_TARGET) / HY_DECAY_PCT_LONG,
                                  math.log(HY_DECAY_TARGET) / HY_DECAY_PCT_SHORT, HY_W, dtype=F32))
    deltas = jnp.tile(deltas, HY_ORDER)[None, :]
    zpos = {"ctx": position_features(seq), "lat": position_features(dec_seq)}

    cond = jnp.zeros((16, d), F32).at[0].set(c_ctx).at[1:1 + dec_batch].set(c)

    xp = x_prompt.reshape(batch * seq, d)
    xs = x_sample.reshape(dec_batch * dec_seq, d)
    new_gla, new_k, new_v = [], [], []
    for l in range(depth):
        m = adaln(cond, w_mod, b_mod[l][None, :], l)
        mods_ctx = tuple(m[0:1, i * d:(i + 1) * d][:, None, :] for i in range(N_MOD))
        mods_lat = tuple(m[1:1 + dec_batch, i * d:(i + 1) * d][:, None, :] for i in range(N_MOD))
        w_in_main, w_in_ga = _reorder_w_in(w_in[l])
        p = dict(norm1_g=norm1_g[l][None], norm2_g=norm2_g[l][None], w_in_main=w_in_main, w_in_ga=w_in_ga,
                 wa2_bd=bf(wa2_bd[l]), ba2=ba2[l], gla_norm_g=gla_norm_g[l][None], na_qnorm_g=na_qnorm_g[l][None],
                 na_knorm_g=na_knorm_g[l][None], na_table=na_bias_table(na_rpb[l], dec_seq // GRID_W),
                 hy_conv_w=hy_conv_w[l], hy_conv_b=hy_conv_b[l][None], hy_bias=hy_bias[l],
                 w_br_a=bf(w_br_a[l]), w_br_b=bf(w_br_b[l]), w_br_c=bf(w_br_c[l]), w_out=bf(w_out[l]),
                 w_mlp1=bf(w_mlp1[l]), w_mlp2=bf(w_mlp2[l]))
        for key, length in (("ctx", seq), ("lat", dec_seq)):
            filt = hyena_filters(zpos[key], f1w[l], hy_f1_b[l][None], hy_f2_w[l], hy_f2_b[l][None],
                                 hy_f3_w[l], hy_freq[l][None], deltas, seq=length)
            p["spectrum_" + key] = filter_spectrum(filt, consts["dft_" + key], seq=length)
        xp, (s_gla, k_ctx, v_ctx) = _trunk_layer(xp, mods_ctx, p, consts, batch=batch, seq=seq, layer=l, latent=False)
        new_gla.append(s_gla)
        new_k.append(k_ctx)
        new_v.append(v_ctx)
        xs, _ = _trunk_layer(xs, mods_lat, p, consts, batch=dec_batch, seq=dec_seq, layer=l, latent=True)
    return (xp.reshape(batch, seq, d), xs.reshape(dec_batch, dec_seq, d),
            jnp.stack(new_gla, axis=1), jnp.stack(new_k, axis=1), jnp.stack(new_v, axis=1))
```

```python
import functools
import math

import jax
import jax.numpy as jnp
import numpy as np
from jax import lax
from jax.experimental import pallas as pl
from jax.experimental.pallas import tpu as pltpu

F32 = jnp.float32
BF16 = jnp.bfloat16
HIGHEST = lax.Precision.HIGHEST

GRID_W = 64
N_MOD = 6
NORM_EPS = 1e-6
GLA_HEADS = 4
GLA_DK = 128
GLA_DV = 256
GLA_LOWRANK = 16
GLA_TAU = 16.0
GLA_CHUNK = 64
GLA_SUB = 16
GLA_HEADS_PER_STEP = 2
GLA_SAFE_EXP = 60.0
ROPE_BASE = 10000.0
NA_HEADS = 8
NA_HD = 128
NA_KH = 8
NA_KW = 16
NA_QROWS = 4
NA_UNION = NA_KH + NA_QROWS
NA_BLOCK_UNROLL = 2
HY_W = 1024
HY_ORDER = 2
HY_BANDS = 16
HY_HIDDEN = 64
HY_DECAY_TARGET = 1e-2
HY_DECAY_PCT_SHORT = 0.3
HY_DECAY_PCT_LONG = 1.5
HY_RADIX = 8
HY_RADIX_MIN_SEQ = 2048
HY_KC = 256
HY_TC_FFT = 256

V7X_VMEM_LIMIT_BYTES = 56 * 1024 * 1024
LANES = 128

KEY_W = GLA_HEADS * GLA_DK
VAL_W = GLA_HEADS * GLA_DV
NA_W = NA_HEADS * NA_HD
OFF_GQ = 0
OFF_GK = OFF_GQ + KEY_W
OFF_GV = OFF_GK + KEY_W
OFF_GG = OFF_GV + VAL_W
OFF_NQ = OFF_GG + VAL_W
OFF_NK = OFF_NQ + NA_W
OFF_NV = OFF_NK + NA_W
OFF_GATES = OFF_NV + NA_W
OFF_HY = OFF_GATES + 3 * 2048


def _cparams(*sem):
    return pltpu.CompilerParams(dimension_semantics=sem, vmem_limit_bytes=V7X_VMEM_LIMIT_BYTES)


def _modulated_norm(x, g, scale, shift):
    ms = jnp.mean(x * x, axis=-1, keepdims=True)
    y = x * lax.rsqrt(ms + NORM_EPS) * g
    return y * (1.0 + scale) + shift


def _adaln_kernel(c_ref, w_ref, b_ref, o_ref):
    c = c_ref[...]
    a = (c * jax.nn.sigmoid(c)).astype(BF16)
    o_ref[...] = jnp.dot(a, w_ref[...].astype(BF16), preferred_element_type=F32) + b_ref[...]


def adaln(cond, w_mod, b_mod, layer):
    m, d = cond.shape
    n = w_mod.shape[2]
    tn = 1024
    return pl.pallas_call(
        _adaln_kernel,
        grid=(n // tn,),
        in_specs=[pl.BlockSpec((m, d), lambda j: (0, 0)),
                  pl.BlockSpec((None, d, tn), lambda j: (layer, 0, j)),
                  pl.BlockSpec((1, tn), lambda j: (0, j))],
        out_specs=pl.BlockSpec((m, tn), lambda j: (0, j)),
        out_shape=jax.ShapeDtypeStruct((m, n), F32),
        compiler_params=_cparams("parallel"),
        name="adaln",
    )(cond, w_mod, b_mod)


def _log_decay(h, wga_ref, wa2_ref, ba2_ref):
    ga = jnp.dot(h, wga_ref[...], preferred_element_type=F32)
    pre = jnp.dot(ga.astype(BF16), wa2_ref[...], preferred_element_type=F32) + ba2_ref[...]
    log_sig = jnp.minimum(pre, 0.0) - jnp.log(1.0 + jnp.exp(-jnp.abs(pre)))
    return log_sig * (1.0 / GLA_TAU)


def _norm_matmul_kernel(*refs, relu2, decay):
    x_ref, g_ref, sc_ref, sh_ref, w_ref = refs[:5]
    if decay:
        wga_ref, wa2_ref, ba2_ref, o_ref, la_ref, h_ref = refs[5:]
    else:
        o_ref, h_ref = refs[5:]

    @pl.when(pl.program_id(1) == 0)
    def _():
        h = _modulated_norm(x_ref[...], g_ref[...], sc_ref[0], sh_ref[0]).astype(BF16)
        h_ref[...] = h
        if decay:
            la_ref[...] = _log_decay(h, wga_ref, wa2_ref, ba2_ref)

    acc = jnp.dot(h_ref[...], w_ref[...], preferred_element_type=F32)
    if relu2:
        acc = jnp.square(jnp.maximum(acc, 0.0))
    o_ref[...] = acc.astype(o_ref.dtype)


def norm_matmul(x, g, scale, shift, w, *, rows_per_mod, tm, tn, relu2, out_dtype, decay_weights=None):
    t, d = x.shape
    n = w.shape[1]
    mod_map = lambda i, j: ((i * tm) // rows_per_mod, 0, 0)
    decay = decay_weights is not None
    in_specs = [pl.BlockSpec((tm, d), lambda i, j: (i, 0)),
                pl.BlockSpec((1, d), lambda i, j: (0, 0)),
                pl.BlockSpec((1, 1, d), mod_map),
                pl.BlockSpec((1, 1, d), mod_map),
                pl.BlockSpec((d, tn), lambda i, j: (0, j))]
    out_specs = pl.BlockSpec((tm, tn), lambda i, j: (i, j))
    out_shape = jax.ShapeDtypeStruct((t, n), out_dtype)
    args = [x, g, scale, shift, w]
    if decay:
        in_specs += [pl.BlockSpec(a.shape, lambda i, j: (0, 0)) for a in decay_weights]
        n_la = decay_weights[1].shape[1]
        out_specs = [out_specs, pl.BlockSpec((tm, n_la), lambda i, j: (i, 0))]
        out_shape = [out_shape, jax.ShapeDtypeStruct((t, n_la), F32)]
        args += list(decay_weights)
    return pl.pallas_call(
        functools.partial(_norm_matmul_kernel, relu2=relu2, decay=decay),
        grid=(t // tm, n // tn),
        in_specs=in_specs, out_specs=out_specs, out_shape=out_shape,
        scratch_shapes=[pltpu.VMEM((tm, d), BF16)],
        compiler_params=_cparams("parallel", "arbitrary"),
        name="norm_matmul_relu2" if relu2 else "norm_matmul",
    )(*args)


def _cumulative_log_decay(la, reverse):
    c = la.shape[0]
    row = lax.broadcasted_iota(jnp.int32, (c, c), 0)
    col = lax.broadcasted_iota(jnp.int32, (c, c), 1)
    tri = jnp.where((row <= col) if reverse else (row >= col), 1.0, 0.0).astype(BF16)
    hi = la.astype(BF16)
    rest = la - hi.astype(F32)
    mid = rest.astype(BF16)
    lo = (rest - mid.astype(F32)).astype(BF16)
    dot = lambda t: jnp.dot(tri, t, preferred_element_type=F32)
    return dot(hi) + dot(mid) + dot(lo)


def _gla_chunk_scores(q, k, v, b, *, reverse, exact):
    c = GLA_CHUNK
    n = GLA_SUB
    edge = b[0:1, :] if reverse else b[c - 1:c, :]
    qe = (q * jnp.exp(b)).astype(BF16)
    kd = (k * jnp.exp(edge - b)).astype(BF16)
    m = jnp.dot(v.T.astype(BF16), kd, preferred_element_type=F32)
    decay = jnp.exp(edge)

    lane = lax.broadcasted_iota(jnp.int32, (n, c), 1)
    sub_row = lax.broadcasted_iota(jnp.int32, (n, c), 0)
    blocks = []
    cap = 0.0 if exact else GLA_SAFE_EXP
    for i in range(c // n):
        lo, hi = i * n, (i + 1) * n
        q_i = q[lo:hi, :]
        b_i = b[lo:hi, :]
        has_border = (hi < c) if reverse else (lo > 0)
        if has_border:
            r = b[hi:hi + 1, :] if reverse else b[lo - 1:lo, :]
            lhs = (q_i * jnp.exp(b_i - r)).astype(BF16)
            rhs = (k * jnp.exp(jnp.minimum(r - b, cap))).astype(BF16)
        else:
            lhs = (q_i * jnp.exp(b_i)).astype(BF16)
            rhs = (k * jnp.exp(jnp.minimum(-b, cap))).astype(BF16)
        a_i = lax.dot_general(lhs, rhs, (((1,), (1,)), ((), ())), preferred_element_type=F32)
        if exact:
            a_diag = jnp.zeros((n, c), F32)
            for j in range(n):
                s = lo + j
                p = q_i * k[s:s + 1, :] * jnp.exp(b_i - b[s:s + 1, :])
                a_diag = jnp.where(lane == s, jnp.sum(p, axis=1, keepdims=True), a_diag)
            in_block = (lane >= lo) & (lane < hi)
            a_i = jnp.where(in_block, a_diag, a_i)
        causal = (lane >= sub_row + lo) if reverse else (lane <= sub_row + lo)
        blocks.append(jnp.where(causal, a_i, 0.0))
    a = jnp.concatenate(blocks, axis=0).astype(BF16)
    return qe, a, m, decay


def _gla_inter(qe, st):
    return lax.dot_general(qe, st.astype(BF16), (((1,), (1,)), ((), ())), preferred_element_type=F32)


def _gla_kernel(*refs, reverse, rope, has_s0, nchunk, heads):
    it = iter(refs)
    q_ref, k_ref, v_ref, la_ref = next(it), next(it), next(it), next(it)
    cos_ref = sin_ref = s0_ref = None
    if rope:
        cos_ref, sin_ref = next(it), next(it)
    if has_s0:
        s0_ref = next(it)
    o_ref, sfin_ref, st_ref = next(it), next(it), next(it)

    blk = pl.program_id(2)

    @pl.when(blk == 0)
    def _():
        for hh in range(heads):
            st_ref[hh] = s0_ref[0, 0, 0, hh].T if has_s0 else jnp.zeros(st_ref.shape[1:], F32)

    kcols = lambda hh: slice(hh * GLA_DK, (hh + 1) * GLA_DK)
    vcols = lambda hh: slice(hh * GLA_DV, (hh + 1) * GLA_DV)

    def load_qk(rows, hh):
        q = q_ref[rows, kcols(hh)] * (GLA_DK ** -0.5)
        k = k_ref[rows, kcols(hh)]
        if rope:
            cs, sn = cos_ref[rows, :], sin_ref[rows, :]
            q = q * cs + pltpu.roll(q, GLA_DK // 2, 1) * sn
            k = k * cs + pltpu.roll(k, GLA_DK // 2, 1) * sn
        return q, k

    def intra(a, rows, hh):
        return jnp.dot(a, v_ref[rows, vcols(hh)].astype(BF16), preferred_element_type=F32)

    def run_exact():
        def body(step, carry):
            ci = (nchunk - 1 - step) if reverse else step
            rows = pl.ds(pl.multiple_of(ci * GLA_CHUNK, GLA_CHUNK), GLA_CHUNK)
            for hh in range(heads):
                q, k = load_qk(rows, hh)
                b = _cumulative_log_decay(la_ref[rows, kcols(hh)], reverse)
                qe, a, m, decay = _gla_chunk_scores(q, k, v_ref[rows, vcols(hh)], b, reverse=reverse, exact=True)
                o_ref[rows, vcols(hh)] = _gla_inter(qe, st_ref[hh]) + intra(a, rows, hh)
                st_ref[hh] = st_ref[hh] * decay + m
            return carry

        lax.fori_loop(0, nchunk, body, 0)

    def run_fast():
        order = list(range(nchunk))[::-1] if reverse else list(range(nchunk))
        rows = [pl.ds(ci * GLA_CHUNK, GLA_CHUNK) for ci in range(nchunk)]
        units = [(hh, ci) for hh in range(heads) for ci in range(nchunk)]
        bs = {u: _cumulative_log_decay(la_ref[rows[u[1]], kcols(u[0])], reverse) for u in units}
        scores = {u: _gla_chunk_scores(*load_qk(rows[u[1]], u[0]), v_ref[rows[u[1]], vcols(u[0])], bs[u],
                                       reverse=reverse, exact=False) for u in units}
        o_intra = {u: intra(scores[u][1], rows[u[1]], u[0]) for u in units}
        states = {}
        for hh in range(heads):
            st = st_ref[hh]
            for ci in order:
                states[hh, ci] = st
                st = st * scores[hh, ci][3] + scores[hh, ci][2]
            st_ref[hh] = st
        for hh, ci in units:
            o_ref[rows[ci], vcols(hh)] = _gla_inter(scores[hh, ci][0], states[hh, ci]) + o_intra[hh, ci]

    steep = jnp.min(la_ref[...]) < -(GLA_SAFE_EXP / GLA_SUB)
    pl.when(steep)(run_exact)
    pl.when(jnp.logical_not(steep))(run_fast)

    @pl.when(blk == pl.num_programs(2) - 1)
    def _():
        for hh in range(heads):
            sfin_ref[0, hh] = st_ref[hh].T


def gla_direction(proj, la, rope_tabs, s0, *, batch, seq, reverse, layer, direction, rows_blk):
    t = proj.shape[0]
    nblk = seq // rows_blk
    nchunk = rows_blk // GLA_CHUNK
    rope = rope_tabs is not None
    has_s0 = s0 is not None

    def rb(b, c):
        cc = (nblk - 1 - c) if reverse else c
        return b * nblk + cc

    hp = GLA_HEADS_PER_STEP
    kw, vw = hp * GLA_DK, hp * GLA_DV
    assert OFF_GQ % kw == 0 and OFF_GK % kw == 0 and OFF_GV % vw == 0 and KEY_W % kw == 0
    kq, kk, kv = OFF_GQ // kw, OFF_GK // kw, OFF_GV // vw
    kla = direction * (KEY_W // kw)
    in_specs = [pl.BlockSpec((rows_blk, kw), lambda b, h, c: (rb(b, c), kq + h)),
                pl.BlockSpec((rows_blk, kw), lambda b, h, c: (rb(b, c), kk + h)),
                pl.BlockSpec((rows_blk, vw), lambda b, h, c: (rb(b, c), kv + h)),
                pl.BlockSpec((rows_blk, kw), lambda b, h, c: (rb(b, c), kla + h))]
    args = [proj, proj, proj, la]
    if rope:
        tab_map = lambda b, h, c: ((nblk - 1 - c) if reverse else c, 0)
        in_specs += [pl.BlockSpec((rows_blk, GLA_DK), tab_map)] * 2
        args += list(rope_tabs)
    if has_s0:
        in_specs.append(pl.BlockSpec((1, 1, 1, hp, GLA_DK, GLA_DV), lambda b, h, c: (b, layer, direction, h, 0, 0)))
        args.append(s0)
    o, sfin = pl.pallas_call(
        functools.partial(_gla_kernel, reverse=reverse, rope=rope, has_s0=has_s0, nchunk=nchunk, heads=hp),
        grid=(batch, GLA_HEADS // hp, nblk),
        in_specs=in_specs,
        out_specs=[pl.BlockSpec((rows_blk, vw), lambda b, h, c: (rb(b, c), h)),
                   pl.BlockSpec((1, hp, GLA_DK, GLA_DV), lambda b, h, c: (b, h, 0, 0))],
        out_shape=[jax.ShapeDtypeStruct((t, VAL_W), F32),
                   jax.ShapeDtypeStruct((batch, GLA_HEADS, GLA_DK, GLA_DV), F32)],
        scratch_shapes=[pltpu.VMEM((hp, GLA_DV, GLA_DK), F32)],
        compiler_params=_cparams("parallel", "parallel", "arbitrary"),
        name="gla_bwd" if reverse else "gla_fwd",
    )(*args)
    return o, sfin


def _head_norm(x, g):
    return x * lax.rsqrt(jnp.mean(x * x, axis=-1, keepdims=True) + NORM_EPS) * g


def _head_norm_mxu(x, g):
    n = x.shape[1]
    mean_sq = jnp.dot((x * x).astype(BF16), jnp.full((n, n), 1.0 / n, BF16), preferred_element_type=F32)
    return x * lax.rsqrt(mean_sq + NORM_EPS) * g


def _ctx_attn_kernel(q_ref, k_ref, v_ref, gq_ref, gk_ref, o_ref, kn_ref, vn_ref):
    scale = NA_HD ** -0.5
    for h in range(NA_HEADS):
        cols = slice(h * NA_HD, (h + 1) * NA_HD)
        qn = _head_norm(q_ref[:, cols], gq_ref[...]) * scale
        kn = _head_norm(k_ref[:, cols], gk_ref[...])
        v = v_ref[:, cols]
        kn_ref[0, h] = kn
        vn_ref[0, h] = v
        s = lax.dot_general(qn.astype(BF16), kn.astype(BF16), (((1,), (1,)), ((), ())),
                            preferred_element_type=F32)
        e = jnp.exp(s - jnp.max(s, axis=-1, keepdims=True))
        p = e / jnp.sum(e, axis=-1, keepdims=True)
        o_ref[:, cols] = jnp.dot(p.astype(BF16), v.astype(BF16), preferred_element_type=F32)


def ctx_attention(proj, gq, gk, *, batch, seq):
    t = proj.shape[0]
    cq, ck, cv = OFF_NQ // NA_W, OFF_NK // NA_W, OFF_NV // NA_W
    cache_shape = jax.ShapeDtypeStruct((batch, NA_HEADS, seq, NA_HD), F32)
    cache_spec = pl.BlockSpec((1, NA_HEADS, seq, NA_HD), lambda b: (b, 0, 0, 0))
    return pl.pallas_call(
        _ctx_attn_kernel,
        grid=(batch,),
        in_specs=[pl.BlockSpec((seq, NA_W), lambda b: (b, cq)),
                  pl.BlockSpec((seq, NA_W), lambda b: (b, ck)),
                  pl.BlockSpec((seq, NA_W), lambda b: (b, cv)),
                  pl.BlockSpec((1, NA_HD), lambda b: (0, 0)),
                  pl.BlockSpec((1, NA_HD), lambda b: (0, 0))],
        out_specs=[pl.BlockSpec((seq, NA_W), lambda b: (b, 0)), cache_spec, cache_spec],
        out_shape=[jax.ShapeDtypeStruct((t, NA_W), F32), cache_shape, cache_shape],
        compiler_params=_cparams("parallel"),
        name="ctx_attention",
    )(proj, proj, proj, gq, gk)


def _na_kernel(q_ref, k_ref, v_ref, kc_ref, vc_ref, gq_ref, gk_ref, tab_ref, o_ref, qn_s, kn_s, v_s, *, rows):
    scale = NA_HD ** -0.5
    qn_s[...] = (_head_norm_mxu(q_ref[...], gq_ref[...]) * scale).astype(BF16)
    kn_s[...] = _head_norm_mxu(k_ref[...], gk_ref[...]).astype(BF16)
    v_s[...] = v_ref[...].astype(BF16)
    kctx = kc_ref[0, 0, 0].astype(BF16)
    vctx = vc_ref[0, 0, 0].astype(BF16)
    nblk = rows // NA_QROWS
    qlen, win = NA_QROWS * GRID_W, NA_UNION * GRID_W

    def scores(g):
        u0 = jnp.clip(g * NA_QROWS - NA_KH // 2, 0, rows - NA_UNION)
        cls = jnp.where(g == 0, 0, jnp.where(g == nblk - 1, 2, 1))
        qrows = pl.ds(pl.multiple_of(g * qlen, qlen), qlen)
        wrows = pl.ds(pl.multiple_of(u0 * GRID_W, GRID_W), win)
        qn = qn_s[qrows, :]
        s_loc = lax.dot_general(qn, kn_s[wrows, :], (((1,), (1,)), ((), ())), preferred_element_type=F32)
        s_ctx = lax.dot_general(qn, kctx, (((1,), (1,)), ((), ())), preferred_element_type=F32)
        return qrows, wrows, s_loc + tab_ref[0, cls], s_ctx

    def softmax(s_loc, s_ctx):
        m = jnp.maximum(jnp.max(s_loc, axis=-1, keepdims=True), jnp.max(s_ctx, axis=-1, keepdims=True))
        e_loc = jnp.exp(s_loc - m)
        e_ctx = jnp.exp(s_ctx - m)
        den = jnp.sum(e_loc, axis=-1, keepdims=True) + jnp.sum(e_ctx, axis=-1, keepdims=True)
        return e_loc.astype(BF16), e_ctx.astype(BF16), den

    def body(step, carry):
        blocks = [scores(step * NA_BLOCK_UNROLL + u) for u in range(NA_BLOCK_UNROLL)]
        probs = [softmax(s_loc, s_ctx) for _, _, s_loc, s_ctx in blocks]
        for (qrows, wrows, _, _), (e_loc, e_ctx, den) in zip(blocks, probs):
            acc = jnp.dot(e_loc, v_s[wrows, :], preferred_element_type=F32)
            acc = acc + jnp.dot(e_ctx, vctx, preferred_element_type=F32)
            o_ref[qrows, :] = acc / den
        return carry

    lax.fori_loop(0, nblk // NA_BLOCK_UNROLL, body, 0)


def neighbourhood_attention(proj, cache_k, cache_v, gq, gk, table, *, batch, seq, layer, row0):
    rows = seq // GRID_W
    assert rows % NA_QROWS == 0 and rows // NA_QROWS >= 3 and rows >= NA_UNION
    past = cache_k.shape[3]
    cq, ck, cv = OFF_NQ // NA_HD, OFF_NK // NA_HD, OFF_NV // NA_HD
    cache_spec = pl.BlockSpec((1, 1, 1, past, NA_HD), lambda b, h: (b, layer, h, 0, 0))
    return pl.pallas_call(
        functools.partial(_na_kernel, rows=rows),
        grid=(batch, NA_HEADS),
        in_specs=[pl.BlockSpec((seq, NA_HD), lambda b, h: (row0 + b, cq + h)),
                  pl.BlockSpec((seq, NA_HD), lambda b, h: (row0 + b, ck + h)),
                  pl.BlockSpec((seq, NA_HD), lambda b, h: (row0 + b, cv + h)),
                  cache_spec, cache_spec,
                  pl.BlockSpec((1, NA_HD), lambda b, h: (0, 0)),
                  pl.BlockSpec((1, NA_HD), lambda b, h: (0, 0)),
                  pl.BlockSpec((1, 3, NA_QROWS * GRID_W, NA_UNION * GRID_W), lambda b, h: (h, 0, 0, 0))],
        out_specs=pl.BlockSpec((seq, NA_HD), lambda b, h: (b, h)),
        out_shape=jax.ShapeDtypeStruct((batch * seq, NA_W), F32),
        scratch_shapes=[pltpu.VMEM((seq, NA_HD), BF16)] * 3,
        compiler_params=_cparams("parallel", "parallel"),
        name="neighbourhood_attention",
    )(proj, proj, proj, cache_k, cache_v, gq, gk, table)


def na_bias_table(rpb, rows):
    nblk = rows // NA_QROWS
    col = np.arange(GRID_W)
    c_start = np.clip(col - NA_KW // 2, 0, GRID_W - NA_KW)
    col_in = (col[None, :] >= c_start[:, None]) & (col[None, :] < c_start[:, None] + NA_KW)
    col_idx = np.clip(col[None, :] - col[:, None] + NA_KW - 1, 0, 2 * NA_KW - 2)
    row_off, row_in = [], []
    for g in (0, 1, nblk - 1):
        u0 = np.clip(g * NA_QROWS - NA_KH // 2, 0, rows - NA_UNION)
        r = g * NA_QROWS + np.arange(NA_QROWS)[:, None]
        r_start = np.clip(r - NA_KH // 2, 0, rows - NA_KH)
        key_row = u0 + np.arange(NA_UNION)[None, :]
        row_in.append((key_row >= r_start) & (key_row < r_start + NA_KH))
        row_off.append(np.clip(key_row - r + NA_KH - 1, 0, 2 * NA_KH - 2))
    row_off, row_in = np.stack(row_off), np.stack(row_in)
    n_off = 2 * NA_KH - 1
    tiles = jnp.where(col_in[None, None], rpb[:, :, col_idx], -jnp.inf)
    tiles = jnp.concatenate([tiles, jnp.full_like(tiles[:, :1], -jnp.inf)], axis=1)
    tab = tiles[:, np.where(row_in, row_off, n_off)]
    tab = jnp.transpose(tab, (0, 1, 2, 4, 3, 5))
    return tab.reshape(rpb.shape[0], 3, NA_QROWS * GRID_W, NA_UNION * GRID_W).astype(F32)


def _filter_kernel(z_ref, f1w_ref, f1b_ref, f2w_ref, f2b_ref, f3w_ref, fr_ref, dl_ref, o_ref, *, seq):
    fr = fr_ref[...]
    h = jnp.sin(fr * (jnp.dot(z_ref[...], f1w_ref[...], preferred_element_type=F32, precision=HIGHEST)
                      + f1b_ref[...]))
    h = jnp.sin(fr * (jnp.dot(h, f2w_ref[...], preferred_element_type=F32, precision=HIGHEST)
                      + f2b_ref[...]))
    h = jnp.dot(h, f3w_ref[...], preferred_element_type=F32, precision=HIGHEST)
    t = lax.broadcasted_iota(jnp.int32, (seq, 1), 0).astype(F32)
    dist = jnp.abs(t - float(seq // 2)) * (2.0 / seq)
    h = h * jnp.exp(-dist * dl_ref[...])
    o_ref[...] = h * lax.rsqrt(jnp.sum(h * h, axis=0, keepdims=True) + NORM_EPS)


def hyena_filters(zpos, f1w, f1b, f2w, f2b, f3w, freq, deltas, *, seq):
    n = f3w.shape[1]
    tc = 512
    full = lambda a: pl.BlockSpec(a.shape, lambda j: (0,) * a.ndim)
    return pl.pallas_call(
        functools.partial(_filter_kernel, seq=seq),
        grid=(n // tc,),
        in_specs=[full(zpos), full(f1w), full(f1b), full(f2w), full(f2b),
                  pl.BlockSpec((HY_HIDDEN, tc), lambda j: (0, j)), full(freq),
                  pl.BlockSpec((1, tc), lambda j: (0, j))],
        out_specs=pl.BlockSpec((seq, tc), lambda j: (0, j)),
        out_shape=jax.ShapeDtypeStruct((seq, n), F32),
        compiler_params=_cparams("parallel"),
        name="hyena_filters",
    )(zpos, f1w, f1b, f2w, f2b, f3w, freq, deltas)


def _lane_tile(t, width):
    return jnp.concatenate([t] * (width // LANES), axis=1) if width > LANES else t


def _cdft(vals, sign):
    if len(vals) == 1:
        return vals
    if len(vals) == 2:
        (ar, ai), (br, bi) = vals
        return [(ar + br, ai + bi), (ar - br, ai - bi)]
    if len(vals) == 8:
        ev, od = _cdft(vals[0::2], sign), _cdft(vals[1::2], sign)
        h = math.sqrt(0.5)
        (o0r, o0i), (o1r, o1i), (o2r, o2i), (o3r, o3i) = od
        tw = [(o0r, o0i),
              ((o1r - sign * o1i) * h, (o1i + sign * o1r) * h),
              (-sign * o2i, sign * o2r),
              ((-o3r - sign * o3i) * h, (sign * o3r - o3i) * h)]
        return ([(er + tr, ei + ti) for (er, ei), (tr, ti) in zip(ev, tw)]
                + [(er - tr, ei - ti) for (er, ei), (tr, ti) in zip(ev, tw)])
    assert len(vals) == 4
    (x0r, x0i), (x1r, x1i), (x2r, x2i), (x3r, x3i) = vals
    a0r, a0i, a1r, a1i = x0r + x2r, x0i + x2i, x0r - x2r, x0i - x2i
    a2r, a2i, a3r, a3i = x1r + x3r, x1i + x3i, x1r - x3r, x1i - x3i
    wr, wi = -sign * a3i, sign * a3r
    return [(a0r + a2r, a0i + a2i), (a1r + wr, a1i + wi), (a0r - a2r, a0i - a2i), (a1r - wr, a1i - wi)]


def _fwd_spectrum(z_list, f_ref, rows):
    lp = f_ref.shape[2]
    xs = []
    for r, z in enumerate(z_list):
        xc = jnp.dot(f_ref[r, rows, :], z, preferred_element_type=F32)
        xn = jnp.dot(f_ref[r, pl.ds(lp + rows.start, rows.size), :], z, preferred_element_type=F32)
        xs.append((xc, -xn))
    return _cdft(xs, -1)


def _spectrum_kernel(*refs, radix, seq, kc, tc):
    h_refs, (f_ref, sh_ref, o_ref) = refs[:radix], refs[radix:]
    lp = seq // radix
    hs = [h[...].astype(BF16) for h in h_refs]
    for c0 in range(0, lp, kc):
        rows = pl.ds(c0, kc)
        cs = _lane_tile(sh_ref[0, rows, :], tc) * (1.0 / seq)
        sn = _lane_tile(sh_ref[1, rows, :], tc) * (1.0 / seq)
        for q, (re, im) in enumerate(_fwd_spectrum(hs, f_ref, rows)):
            o_ref[0, q, rows, :] = re * cs - im * sn
            o_ref[1, q, rows, :] = re * sn + im * cs


def _whole(a):
    return pl.BlockSpec(a.shape, lambda *_: (0,) * a.ndim, pipeline_mode=pl.Buffered(1))


def filter_spectrum(filt, tabs, *, seq):
    n = filt.shape[1]
    radix, tc = _hy_radix(seq), _hy_fft_tc(seq)
    lp = seq // radix
    view = filt.reshape(lp, radix * n)
    f, _, sh = tabs
    groups = [pl.BlockSpec((lp, tc), functools.partial(lambda j, r: (0, r * (n // tc) + j), r=r)) for r in range(radix)]
    return pl.pallas_call(
        functools.partial(_spectrum_kernel, radix=radix, seq=seq, kc=min(HY_KC, lp), tc=tc),
        grid=(n // tc,),
        in_specs=groups + [_whole(f), _whole(sh)],
        out_specs=pl.BlockSpec((2, radix, lp, tc), lambda j: (0, 0, 0, j)),
        out_shape=jax.ShapeDtypeStruct((2, radix, lp, n), F32),
        compiler_params=_cparams("parallel"),
        name="filter_spectrum",
    )(*([view] * radix), f, sh)


def _short_conv(u, w, bias, seq):
    t = lax.broadcasted_iota(jnp.int32, (seq, 1), 0)
    prev = jnp.where(t == 0, 0.0, pltpu.roll(u, 1, 0))
    nxt = jnp.where(t == seq - 1, 0.0, pltpu.roll(u, seq - 1, 0))
    return bias + prev * w[0:1, :] + u * w[1:2, :] + nxt * w[2:3, :]


def _hy_radix(seq):
    return HY_RADIX if seq >= HY_RADIX_MIN_SEQ else 1


def _split_rows(val, out_refs, tmp_ref):
    radix = len(out_refs)
    if radix == 1:
        out_refs[0][...] = val.astype(BF16)
        return
    tmp_ref[...] = val
    for r in range(radix):
        out_refs[r][...] = tmp_ref[pl.ds(r, val.shape[0] // radix, stride=radix), :].astype(BF16)


def _merge_rows(in_refs, tmp_ref):
    radix = len(in_refs)
    if radix == 1:
        return in_refs[0][...]
    for r in range(radix):
        tmp_ref[pl.ds(r, in_refs[r].shape[0], stride=radix), :] = in_refs[r][...]
    return tmp_ref[...]


def _hy_pre_kernel(*refs, seq, radix):
    (u_ref, w_ref, b_ref), z_refs, tmp = refs[:3], refs[3:3 + radix], refs[3 + radix:]
    z = _short_conv(u_ref[...], w_ref[...], b_ref[...], seq)
    _split_rows(z, z_refs, tmp[0] if tmp else None)


def _hy_gate_kernel(*refs, seq, radix):
    ux_ref, uz_ref, wx_ref, bx_ref, wz_ref, bz_ref = refs[:6]
    cv_refs, hb_ref, zo_ref = refs[6:6 + radix], refs[6 + radix], refs[7 + radix]
    zb_refs, tmp = refs[8 + radix:8 + 2 * radix], refs[8 + 2 * radix:]
    tmp_ref = tmp[0] if tmp else None
    x = _short_conv(ux_ref[...], wx_ref[...], bx_ref[...], seq)
    z = _short_conv(uz_ref[...], wz_ref[...], bz_ref[...], seq)
    out = x * (_merge_rows(cv_refs, tmp_ref) + hb_ref[...] * z)
    zo_ref[...] = out
    _split_rows(out, zb_refs, tmp_ref)


def _hy_final_kernel(*refs, seq, radix):
    (ux_ref, wx_ref, bx_ref), cv_refs = refs[:3], refs[3:3 + radix]
    hb_ref, z_ref, zb_ref = refs[3 + radix:6 + radix]
    tmp = refs[6 + radix:]
    x = _short_conv(ux_ref[...], wx_ref[...], bx_ref[...], seq)
    cv = _merge_rows(cv_refs, tmp[0] if tmp else None)
    zb_ref[...] = (x * (cv + hb_ref[...] * z_ref[...])).astype(BF16)


HY_ELEM_SLAB_BYTES = 2 * 1024 * 1024


def _hy_tc(seq):
    if _hy_radix(seq) > 1:
        return LANES
    return max(LANES, min(HY_W, HY_ELEM_SLAB_BYTES // (4 * seq)))


def _hy_specs(seq, row0, col0):
    tc = _hy_tc(seq)
    cb, wb = col0 // tc, (col0 - OFF_HY) // tc
    u = pl.BlockSpec((seq, tc), lambda b, j: (row0 + b, cb + j))
    w = pl.BlockSpec((3, tc), lambda b, j: (0, wb + j))
    bb = pl.BlockSpec((1, tc), lambda b, j: (0, wb + j))
    act = pl.BlockSpec((seq, tc), lambda b, j: (b, j))
    hb = pl.BlockSpec((1, tc), lambda b, j: (0, j))
    part = pl.BlockSpec((seq // _hy_radix(seq), tc), lambda b, j: (b, j))
    return u, w, bb, act, hb, part


def _hy_elementwise_call(kernel, name, seq, batch, in_specs, out_specs, out_shape, args):
    radix, tc = _hy_radix(seq), _hy_tc(seq)
    assert radix == 1 or tc == LANES
    return pl.pallas_call(
        functools.partial(kernel, seq=seq, radix=radix),
        grid=(batch, HY_W // tc),
        in_specs=in_specs, out_specs=out_specs, out_shape=out_shape,
        scratch_shapes=[pltpu.VMEM((seq, tc), F32)] if radix > 1 else [],
        compiler_params=_cparams("parallel", "parallel"), name=name,
    )(*args)


def hyena_pre(proj, conv_w, conv_b, *, batch, seq, row0):
    radix = _hy_radix(seq)
    u, w, bb, _, _, part = _hy_specs(seq, row0, OFF_HY + HY_ORDER * HY_W)
    parts = [jax.ShapeDtypeStruct((batch * seq // radix, HY_W), BF16)] * radix
    return _hy_elementwise_call(_hy_pre_kernel, "hyena_pre", seq, batch, [u, w, bb], [part] * radix, parts,
                                (proj, conv_w, conv_b))


def hyena_gate(proj, conv_w, conv_b, conv, hy_bias, *, batch, seq, row0):
    radix = _hy_radix(seq)
    ux, wx, bx, act, hb, part = _hy_specs(seq, row0, OFF_HY)
    uz, wz, bz, _, _, _ = _hy_specs(seq, row0, OFF_HY + HY_ORDER * HY_W)
    parts = [jax.ShapeDtypeStruct((batch * seq // radix, HY_W), BF16)] * radix
    outs = _hy_elementwise_call(
        _hy_gate_kernel, "hyena_gate", seq, batch,
        [ux, uz, wx, bx, wz, bz] + [part] * radix + [hb], [act] + [part] * radix,
        [jax.ShapeDtypeStruct((batch * seq, HY_W), F32)] + parts,
        (proj, proj, conv_w, conv_b, conv_w, conv_b, *conv, hy_bias[0:1]))
    return outs[0], outs[1:]


def hyena_final(proj, conv_w, conv_b, conv, hy_bias, z1, *, batch, seq, row0):
    radix = _hy_radix(seq)
    ux, wx, bx, act, hb, part = _hy_specs(seq, row0, OFF_HY + HY_W)
    return _hy_elementwise_call(
        _hy_final_kernel, "hyena_final", seq, batch, [ux, wx, bx] + [part] * radix + [hb, act], act,
        jax.ShapeDtypeStruct((batch * seq, HY_W), BF16), (proj, conv_w, conv_b, *conv, hy_bias[1:2], z1))


def _long_conv_kernel(*refs, radix, seq, kc, tc):
    z_refs, (f_ref, ft_ref, g_ref) = refs[:radix], refs[radix:radix + 3]
    o_refs, u_s = refs[radix + 3:2 * radix + 3], refs[2 * radix + 3]
    lp = seq // radix
    zs = [z[...] for z in z_refs]
    for c0 in range(0, lp, kc):
        rows = pl.ds(c0, kc)
        ys = []
        for q, (zr, zi) in enumerate(_fwd_spectrum(zs, f_ref, rows)):
            gr, gi = g_ref[0, q, rows, :], g_ref[1, q, rows, :]
            ys.append((zr * gr - zi * gi, zr * gi + zi * gr))
        for r, (re, im) in enumerate(_cdft(ys, +1)):
            u_s[r, rows, :] = re.astype(BF16)
            u_s[r, pl.ds(lp + c0, kc), :] = (-im).astype(BF16)
    for r in range(radix):
        o_refs[r][...] = jnp.dot(ft_ref[r], u_s[r], preferred_element_type=F32)


def _hy_fft_tc(seq):
    return HY_TC_FFT if _hy_radix(seq) > 1 else 2 * HY_TC_FFT


def hyena_long_conv(z_parts, g, tabs, *, batch, seq, order):
    c = z_parts[0].shape[1]
    radix, tc = _hy_radix(seq), _hy_fft_tc(seq)
    lp = seq // radix
    f, ft, _ = tabs
    ncb = c // tc
    goff = order * ncb
    part = pl.BlockSpec((lp, tc), lambda j, b: (b, j))
    return pl.pallas_call(
        functools.partial(_long_conv_kernel, radix=radix, seq=seq, kc=min(HY_KC, lp), tc=tc),
        grid=(ncb, batch),
        in_specs=[part] * radix + [_whole(f), _whole(ft),
                                   pl.BlockSpec((2, radix, lp, tc), lambda j, b: (0, 0, 0, goff + j),
                                                pipeline_mode=pl.Buffered(1))],
        out_specs=[part] * radix,
        out_shape=[jax.ShapeDtypeStruct((batch * lp, c), F32)] * radix,
        scratch_shapes=[pltpu.VMEM((radix, 2 * lp, tc), BF16)],
        compiler_params=_cparams("parallel", "parallel"),
        name="hyena_long_conv",
    )(*z_parts, f, ft, g)


def dft_tables(seq):
    radix = _hy_radix(seq)
    lp = seq // radix
    odd = 2 * jnp.arange(lp, dtype=jnp.int32) + 1

    def cs(num, den):
        ang = (num % (2 * den)).astype(F32) * (math.pi / den)
        return jnp.cos(ang), jnp.sin(ang)

    ca, sa = cs(odd[:, None] * jnp.arange(lp, dtype=jnp.int32)[None, :], 2 * lp)
    cb, sb = cs(odd[None, :] * jnp.arange(radix, dtype=jnp.int32)[:, None], 2 * seq)
    fc = ca[None] * cb[:, :, None] - sa[None] * sb[:, :, None]
    fs = sa[None] * cb[:, :, None] + ca[None] * sb[:, :, None]
    f = jnp.concatenate([fc, fs], axis=1).astype(BF16)
    sc, ss = cs(odd, 4)
    sh = jnp.broadcast_to(jnp.stack([sc, ss], axis=0)[..., None], (2, lp, LANES))
    return f, jnp.transpose(f, (0, 2, 1)), sh


def position_features(seq):
    t = jnp.arange(seq, dtype=F32)
    tn = t / seq
    bands = jnp.arange(1, HY_BANDS + 1, dtype=F32)
    ang = 2.0 * jnp.pi * tn[:, None] * bands[None, :]
    z = jnp.concatenate([tn[:, None], jnp.cos(ang), jnp.sin(ang)], axis=-1)
    return jnp.pad(z, ((0, 0), (0, LANES - z.shape[1])))


def rope_tables(seq):
    n_freq = GLA_DK // 4
    pos = jnp.arange(seq)
    row = (pos // GRID_W).astype(F32)
    col = (pos % GRID_W).astype(F32)
    inv = ROPE_BASE ** (-jnp.arange(n_freq, dtype=F32) / n_freq)
    ang = jnp.concatenate([row[:, None] * inv, col[:, None] * inv], axis=-1)
    cos, sin = jnp.cos(ang), jnp.sin(ang)
    return jnp.concatenate([cos, cos], axis=-1), jnp.concatenate([-sin, sin], axis=-1)


def _merge_kernel(of_ref, ob_ref, gg_ref, gn_ref, nb_ref, hc_ref, ga_ref, gb_ref, gc_ref,
                  wa_ref, wb_ref, wc_ref, o_ref, a_s):
    @pl.when(pl.program_id(1) == 0)
    def _():
        for h in range(GLA_HEADS):
            cols = slice(h * GLA_DV, (h + 1) * GLA_DV)
            oa = of_ref[:, cols] + ob_ref[:, cols]
            oa = _head_norm(oa, gn_ref[...])
            gg = gg_ref[:, cols]
            a_s[:, cols] = (oa * (gg * jax.nn.sigmoid(gg))).astype(BF16)

    ya = jnp.dot(a_s[...], wa_ref[...], preferred_element_type=F32)
    yb = jnp.dot(nb_ref[...].astype(BF16), wb_ref[...], preferred_element_type=F32)
    yc = jnp.dot(hc_ref[...], wc_ref[...], preferred_element_type=F32)
    merged = (jax.nn.sigmoid(ga_ref[...]) * ya + jax.nn.sigmoid(gb_ref[...]) * yb
              + jax.nn.sigmoid(gc_ref[...]) * yc)
    o_ref[...] = merged.astype(o_ref.dtype)


def branch_merge(o_f, o_b, proj, gnorm, o_attn, o_hy, w_a, w_b, w_c, *, row0_blocks, tm, tn):
    t = o_f.shape[0]
    d = w_a.shape[1]
    assert OFF_GATES % tn == 0 and d % tn == 0
    act = lambda w: pl.BlockSpec((tm, w), lambda i, j: (i, 0))
    gate = lambda g: pl.BlockSpec((tm, tn), lambda i, j: (row0_blocks + i, (OFF_GATES + g * d) // tn + j))
    wspec = lambda w: pl.BlockSpec((w.shape[0], tn), lambda i, j: (0, j))
    return pl.pallas_call(
        _merge_kernel,
        grid=(t // tm, d // tn),
        in_specs=[act(VAL_W), act(VAL_W),
                  pl.BlockSpec((tm, VAL_W), lambda i, j: (row0_blocks + i, OFF_GG // VAL_W)),
                  pl.BlockSpec((1, GLA_DV), lambda i, j: (0, 0)),
                  act(NA_W), act(HY_W), gate(0), gate(1), gate(2),
                  wspec(w_a), wspec(w_b), wspec(w_c)],
        out_specs=pl.BlockSpec((tm, tn), lambda i, j: (i, j)),
        out_shape=jax.ShapeDtypeStruct((t, d), BF16),
        scratch_shapes=[pltpu.VMEM((tm, VAL_W), BF16)],
        compiler_params=_cparams("parallel", "arbitrary"),
        name="branch_merge",
    )(o_f, o_b, proj, gnorm, o_attn, o_hy, proj, proj, proj, w_a, w_b, w_c)


def _matmul_residual_kernel(a_ref, w_ref, g_ref, r_ref, o_ref):
    y = jnp.dot(a_ref[...], w_ref[...], preferred_element_type=F32)
    o_ref[...] = r_ref[...] + g_ref[0] * y


def matmul_residual(a, w, gate, resid, *, rows_per_mod, tm, tn):
    t, k = a.shape
    n = w.shape[1]
    return pl.pallas_call(
        _matmul_residual_kernel,
        grid=(t // tm, n // tn),
        in_specs=[pl.BlockSpec((tm, k), lambda i, j: (i, 0)),
                  pl.BlockSpec((k, tn), lambda i, j: (0, j)),
                  pl.BlockSpec((1, 1, tn), lambda i, j: ((i * tm) // rows_per_mod, 0, j)),
                  pl.BlockSpec((tm, tn), lambda i, j: (i, j))],
        out_specs=pl.BlockSpec((tm, tn), lambda i, j: (i, j)),
        out_shape=jax.ShapeDtypeStruct((t, n), F32),
        compiler_params=_cparams("parallel", "arbitrary"),
        name="matmul_residual",
    )(a, w, gate, resid)


def _trunk_layer(x, mods, p, consts, *, batch, seq, layer, latent):
    sh1, sc1, g1, sh2, sc2, g2 = mods
    t = x.shape[0]
    rows_per_mod = t // sh1.shape[0]
    tm = 512

    proj, la = norm_matmul(x, p["norm1_g"], sc1, sh1, p["w_in_main"], rows_per_mod=rows_per_mod,
                           tm=tm, tn=1536, relu2=False, out_dtype=F32,
                           decay_weights=(p["w_in_ga"], p["wa2_bd"], p["ba2"]))

    rows_blk = min(seq, 512)
    rope = consts["rope"] if latent else None
    s0 = consts["state_gla"] if latent else None
    o_f, s_f = gla_direction(proj, la, rope, s0, batch=batch, seq=seq, reverse=False, layer=layer,
                             direction=0, rows_blk=rows_blk)
    o_b, s_b = gla_direction(proj, la, rope, s0, batch=batch, seq=seq, reverse=True, layer=layer,
                             direction=1, rows_blk=rows_blk)

    if latent:
        o_attn = neighbourhood_attention(proj, consts["cache_k"], consts["cache_v"], p["na_qnorm_g"],
                                         p["na_knorm_g"], p["na_table"], batch=batch, seq=seq,
                                         layer=layer, row0=0)
        new_ctx = None
    else:
        o_attn, kn, vn = ctx_attention(proj, p["na_qnorm_g"], p["na_knorm_g"], batch=batch, seq=seq)
        new_ctx = (jnp.stack([s_f, s_b], axis=1), kn, vn)

    key = "lat" if latent else "ctx"
    tabs = consts["dft_" + key]
    g_spec = p["spectrum_" + key]
    z0 = hyena_pre(proj, p["hy_conv_w"], p["hy_conv_b"], batch=batch, seq=seq, row0=0)
    cv0 = hyena_long_conv(z0, g_spec, tabs, batch=batch, seq=seq, order=0)
    z1, z1b = hyena_gate(proj, p["hy_conv_w"], p["hy_conv_b"], cv0, p["hy_bias"], batch=batch, seq=seq, row0=0)
    cv1 = hyena_long_conv(z1b, g_spec, tabs, batch=batch, seq=seq, order=1)
    o_hy = hyena_final(proj, p["hy_conv_w"], p["hy_conv_b"], cv1, p["hy_bias"], z1, batch=batch, seq=seq, row0=0)

    merged = branch_merge(o_f, o_b, proj, p["gla_norm_g"], o_attn, o_hy, p["w_br_a"], p["w_br_b"],
                          p["w_br_c"], row0_blocks=0, tm=256, tn=2048)
    x = matmul_residual(merged, p["w_out"], g1, x, rows_per_mod=rows_per_mod, tm=tm, tn=2048)

    u = norm_matmul(x, p["norm2_g"], sc2, sh2, p["w_mlp1"], rows_per_mod=rows_per_mod,
                    tm=tm, tn=2048, relu2=True, out_dtype=BF16)
    x = matmul_residual(u, p["w_mlp2"], g2, x, rows_per_mod=rows_per_mod, tm=tm, tn=512)
    return x, new_ctx


def _reorder_w_in(w_in):
    o_ga = 2 * KEY_W + 2 * VAL_W
    n_ga = 2 * GLA_LOWRANK
    o_hy = o_ga + n_ga + 3 * NA_W
    o_gates = o_hy + (HY_ORDER + 1) * HY_W
    main = jnp.concatenate([w_in[..., :o_ga], w_in[..., o_ga + n_ga:o_hy], w_in[..., o_gates:],
                            w_in[..., o_hy:o_gates]], axis=-1).astype(BF16)
    ga = jnp.pad(w_in[..., o_ga:o_ga + n_ga], ((0, 0), (0, LANES - n_ga))).astype(BF16)
    return main, ga


def kernel(x_prompt, x_sample, state_gla, cache_na_k, cache_na_v, c, c_ctx, w_mod, b_mod, norm1_g, norm2_g, w_in, gla_wa2, gla_ba2, gla_norm_g, na_qnorm_g, na_knorm_g, na_rpb, hy_conv_w, hy_conv_b, hy_f1_w, hy_f1_b, hy_f2_w, hy_f2_b, hy_f3_w, hy_freq, hy_bias, w_br_a, w_br_b, w_br_c, w_out, w_mlp1, w_mlp2):
    batch, seq, d = x_prompt.shape
    dec_batch, dec_seq, _ = x_sample.shape
    depth = w_mod.shape[0]

    bf = lambda w: w.astype(BF16)
    wa2_bd = jnp.zeros((depth, LANES, 2 * KEY_W), F32)
    wa2_bd = wa2_bd.at[:, :GLA_LOWRANK, :KEY_W].set(gla_wa2[:, 0])
    wa2_bd = wa2_bd.at[:, GLA_LOWRANK:2 * GLA_LOWRANK, KEY_W:].set(gla_wa2[:, 1])
    ba2 = gla_ba2.reshape(depth, 1, 2 * KEY_W)
    f1w = jnp.pad(hy_f1_w, ((0, 0), (0, LANES - hy_f1_w.shape[1]), (0, 0)))

    consts = {
        "rope": rope_tables(dec_seq),
        "state_gla": state_gla, "cache_k": cache_na_k, "cache_v": cache_na_v,
        "dft_ctx": dft_tables(seq), "dft_lat": dft_tables(dec_seq),
    }
    deltas = jnp.abs(jnp.linspace(math.log(HY_DECAY_TARGET) / HY_DECAY_PCT_LONG,
                                  math.log(HY_DECAY_TARGET) / HY_DECAY_PCT_SHORT, HY_W, dtype=F32))
    deltas = jnp.tile(deltas, HY_ORDER)[None, :]
    zpos = {"ctx": position_features(seq), "lat": position_features(dec_seq)}

    cond = jnp.zeros((16, d), F32).at[0].set(c_ctx).at[1:1 + dec_batch].set(c)

    xp = x_prompt.reshape(batch * seq, d)
    xs = x_sample.reshape(dec_batch * dec_seq, d)
    new_gla, new_k, new_v = [], [], []
    for l in range(depth):
        m = adaln(cond, w_mod, b_mod[l][None, :], l)
        mods_ctx = tuple(m[0:1, i * d:(i + 1) * d][:, None, :] for i in range(N_MOD))
        mods_lat = tuple(m[1:1 + dec_batch, i * d:(i + 1) * d][:, None, :] for i in range(N_MOD))
        w_in_main, w_in_ga = _reorder_w_in(w_in[l])
        p = dict(norm1_g=norm1_g[l][None], norm2_g=norm2_g[l][None], w_in_main=w_in_main, w_in_ga=w_in_ga,
                 wa2_bd=bf(wa2_bd[l]), ba2=ba2[l], gla_norm_g=gla_norm_g[l][None], na_qnorm_g=na_qnorm_g[l][None],
                 na_knorm_g=na_knorm_g[l][None], na_table=na_bias_table(na_rpb[l], dec_seq // GRID_W),
                 hy_conv_w=hy_conv_w[l], hy_conv_b=hy_conv_b[l][None], hy_bias=hy_bias[l],
                 w_br_a=bf(w_br_a[l]), w_br_b=bf(w_br_b[l]), w_br_c=bf(w_br_c[l]), w_out=bf(w_out[l]),
                 w_mlp1=bf(w_mlp1[l]), w_mlp2=bf(w_mlp2[l]))
        for key, length in (("ctx", seq), ("lat", dec_seq)):
            filt = hyena_filters(zpos[key], f1w[l], hy_f1_b[l][None], hy_f2_w[l], hy_f2_b[l][None],
                                 hy_f3_w[l], hy_freq[l][None], deltas, seq=length)
            p["spectrum_" + key] = filter_spectrum(filt, consts["dft_" + key], seq=length)
        xp, (s_gla, k_ctx, v_ctx) = _trunk_layer(xp, mods_ctx, p, consts, batch=batch, seq=seq, layer=l, latent=False)
        new_gla.append(s_gla)
        new_k.append(k_ctx)
        new_v.append(v_ctx)
        xs, _ = _trunk_layer(xs, mods_lat, p, consts, batch=dec_batch, seq=dec_seq, layer=l, latent=True)
    return (xp.reshape(batch, seq, d), xs.reshape(dec_batch, dec_seq, d),
            jnp.stack(new_gla, axis=1), jnp.stack(new_k, axis=1), jnp.stack(new_v, axis=1))
```

```python
import functools
import math

import jax
import jax.numpy as jnp
import numpy as np
from jax import lax
from jax.experimental import pallas as pl
from jax.experimental.pallas import tpu as pltpu

F32 = jnp.float32
BF16 = jnp.bfloat16
HIGHEST = lax.Precision.HIGHEST

GRID_W = 64
N_MOD = 6
NORM_EPS = 1e-6
GLA_HEADS = 4
GLA_DK = 128
GLA_DV = 256
GLA_LOWRANK = 16
GLA_TAU = 16.0
GLA_CHUNK = 64
GLA_SUB = 16
GLA_HEADS_PER_STEP = 4
GLA_SAFE_EXP = 60.0
ROPE_BASE = 10000.0
NA_HEADS = 8
NA_HD = 128
NA_KH = 8
NA_KW = 16
NA_QROWS = 4
NA_UNION = NA_KH + NA_QROWS
NA_BLOCK_UNROLL = 4
HY_W = 1024
HY_ORDER = 2
HY_BANDS = 16
HY_HIDDEN = 64
HY_DECAY_TARGET = 1e-2
HY_DECAY_PCT_SHORT = 0.3
HY_DECAY_PCT_LONG = 1.5
HY_RADIX = 8
HY_RADIX_MIN_SEQ = 2048
HY_KC = 256
HY_TC_FFT = 256

V7X_VMEM_LIMIT_BYTES = 56 * 1024 * 1024
LANES = 128

KEY_W = GLA_HEADS * GLA_DK
VAL_W = GLA_HEADS * GLA_DV
NA_W = NA_HEADS * NA_HD
OFF_GQ = 0
OFF_GK = OFF_GQ + KEY_W
OFF_GV = OFF_GK + KEY_W
OFF_GG = OFF_GV + VAL_W
OFF_NQ = OFF_GG + VAL_W
OFF_NK = OFF_NQ + NA_W
OFF_NV = OFF_NK + NA_W
OFF_GATES = OFF_NV + NA_W
OFF_HY = OFF_GATES + 3 * 2048


def _cparams(*sem):
    return pltpu.CompilerParams(dimension_semantics=sem, vmem_limit_bytes=V7X_VMEM_LIMIT_BYTES)


def _modulated_norm(x, g, scale, shift):
    ms = jnp.mean(x * x, axis=-1, keepdims=True)
    y = x * lax.rsqrt(ms + NORM_EPS) * g
    return y * (1.0 + scale) + shift


def _adaln_kernel(c_ref, w_ref, b_ref, o_ref):
    c = c_ref[...]
    a = (c * jax.nn.sigmoid(c)).astype(BF16)
    o_ref[...] = jnp.dot(a, w_ref[...].astype(BF16), preferred_element_type=F32) + b_ref[...]


def adaln(cond, w_mod, b_mod, layer):
    m, d = cond.shape
    n = w_mod.shape[2]
    tn = 1024
    return pl.pallas_call(
        _adaln_kernel,
        grid=(n // tn,),
        in_specs=[pl.BlockSpec((m, d), lambda j: (0, 0)),
                  pl.BlockSpec((None, d, tn), lambda j: (layer, 0, j)),
                  pl.BlockSpec((1, tn), lambda j: (0, j))],
        out_specs=pl.BlockSpec((m, tn), lambda j: (0, j)),
        out_shape=jax.ShapeDtypeStruct((m, n), F32),
        compiler_params=_cparams("parallel"),
        name="adaln",
    )(cond, w_mod, b_mod)


def _log_decay(h, wga_ref, wa2_ref, ba2_ref):
    ga = jnp.dot(h, wga_ref[...], preferred_element_type=F32)
    pre = jnp.dot(ga.astype(BF16), wa2_ref[...], preferred_element_type=F32) + ba2_ref[...]
    log_sig = jnp.minimum(pre, 0.0) - jnp.log(1.0 + jnp.exp(-jnp.abs(pre)))
    return log_sig * (1.0 / GLA_TAU)


def _norm_matmul_kernel(*refs, relu2, decay):
    x_ref, g_ref, sc_ref, sh_ref, w_ref = refs[:5]
    if decay:
        wga_ref, wa2_ref, ba2_ref, o_ref, la_ref, h_ref = refs[5:]
    else:
        o_ref, h_ref = refs[5:]

    @pl.when(pl.program_id(1) == 0)
    def _():
        h = _modulated_norm(x_ref[...], g_ref[...], sc_ref[0], sh_ref[0]).astype(BF16)
        h_ref[...] = h
        if decay:
            la_ref[...] = _log_decay(h, wga_ref, wa2_ref, ba2_ref)

    acc = jnp.dot(h_ref[...], w_ref[...], preferred_element_type=F32)
    if relu2:
        acc = jnp.square(jnp.maximum(acc, 0.0))
    o_ref[...] = acc.astype(o_ref.dtype)


def norm_matmul(x, g, scale, shift, w, *, rows_per_mod, tm, tn, relu2, out_dtype, decay_weights=None):
    t, d = x.shape
    n = w.shape[1]
    mod_map = lambda i, j: ((i * tm) // rows_per_mod, 0, 0)
    decay = decay_weights is not None
    in_specs = [pl.BlockSpec((tm, d), lambda i, j: (i, 0)),
                pl.BlockSpec((1, d), lambda i, j: (0, 0)),
                pl.BlockSpec((1, 1, d), mod_map),
                pl.BlockSpec((1, 1, d), mod_map),
                pl.BlockSpec((d, tn), lambda i, j: (0, j))]
    out_specs = pl.BlockSpec((tm, tn), lambda i, j: (i, j))
    out_shape = jax.ShapeDtypeStruct((t, n), out_dtype)
    args = [x, g, scale, shift, w]
    if decay:
        in_specs += [pl.BlockSpec(a.shape, lambda i, j: (0, 0)) for a in decay_weights]
        n_la = decay_weights[1].shape[1]
        out_specs = [out_specs, pl.BlockSpec((tm, n_la), lambda i, j: (i, 0))]
        out_shape = [out_shape, jax.ShapeDtypeStruct((t, n_la), F32)]
        args += list(decay_weights)
    return pl.pallas_call(
        functools.partial(_norm_matmul_kernel, relu2=relu2, decay=decay),
        grid=(t // tm, n // tn),
        in_specs=in_specs, out_specs=out_specs, out_shape=out_shape,
        scratch_shapes=[pltpu.VMEM((tm, d), BF16)],
        compiler_params=_cparams("parallel", "arbitrary"),
        name="norm_matmul_relu2" if relu2 else "norm_matmul",
    )(*args)


def _cumulative_log_decay(la, reverse):
    c = la.shape[0]
    row = lax.broadcasted_iota(jnp.int32, (c, c), 0)
    col = lax.broadcasted_iota(jnp.int32, (c, c), 1)
    tri = jnp.where((row <= col) if reverse else (row >= col), 1.0, 0.0).astype(BF16)
    hi = la.astype(BF16)
    rest = la - hi.astype(F32)
    mid = rest.astype(BF16)
    lo = (rest - mid.astype(F32)).astype(BF16)
    dot = lambda t: jnp.dot(tri, t, preferred_element_type=F32)
    return dot(hi) + dot(mid) + dot(lo)


def _gla_chunk_scores(q, k, v, b, *, reverse, exact):
    c = GLA_CHUNK
    n = GLA_SUB
    edge = b[0:1, :] if reverse else b[c - 1:c, :]
    qe = (q * jnp.exp(b)).astype(BF16)
    kd = (k * jnp.exp(edge - b)).astype(BF16)
    m = jnp.dot(v.T.astype(BF16), kd, preferred_element_type=F32)
    decay = jnp.exp(edge)

    lane = lax.broadcasted_iota(jnp.int32, (n, c), 1)
    sub_row = lax.broadcasted_iota(jnp.int32, (n, c), 0)
    blocks = []
    cap = 0.0 if exact else GLA_SAFE_EXP
    for i in range(c // n):
        lo, hi = i * n, (i + 1) * n
        q_i = q[lo:hi, :]
        b_i = b[lo:hi, :]
        has_border = (hi < c) if reverse else (lo > 0)
        if has_border:
            r = b[hi:hi + 1, :] if reverse else b[lo - 1:lo, :]
            lhs = (q_i * jnp.exp(b_i - r)).astype(BF16)
            rhs = (k * jnp.exp(jnp.minimum(r - b, cap))).astype(BF16)
        else:
            lhs = (q_i * jnp.exp(b_i)).astype(BF16)
            rhs = (k * jnp.exp(jnp.minimum(-b, cap))).astype(BF16)
        a_i = lax.dot_general(lhs, rhs, (((1,), (1,)), ((), ())), preferred_element_type=F32)
        if exact:
            a_diag = jnp.zeros((n, c), F32)
            for j in range(n):
                s = lo + j
                p = q_i * k[s:s + 1, :] * jnp.exp(b_i - b[s:s + 1, :])
                a_diag = jnp.where(lane == s, jnp.sum(p, axis=1, keepdims=True), a_diag)
            in_block = (lane >= lo) & (lane < hi)
            a_i = jnp.where(in_block, a_diag, a_i)
        causal = (lane >= sub_row + lo) if reverse else (lane <= sub_row + lo)
        blocks.append(jnp.where(causal, a_i, 0.0))
    a = jnp.concatenate(blocks, axis=0).astype(BF16)
    return qe, a, m, decay


def _gla_inter(qe, st):
    return lax.dot_general(qe, st.astype(BF16), (((1,), (1,)), ((), ())), preferred_element_type=F32)


def _gla_kernel(*refs, reverse, rope, has_s0, nchunk, heads):
    it = iter(refs)
    q_ref, k_ref, v_ref, la_ref = next(it), next(it), next(it), next(it)
    cos_ref = sin_ref = s0_ref = None
    if rope:
        cos_ref, sin_ref = next(it), next(it)
    if has_s0:
        s0_ref = next(it)
    o_ref, sfin_ref, st_ref = next(it), next(it), next(it)

    blk = pl.program_id(2)

    @pl.when(blk == 0)
    def _():
        for hh in range(heads):
            st_ref[hh] = s0_ref[0, 0, 0, hh].T if has_s0 else jnp.zeros(st_ref.shape[1:], F32)

    kcols = lambda hh: slice(hh * GLA_DK, (hh + 1) * GLA_DK)
    vcols = lambda hh: slice(hh * GLA_DV, (hh + 1) * GLA_DV)

    def load_qk(rows, hh):
        q = q_ref[rows, kcols(hh)] * (GLA_DK ** -0.5)
        k = k_ref[rows, kcols(hh)]
        if rope:
            cs, sn = cos_ref[rows, :], sin_ref[rows, :]
            q = q * cs + pltpu.roll(q, GLA_DK // 2, 1) * sn
            k = k * cs + pltpu.roll(k, GLA_DK // 2, 1) * sn
        return q, k

    def intra(a, rows, hh):
        return jnp.dot(a, v_ref[rows, vcols(hh)].astype(BF16), preferred_element_type=F32)

    def run_exact():
        def body(step, carry):
            ci = (nchunk - 1 - step) if reverse else step
            rows = pl.ds(pl.multiple_of(ci * GLA_CHUNK, GLA_CHUNK), GLA_CHUNK)
            for hh in range(heads):
                q, k = load_qk(rows, hh)
                b = _cumulative_log_decay(la_ref[rows, kcols(hh)], reverse)
                qe, a, m, decay = _gla_chunk_scores(q, k, v_ref[rows, vcols(hh)], b, reverse=reverse, exact=True)
                o_ref[rows, vcols(hh)] = _gla_inter(qe, st_ref[hh]) + intra(a, rows, hh)
                st_ref[hh] = st_ref[hh] * decay + m
            return carry

        lax.fori_loop(0, nchunk, body, 0)

    def run_fast():
        order = list(range(nchunk))[::-1] if reverse else list(range(nchunk))
        rows = [pl.ds(ci * GLA_CHUNK, GLA_CHUNK) for ci in range(nchunk)]
        units = [(hh, ci) for hh in range(heads) for ci in range(nchunk)]
        bs = {u: _cumulative_log_decay(la_ref[rows[u[1]], kcols(u[0])], reverse) for u in units}
        scores = {u: _gla_chunk_scores(*load_qk(rows[u[1]], u[0]), v_ref[rows[u[1]], vcols(u[0])], bs[u],
                                       reverse=reverse, exact=False) for u in units}
        o_intra = {u: intra(scores[u][1], rows[u[1]], u[0]) for u in units}
        states = {}
        for hh in range(heads):
            st = st_ref[hh]
            for ci in order:
                states[hh, ci] = st
                st = st * scores[hh, ci][3] + scores[hh, ci][2]
            st_ref[hh] = st
        for hh, ci in units:
            o_ref[rows[ci], vcols(hh)] = _gla_inter(scores[hh, ci][0], states[hh, ci]) + o_intra[hh, ci]

    steep = jnp.min(la_ref[...]) < -(GLA_SAFE_EXP / GLA_SUB)
    pl.when(steep)(run_exact)
    pl.when(jnp.logical_not(steep))(run_fast)

    @pl.when(blk == pl.num_programs(2) - 1)
    def _():
        for hh in range(heads):
            sfin_ref[0, hh] = st_ref[hh].T


def gla_direction(proj, la, rope_tabs, s0, *, batch, seq, reverse, layer, direction, rows_blk):
    t = proj.shape[0]
    nblk = seq // rows_blk
    nchunk = rows_blk // GLA_CHUNK
    rope = rope_tabs is not None
    has_s0 = s0 is not None

    def rb(b, c):
        cc = (nblk - 1 - c) if reverse else c
        return b * nblk + cc

    hp = GLA_HEADS_PER_STEP
    kw, vw = hp * GLA_DK, hp * GLA_DV
    assert OFF_GQ % kw == 0 and OFF_GK % kw == 0 and OFF_GV % vw == 0 and KEY_W % kw == 0
    kq, kk, kv = OFF_GQ // kw, OFF_GK // kw, OFF_GV // vw
    kla = direction * (KEY_W // kw)
    in_specs = [pl.BlockSpec((rows_blk, kw), lambda b, h, c: (rb(b, c), kq + h)),
                pl.BlockSpec((rows_blk, kw), lambda b, h, c: (rb(b, c), kk + h)),
                pl.BlockSpec((rows_blk, vw), lambda b, h, c: (rb(b, c), kv + h)),
                pl.BlockSpec((rows_blk, kw), lambda b, h, c: (rb(b, c), kla + h))]
    args = [proj, proj, proj, la]
    if rope:
        tab_map = lambda b, h, c: ((nblk - 1 - c) if reverse else c, 0)
        in_specs += [pl.BlockSpec((rows_blk, GLA_DK), tab_map)] * 2
        args += list(rope_tabs)
    if has_s0:
        in_specs.append(pl.BlockSpec((1, 1, 1, hp, GLA_DK, GLA_DV), lambda b, h, c: (b, layer, direction, h, 0, 0)))
        args.append(s0)
    o, sfin = pl.pallas_call(
        functools.partial(_gla_kernel, reverse=reverse, rope=rope, has_s0=has_s0, nchunk=nchunk, heads=hp),
        grid=(batch, GLA_HEADS // hp, nblk),
        in_specs=in_specs,
        out_specs=[pl.BlockSpec((rows_blk, vw), lambda b, h, c: (rb(b, c), h)),
                   pl.BlockSpec((1, hp, GLA_DK, GLA_DV), lambda b, h, c: (b, h, 0, 0))],
        out_shape=[jax.ShapeDtypeStruct((t, VAL_W), F32),
                   jax.ShapeDtypeStruct((batch, GLA_HEADS, GLA_DK, GLA_DV), F32)],
        scratch_shapes=[pltpu.VMEM((hp, GLA_DV, GLA_DK), F32)],
        compiler_params=_cparams("parallel", "parallel", "arbitrary"),
        name="gla_bwd" if reverse else "gla_fwd",
    )(*args)
    return o, sfin


def _head_norm(x, g):
    return x * lax.rsqrt(jnp.mean(x * x, axis=-1, keepdims=True) + NORM_EPS) * g


def _head_norm_mxu(x, g):
    n = x.shape[1]
    mean_sq = jnp.dot((x * x).astype(BF16), jnp.full((n, n), 1.0 / n, BF16), preferred_element_type=F32)
    return x * lax.rsqrt(mean_sq + NORM_EPS) * g


def _ctx_attn_kernel(q_ref, k_ref, v_ref, gq_ref, gk_ref, o_ref, kn_ref, vn_ref):
    scale = NA_HD ** -0.5
    for h in range(NA_HEADS):
        cols = slice(h * NA_HD, (h + 1) * NA_HD)
        qn = _head_norm(q_ref[:, cols], gq_ref[...]) * scale
        kn = _head_norm(k_ref[:, cols], gk_ref[...])
        v = v_ref[:, cols]
        kn_ref[0, h] = kn
        vn_ref[0, h] = v
        s = lax.dot_general(qn.astype(BF16), kn.astype(BF16), (((1,), (1,)), ((), ())),
                            preferred_element_type=F32)
        e = jnp.exp(s - jnp.max(s, axis=-1, keepdims=True))
        p = e / jnp.sum(e, axis=-1, keepdims=True)
        o_ref[:, cols] = jnp.dot(p.astype(BF16), v.astype(BF16), preferred_element_type=F32)


def ctx_attention(proj, gq, gk, *, batch, seq):
    t = proj.shape[0]
    cq, ck, cv = OFF_NQ // NA_W, OFF_NK // NA_W, OFF_NV // NA_W
    cache_shape = jax.ShapeDtypeStruct((batch, NA_HEADS, seq, NA_HD), F32)
    cache_spec = pl.BlockSpec((1, NA_HEADS, seq, NA_HD), lambda b: (b, 0, 0, 0))
    return pl.pallas_call(
        _ctx_attn_kernel,
        grid=(batch,),
        in_specs=[pl.BlockSpec((seq, NA_W), lambda b: (b, cq)),
                  pl.BlockSpec((seq, NA_W), lambda b: (b, ck)),
                  pl.BlockSpec((seq, NA_W), lambda b: (b, cv)),
                  pl.BlockSpec((1, NA_HD), lambda b: (0, 0)),
                  pl.BlockSpec((1, NA_HD), lambda b: (0, 0))],
        out_specs=[pl.BlockSpec((seq, NA_W), lambda b: (b, 0)), cache_spec, cache_spec],
        out_shape=[jax.ShapeDtypeStruct((t, NA_W), F32), cache_shape, cache_shape],
        compiler_params=_cparams("parallel"),
        name="ctx_attention",
    )(proj, proj, proj, gq, gk)


def _na_kernel(q_ref, k_ref, v_ref, kc_ref, vc_ref, gq_ref, gk_ref, tab_ref, o_ref, qn_s, kn_s, v_s, *, rows):
    scale = NA_HD ** -0.5
    qn_s[...] = (_head_norm_mxu(q_ref[...], gq_ref[...]) * scale).astype(BF16)
    kn_s[...] = _head_norm_mxu(k_ref[...], gk_ref[...]).astype(BF16)
    v_s[...] = v_ref[...].astype(BF16)
    kctx = kc_ref[0, 0, 0].astype(BF16)
    vctx = vc_ref[0, 0, 0].astype(BF16)
    nblk = rows // NA_QROWS
    qlen, win = NA_QROWS * GRID_W, NA_UNION * GRID_W

    def scores(g):
        u0 = jnp.clip(g * NA_QROWS - NA_KH // 2, 0, rows - NA_UNION)
        cls = jnp.where(g == 0, 0, jnp.where(g == nblk - 1, 2, 1))
        qrows = pl.ds(pl.multiple_of(g * qlen, qlen), qlen)
        wrows = pl.ds(pl.multiple_of(u0 * GRID_W, GRID_W), win)
        qn = qn_s[qrows, :]
        s_loc = lax.dot_general(qn, kn_s[wrows, :], (((1,), (1,)), ((), ())), preferred_element_type=F32)
        s_ctx = lax.dot_general(qn, kctx, (((1,), (1,)), ((), ())), preferred_element_type=F32)
        return qrows, wrows, s_loc + tab_ref[0, cls], s_ctx

    def softmax(s_loc, s_ctx):
        m = jnp.maximum(jnp.max(s_loc, axis=-1, keepdims=True), jnp.max(s_ctx, axis=-1, keepdims=True))
        e_loc = jnp.exp(s_loc - m)
        e_ctx = jnp.exp(s_ctx - m)
        den = jnp.sum(e_loc, axis=-1, keepdims=True) + jnp.sum(e_ctx, axis=-1, keepdims=True)
        return e_loc.astype(BF16), e_ctx.astype(BF16), den

    def body(step, carry):
        blocks = [scores(step * NA_BLOCK_UNROLL + u) for u in range(NA_BLOCK_UNROLL)]
        probs = [softmax(s_loc, s_ctx) for _, _, s_loc, s_ctx in blocks]
        for (qrows, wrows, _, _), (e_loc, e_ctx, den) in zip(blocks, probs):
            acc = jnp.dot(e_loc, v_s[wrows, :], preferred_element_type=F32)
            acc = acc + jnp.dot(e_ctx, vctx, preferred_element_type=F32)
            o_ref[qrows, :] = acc / den
        return carry

    lax.fori_loop(0, nblk // NA_BLOCK_UNROLL, body, 0)


def neighbourhood_attention(proj, cache_k, cache_v, gq, gk, table, *, batch, seq, layer, row0):
    rows = seq // GRID_W
    assert rows % NA_QROWS == 0 and rows // NA_QROWS >= 3 and rows >= NA_UNION
    past = cache_k.shape[3]
    cq, ck, cv = OFF_NQ // NA_HD, OFF_NK // NA_HD, OFF_NV // NA_HD
    cache_spec = pl.BlockSpec((1, 1, 1, past, NA_HD), lambda b, h: (b, layer, h, 0, 0))
    return pl.pallas_call(
        functools.partial(_na_kernel, rows=rows),
        grid=(batch, NA_HEADS),
        in_specs=[pl.BlockSpec((seq, NA_HD), lambda b, h: (row0 + b, cq + h)),
                  pl.BlockSpec((seq, NA_HD), lambda b, h: (row0 + b, ck + h)),
                  pl.BlockSpec((seq, NA_HD), lambda b, h: (row0 + b, cv + h)),
                  cache_spec, cache_spec,
                  pl.BlockSpec((1, NA_HD), lambda b, h: (0, 0)),
                  pl.BlockSpec((1, NA_HD), lambda b, h: (0, 0)),
                  pl.BlockSpec((1, 3, NA_QROWS * GRID_W, NA_UNION * GRID_W), lambda b, h: (h, 0, 0, 0))],
        out_specs=pl.BlockSpec((seq, NA_HD), lambda b, h: (b, h)),
        out_shape=jax.ShapeDtypeStruct((batch * seq, NA_W), F32),
        scratch_shapes=[pltpu.VMEM((seq, NA_HD), BF16)] * 3,
        compiler_params=_cparams("parallel", "parallel"),
        name="neighbourhood_attention",
    )(proj, proj, proj, cache_k, cache_v, gq, gk, table)


def na_bias_table(rpb, rows):
    nblk = rows // NA_QROWS
    col = np.arange(GRID_W)
    c_start = np.clip(col - NA_KW // 2, 0, GRID_W - NA_KW)
    col_in = (col[None, :] >= c_start[:, None]) & (col[None, :] < c_start[:, None] + NA_KW)
    col_idx = np.clip(col[None, :] - col[:, None] + NA_KW - 1, 0, 2 * NA_KW - 2)
    row_off, row_in = [], []
    for g in (0, 1, nblk - 1):
        u0 = np.clip(g * NA_QROWS - NA_KH // 2, 0, rows - NA_UNION)
        r = g * NA_QROWS + np.arange(NA_QROWS)[:, None]
        r_start = np.clip(r - NA_KH // 2, 0, rows - NA_KH)
        key_row = u0 + np.arange(NA_UNION)[None, :]
        row_in.append((key_row >= r_start) & (key_row < r_start + NA_KH))
        row_off.append(np.clip(key_row - r + NA_KH - 1, 0, 2 * NA_KH - 2))
    row_off, row_in = np.stack(row_off), np.stack(row_in)
    n_off = 2 * NA_KH - 1
    tiles = jnp.where(col_in[None, None], rpb[:, :, col_idx], -jnp.inf)
    tiles = jnp.concatenate([tiles, jnp.full_like(tiles[:, :1], -jnp.inf)], axis=1)
    tab = tiles[:, np.where(row_in, row_off, n_off)]
    tab = jnp.transpose(tab, (0, 1, 2, 4, 3, 5))
    return tab.reshape(rpb.shape[0], 3, NA_QROWS * GRID_W, NA_UNION * GRID_W).astype(F32)


def _filter_kernel(z_ref, f1w_ref, f1b_ref, f2w_ref, f2b_ref, f3w_ref, fr_ref, dl_ref, o_ref, *, seq):
    fr = fr_ref[...]
    h = jnp.sin(fr * (jnp.dot(z_ref[...], f1w_ref[...], preferred_element_type=F32, precision=HIGHEST)
                      + f1b_ref[...]))
    h = jnp.sin(fr * (jnp.dot(h, f2w_ref[...], preferred_element_type=F32, precision=HIGHEST)
                      + f2b_ref[...]))
    h = jnp.dot(h, f3w_ref[...], preferred_element_type=F32, precision=HIGHEST)
    t = lax.broadcasted_iota(jnp.int32, (seq, 1), 0).astype(F32)
    dist = jnp.abs(t - float(seq // 2)) * (2.0 / seq)
    h = h * jnp.exp(-dist * dl_ref[...])
    o_ref[...] = h * lax.rsqrt(jnp.sum(h * h, axis=0, keepdims=True) + NORM_EPS)


def hyena_filters(zpos, f1w, f1b, f2w, f2b, f3w, freq, deltas, *, seq):
    n = f3w.shape[1]
    tc = 512
    full = lambda a: pl.BlockSpec(a.shape, lambda j: (0,) * a.ndim)
    return pl.pallas_call(
        functools.partial(_filter_kernel, seq=seq),
        grid=(n // tc,),
        in_specs=[full(zpos), full(f1w), full(f1b), full(f2w), full(f2b),
                  pl.BlockSpec((HY_HIDDEN, tc), lambda j: (0, j)), full(freq),
                  pl.BlockSpec((1, tc), lambda j: (0, j))],
        out_specs=pl.BlockSpec((seq, tc), lambda j: (0, j)),
        out_shape=jax.ShapeDtypeStruct((seq, n), F32),
        compiler_params=_cparams("parallel"),
        name="hyena_filters",
    )(zpos, f1w, f1b, f2w, f2b, f3w, freq, deltas)


def _lane_tile(t, width):
    return jnp.concatenate([t] * (width // LANES), axis=1) if width > LANES else t


def _cdft(vals, sign):
    if len(vals) == 1:
        return vals
    if len(vals) == 2:
        (ar, ai), (br, bi) = vals
        return [(ar + br, ai + bi), (ar - br, ai - bi)]
    if len(vals) == 8:
        ev, od = _cdft(vals[0::2], sign), _cdft(vals[1::2], sign)
        h = math.sqrt(0.5)
        (o0r, o0i), (o1r, o1i), (o2r, o2i), (o3r, o3i) = od
        tw = [(o0r, o0i),
              ((o1r - sign * o1i) * h, (o1i + sign * o1r) * h),
              (-sign * o2i, sign * o2r),
              ((-o3r - sign * o3i) * h, (sign * o3r - o3i) * h)]
        return ([(er + tr, ei + ti) for (er, ei), (tr, ti) in zip(ev, tw)]
                + [(er - tr, ei - ti) for (er, ei), (tr, ti) in zip(ev, tw)])
    assert len(vals) == 4
    (x0r, x0i), (x1r, x1i), (x2r, x2i), (x3r, x3i) = vals
    a0r, a0i, a1r, a1i = x0r + x2r, x0i + x2i, x0r - x2r, x0i - x2i
    a2r, a2i, a3r, a3i = x1r + x3r, x1i + x3i, x1r - x3r, x1i - x3i
    wr, wi = -sign * a3i, sign * a3r
    return [(a0r + a2r, a0i + a2i), (a1r + wr, a1i + wi), (a0r - a2r, a0i - a2i), (a1r - wr, a1i - wi)]


def _fwd_spectrum(z_list, f_ref, rows):
    lp = f_ref.shape[2]
    xs = []
    for r, z in enumerate(z_list):
        xc = jnp.dot(f_ref[r, rows, :], z, preferred_element_type=F32)
        xn = jnp.dot(f_ref[r, pl.ds(lp + rows.start, rows.size), :], z, preferred_element_type=F32)
        xs.append((xc, -xn))
    return _cdft(xs, -1)


def _spectrum_kernel(*refs, radix, seq, kc, tc):
    h_refs, (f_ref, sh_ref, o_ref) = refs[:radix], refs[radix:]
    lp = seq // radix
    hs = [h[...].astype(BF16) for h in h_refs]
    for c0 in range(0, lp, kc):
        rows = pl.ds(c0, kc)
        cs = _lane_tile(sh_ref[0, rows, :], tc) * (1.0 / seq)
        sn = _lane_tile(sh_ref[1, rows, :], tc) * (1.0 / seq)
        for q, (re, im) in enumerate(_fwd_spectrum(hs, f_ref, rows)):
            o_ref[0, q, rows, :] = re * cs - im * sn
            o_ref[1, q, rows, :] = re * sn + im * cs


def _whole(a):
    return pl.BlockSpec(a.shape, lambda *_: (0,) * a.ndim, pipeline_mode=pl.Buffered(1))


def filter_spectrum(filt, tabs, *, seq):
    n = filt.shape[1]
    radix, tc = _hy_radix(seq), _hy_fft_tc(seq)
    lp = seq // radix
    view = filt.reshape(lp, radix * n)
    f, _, sh = tabs
    groups = [pl.BlockSpec((lp, tc), functools.partial(lambda j, r: (0, r * (n // tc) + j), r=r)) for r in range(radix)]
    return pl.pallas_call(
        functools.partial(_spectrum_kernel, radix=radix, seq=seq, kc=min(HY_KC, lp), tc=tc),
        grid=(n // tc,),
        in_specs=groups + [_whole(f), _whole(sh)],
        out_specs=pl.BlockSpec((2, radix, lp, tc), lambda j: (0, 0, 0, j)),
        out_shape=jax.ShapeDtypeStruct((2, radix, lp, n), F32),
        compiler_params=_cparams("parallel"),
        name="filter_spectrum",
    )(*([view] * radix), f, sh)


def _short_conv(u, w, bias, seq):
    t = lax.broadcasted_iota(jnp.int32, (seq, 1), 0)
    prev = jnp.where(t == 0, 0.0, pltpu.roll(u, 1, 0))
    nxt = jnp.where(t == seq - 1, 0.0, pltpu.roll(u, seq - 1, 0))
    return bias + prev * w[0:1, :] + u * w[1:2, :] + nxt * w[2:3, :]


def _hy_radix(seq):
    return HY_RADIX if seq >= HY_RADIX_MIN_SEQ else 1


def _split_rows(val, out_refs, tmp_ref):
    radix = len(out_refs)
    if radix == 1:
        out_refs[0][...] = val.astype(BF16)
        return
    tmp_ref[...] = val
    for r in range(radix):
        out_refs[r][...] = tmp_ref[pl.ds(r, val.shape[0] // radix, stride=radix), :].astype(BF16)


def _merge_rows(in_refs, tmp_ref):
    radix = len(in_refs)
    if radix == 1:
        return in_refs[0][...]
    for r in range(radix):
        tmp_ref[pl.ds(r, in_refs[r].shape[0], stride=radix), :] = in_refs[r][...]
    return tmp_ref[...]


def _hy_pre_kernel(*refs, seq, radix):
    (u_ref, w_ref, b_ref), z_refs, tmp = refs[:3], refs[3:3 + radix], refs[3 + radix:]
    z = _short_conv(u_ref[...], w_ref[...], b_ref[...], seq)
    _split_rows(z, z_refs, tmp[0] if tmp else None)


def _hy_gate_kernel(*refs, seq, radix):
    ux_ref, uz_ref, wx_ref, bx_ref, wz_ref, bz_ref = refs[:6]
    cv_refs, hb_ref, zo_ref = refs[6:6 + radix], refs[6 + radix], refs[7 + radix]
    zb_refs, tmp = refs[8 + radix:8 + 2 * radix], refs[8 + 2 * radix:]
    tmp_ref = tmp[0] if tmp else None
    x = _short_conv(ux_ref[...], wx_ref[...], bx_ref[...], seq)
    z = _short_conv(uz_ref[...], wz_ref[...], bz_ref[...], seq)
    out = x * (_merge_rows(cv_refs, tmp_ref) + hb_ref[...] * z)
    zo_ref[...] = out
    _split_rows(out, zb_refs, tmp_ref)


def _hy_final_kernel(*refs, seq, radix):
    (ux_ref, wx_ref, bx_ref), cv_refs = refs[:3], refs[3:3 + radix]
    hb_ref, z_ref, zb_ref = refs[3 + radix:6 + radix]
    tmp = refs[6 + radix:]
    x = _short_conv(ux_ref[...], wx_ref[...], bx_ref[...], seq)
    cv = _merge_rows(cv_refs, tmp[0] if tmp else None)
    zb_ref[...] = (x * (cv + hb_ref[...] * z_ref[...])).astype(BF16)


HY_ELEM_SLAB_BYTES = 2 * 1024 * 1024


def _hy_tc(seq):
    if _hy_radix(seq) > 1:
        return LANES
    return max(LANES, min(HY_W, HY_ELEM_SLAB_BYTES // (4 * seq)))


def _hy_specs(seq, row0, col0):
    tc = _hy_tc(seq)
    cb, wb = col0 // tc, (col0 - OFF_HY) // tc
    u = pl.BlockSpec((seq, tc), lambda b, j: (row0 + b, cb + j))
    w = pl.BlockSpec((3, tc), lambda b, j: (0, wb + j))
    bb = pl.BlockSpec((1, tc), lambda b, j: (0, wb + j))
    act = pl.BlockSpec((seq, tc), lambda b, j: (b, j))
    hb = pl.BlockSpec((1, tc), lambda b, j: (0, j))
    part = pl.BlockSpec((seq // _hy_radix(seq), tc), lambda b, j: (b, j))
    return u, w, bb, act, hb, part


def _hy_elementwise_call(kernel, name, seq, batch, in_specs, out_specs, out_shape, args):
    radix, tc = _hy_radix(seq), _hy_tc(seq)
    assert radix == 1 or tc == LANES
    return pl.pallas_call(
        functools.partial(kernel, seq=seq, radix=radix),
        grid=(batch, HY_W // tc),
        in_specs=in_specs, out_specs=out_specs, out_shape=out_shape,
        scratch_shapes=[pltpu.VMEM((seq, tc), F32)] if radix > 1 else [],
        compiler_params=_cparams("parallel", "parallel"), name=name,
    )(*args)


def hyena_pre(proj, conv_w, conv_b, *, batch, seq, row0):
    radix = _hy_radix(seq)
    u, w, bb, _, _, part = _hy_specs(seq, row0, OFF_HY + HY_ORDER * HY_W)
    parts = [jax.ShapeDtypeStruct((batch * seq // radix, HY_W), BF16)] * radix
    return _hy_elementwise_call(_hy_pre_kernel, "hyena_pre", seq, batch, [u, w, bb], [part] * radix, parts,
                                (proj, conv_w, conv_b))


def hyena_gate(proj, conv_w, conv_b, conv, hy_bias, *, batch, seq, row0):
    radix = _hy_radix(seq)
    ux, wx, bx, act, hb, part = _hy_specs(seq, row0, OFF_HY)
    uz, wz, bz, _, _, _ = _hy_specs(seq, row0, OFF_HY + HY_ORDER * HY_W)
    parts = [jax.ShapeDtypeStruct((batch * seq // radix, HY_W), BF16)] * radix
    outs = _hy_elementwise_call(
        _hy_gate_kernel, "hyena_gate", seq, batch,
        [ux, uz, wx, bx, wz, bz] + [part] * radix + [hb], [act] + [part] * radix,
        [jax.ShapeDtypeStruct((batch * seq, HY_W), F32)] + parts,
        (proj, proj, conv_w, conv_b, conv_w, conv_b, *conv, hy_bias[0:1]))
    return outs[0], outs[1:]


def hyena_final(proj, conv_w, conv_b, conv, hy_bias, z1, *, batch, seq, row0):
    radix = _hy_radix(seq)
    ux, wx, bx, act, hb, part = _hy_specs(seq, row0, OFF_HY + HY_W)
    return _hy_elementwise_call(
        _hy_final_kernel, "hyena_final", seq, batch, [ux, wx, bx] + [part] * radix + [hb, act], act,
        jax.ShapeDtypeStruct((batch * seq, HY_W), BF16), (proj, conv_w, conv_b, *conv, hy_bias[1:2], z1))


def _long_conv_kernel(*refs, radix, seq, kc, tc):
    z_refs, (f_ref, ft_ref, g_ref) = refs[:radix], refs[radix:radix + 3]
    o_refs, u_s = refs[radix + 3:2 * radix + 3], refs[2 * radix + 3]
    lp = seq // radix
    zs = [z[...] for z in z_refs]
    for c0 in range(0, lp, kc):
        rows = pl.ds(c0, kc)
        ys = []
        for q, (zr, zi) in enumerate(_fwd_spectrum(zs, f_ref, rows)):
            gr, gi = g_ref[0, q, rows, :], g_ref[1, q, rows, :]
            ys.append((zr * gr - zi * gi, zr * gi + zi * gr))
        for r, (re, im) in enumerate(_cdft(ys, +1)):
            u_s[r, rows, :] = re.astype(BF16)
            u_s[r, pl.ds(lp + c0, kc), :] = (-im).astype(BF16)
    for r in range(radix):
        o_refs[r][...] = jnp.dot(ft_ref[r], u_s[r], preferred_element_type=F32)


def _hy_fft_tc(seq):
    return HY_TC_FFT if _hy_radix(seq) > 1 else 2 * HY_TC_FFT


def hyena_long_conv(z_parts, g, tabs, *, batch, seq, order):
    c = z_parts[0].shape[1]
    radix, tc = _hy_radix(seq), _hy_fft_tc(seq)
    lp = seq // radix
    f, ft, _ = tabs
    ncb = c // tc
    goff = order * ncb
    part = pl.BlockSpec((lp, tc), lambda j, b: (b, j))
    return pl.pallas_call(
        functools.partial(_long_conv_kernel, radix=radix, seq=seq, kc=min(HY_KC, lp), tc=tc),
        grid=(ncb, batch),
        in_specs=[part] * radix + [_whole(f), _whole(ft),
                                   pl.BlockSpec((2, radix, lp, tc), lambda j, b: (0, 0, 0, goff + j),
                                                pipeline_mode=pl.Buffered(1))],
        out_specs=[part] * radix,
        out_shape=[jax.ShapeDtypeStruct((batch * lp, c), F32)] * radix,
        scratch_shapes=[pltpu.VMEM((radix, 2 * lp, tc), BF16)],
        compiler_params=_cparams("parallel", "parallel"),
        name="hyena_long_conv",
    )(*z_parts, f, ft, g)


def dft_tables(seq):
    radix = _hy_radix(seq)
    lp = seq // radix
    odd = 2 * jnp.arange(lp, dtype=jnp.int32) + 1

    def cs(num, den):
        ang = (num % (2 * den)).astype(F32) * (math.pi / den)
        return jnp.cos(ang), jnp.sin(ang)

    ca, sa = cs(odd[:, None] * jnp.arange(lp, dtype=jnp.int32)[None, :], 2 * lp)
    cb, sb = cs(odd[None, :] * jnp.arange(radix, dtype=jnp.int32)[:, None], 2 * seq)
    fc = ca[None] * cb[:, :, None] - sa[None] * sb[:, :, None]
    fs = sa[None] * cb[:, :, None] + ca[None] * sb[:, :, None]
    f = jnp.concatenate([fc, fs], axis=1).astype(BF16)
    sc, ss = cs(odd, 4)
    sh = jnp.broadcast_to(jnp.stack([sc, ss], axis=0)[..., None], (2, lp, LANES))
    return f, jnp.transpose(f, (0, 2, 1)), sh


def position_features(seq):
    t = jnp.arange(seq, dtype=F32)
    tn = t / seq
    bands = jnp.arange(1, HY_BANDS + 1, dtype=F32)
    ang = 2.0 * jnp.pi * tn[:, None] * bands[None, :]
    z = jnp.concatenate([tn[:, None], jnp.cos(ang), jnp.sin(ang)], axis=-1)
    return jnp.pad(z, ((0, 0), (0, LANES - z.shape[1])))


def rope_tables(seq):
    n_freq = GLA_DK // 4
    pos = jnp.arange(seq)
    row = (pos // GRID_W).astype(F32)
    col = (pos % GRID_W).astype(F32)
    inv = ROPE_BASE ** (-jnp.arange(n_freq, dtype=F32) / n_freq)
    ang = jnp.concatenate([row[:, None] * inv, col[:, None] * inv], axis=-1)
    cos, sin = jnp.cos(ang), jnp.sin(ang)
    return jnp.concatenate([cos, cos], axis=-1), jnp.concatenate([-sin, sin], axis=-1)


def _merge_kernel(of_ref, ob_ref, gg_ref, gn_ref, nb_ref, hc_ref, ga_ref, gb_ref, gc_ref,
                  wa_ref, wb_ref, wc_ref, o_ref, a_s):
    @pl.when(pl.program_id(1) == 0)
    def _():
        for h in range(GLA_HEADS):
            cols = slice(h * GLA_DV, (h + 1) * GLA_DV)
            oa = of_ref[:, cols] + ob_ref[:, cols]
            oa = _head_norm(oa, gn_ref[...])
            gg = gg_ref[:, cols]
            a_s[:, cols] = (oa * (gg * jax.nn.sigmoid(gg))).astype(BF16)

    ya = jnp.dot(a_s[...], wa_ref[...], preferred_element_type=F32)
    yb = jnp.dot(nb_ref[...].astype(BF16), wb_ref[...], preferred_element_type=F32)
    yc = jnp.dot(hc_ref[...], wc_ref[...], preferred_element_type=F32)
    merged = (jax.nn.sigmoid(ga_ref[...]) * ya + jax.nn.sigmoid(gb_ref[...]) * yb
              + jax.nn.sigmoid(gc_ref[...]) * yc)
    o_ref[...] = merged.astype(o_ref.dtype)


def branch_merge(o_f, o_b, proj, gnorm, o_attn, o_hy, w_a, w_b, w_c, *, row0_blocks, tm, tn):
    t = o_f.shape[0]
    d = w_a.shape[1]
    assert OFF_GATES % tn == 0 and d % tn == 0
    act = lambda w: pl.BlockSpec((tm, w), lambda i, j: (i, 0))
    gate = lambda g: pl.BlockSpec((tm, tn), lambda i, j: (row0_blocks + i, (OFF_GATES + g * d) // tn + j))
    wspec = lambda w: pl.BlockSpec((w.shape[0], tn), lambda i, j: (0, j))
    return pl.pallas_call(
        _merge_kernel,
        grid=(t // tm, d // tn),
        in_specs=[act(VAL_W), act(VAL_W),
                  pl.BlockSpec((tm, VAL_W), lambda i, j: (row0_blocks + i, OFF_GG // VAL_W)),
                  pl.BlockSpec((1, GLA_DV), lambda i, j: (0, 0)),
                  act(NA_W), act(HY_W), gate(0), gate(1), gate(2),
                  wspec(w_a), wspec(w_b), wspec(w_c)],
        out_specs=pl.BlockSpec((tm, tn), lambda i, j: (i, j)),
        out_shape=jax.ShapeDtypeStruct((t, d), BF16),
        scratch_shapes=[pltpu.VMEM((tm, VAL_W), BF16)],
        compiler_params=_cparams("parallel", "arbitrary"),
        name="branch_merge",
    )(o_f, o_b, proj, gnorm, o_attn, o_hy, proj, proj, proj, w_a, w_b, w_c)


def _matmul_residual_kernel(a_ref, w_ref, g_ref, r_ref, o_ref):
    y = jnp.dot(a_ref[...], w_ref[...], preferred_element_type=F32)
    o_ref[...] = r_ref[...] + g_ref[0] * y


def matmul_residual(a, w, gate, resid, *, rows_per_mod, tm, tn):
    t, k = a.shape
    n = w.shape[1]
    return pl.pallas_call(
        _matmul_residual_kernel,
        grid=(t // tm, n // tn),
        in_specs=[pl.BlockSpec((tm, k), lambda i, j: (i, 0)),
                  pl.BlockSpec((k, tn), lambda i, j: (0, j)),
                  pl.BlockSpec((1, 1, tn), lambda i, j: ((i * tm) // rows_per_mod, 0, j)),
                  pl.BlockSpec((tm, tn), lambda i, j: (i, j))],
        out_specs=pl.BlockSpec((tm, tn), lambda i, j: (i, j)),
        out_shape=jax.ShapeDtypeStruct((t, n), F32),
        compiler_params=_cparams("parallel", "arbitrary"),
        name="matmul_residual",
    )(a, w, gate, resid)


def _trunk_layer(x, mods, p, consts, *, batch, seq, layer, latent):
    sh1, sc1, g1, sh2, sc2, g2 = mods
    t = x.shape[0]
    rows_per_mod = t // sh1.shape[0]
    tm = 512

    proj, la = norm_matmul(x, p["norm1_g"], sc1, sh1, p["w_in_main"], rows_per_mod=rows_per_mod,
                           tm=tm, tn=1536, relu2=False, out_dtype=F32,
                           decay_weights=(p["w_in_ga"], p["wa2_bd"], p["ba2"]))

    rows_blk = min(seq, 512)
    rope = consts["rope"] if latent else None
    s0 = consts["state_gla"] if latent else None
    o_f, s_f = gla_direction(proj, la, rope, s0, batch=batch, seq=seq, reverse=False, layer=layer,
                             direction=0, rows_blk=rows_blk)
    o_b, s_b = gla_direction(proj, la, rope, s0, batch=batch, seq=seq, reverse=True, layer=layer,
                             direction=1, rows_blk=rows_blk)

    if latent:
        o_attn = neighbourhood_attention(proj, consts["cache_k"], consts["cache_v"], p["na_qnorm_g"],
                                         p["na_knorm_g"], p["na_table"], batch=batch, seq=seq,
                                         layer=layer, row0=0)
        new_ctx = None
    else:
        o_attn, kn, vn = ctx_attention(proj, p["na_qnorm_g"], p["na_knorm_g"], batch=batch, seq=seq)
        new_ctx = (jnp.stack([s_f, s_b], axis=1), kn, vn)

    key = "lat" if latent else "ctx"
    tabs = consts["dft_" + key]
    g_spec = p["spectrum_" + key]
    z0 = hyena_pre(proj, p["hy_conv_w"], p["hy_conv_b"], batch=batch, seq=seq, row0=0)
    cv0 = hyena_long_conv(z0, g_spec, tabs, batch=batch, seq=seq, order=0)
    z1, z1b = hyena_gate(proj, p["hy_conv_w"], p["hy_conv_b"], cv0, p["hy_bias"], batch=batch, seq=seq, row0=0)
    cv1 = hyena_long_conv(z1b, g_spec, tabs, batch=batch, seq=seq, order=1)
    o_hy = hyena_final(proj, p["hy_conv_w"], p["hy_conv_b"], cv1, p["hy_bias"], z1, batch=batch, seq=seq, row0=0)

    merged = branch_merge(o_f, o_b, proj, p["gla_norm_g"], o_attn, o_hy, p["w_br_a"], p["w_br_b"],
                          p["w_br_c"], row0_blocks=0, tm=256, tn=2048)
    x = matmul_residual(merged, p["w_out"], g1, x, rows_per_mod=rows_per_mod, tm=tm, tn=2048)

    u = norm_matmul(x, p["norm2_g"], sc2, sh2, p["w_mlp1"], rows_per_mod=rows_per_mod,
                    tm=tm, tn=2048, relu2=True, out_dtype=BF16)
    x = matmul_residual(u, p["w_mlp2"], g2, x, rows_per_mod=rows_per_mod, tm=tm, tn=512)
    return x, new_ctx


def _reorder_w_in(w_in):
    o_ga = 2 * KEY_W + 2 * VAL_W
    n_ga = 2 * GLA_LOWRANK
    o_hy = o_ga + n_ga + 3 * NA_W
    o_gates = o_hy + (HY_ORDER + 1) * HY_W
    main = jnp.concatenate([w_in[..., :o_ga], w_in[..., o_ga + n_ga:o_hy], w_in[..., o_gates:],
                            w_in[..., o_hy:o_gates]], axis=-1).astype(BF16)
    ga = jnp.pad(w_in[..., o_ga:o_ga + n_ga], ((0, 0), (0, LANES - n_ga))).astype(BF16)
    return main, ga


def kernel(x_prompt, x_sample, state_gla, cache_na_k, cache_na_v, c, c_ctx, w_mod, b_mod, norm1_g, norm2_g, w_in, gla_wa2, gla_ba2, gla_norm_g, na_qnorm_g, na_knorm_g, na_rpb, hy_conv_w, hy_conv_b, hy_f1_w, hy_f1_b, hy_f2_w, hy_f2_b, hy_f3_w, hy_freq, hy_bias, w_br_a, w_br_b, w_br_c, w_out, w_mlp1, w_mlp2):
    batch, seq, d = x_prompt.shape
    dec_batch, dec_seq, _ = x_sample.shape
    depth = w_mod.shape[0]

    bf = lambda w: w.astype(BF16)
    wa2_bd = jnp.zeros((depth, LANES, 2 * KEY_W), F32)
    wa2_bd = wa2_bd.at[:, :GLA_LOWRANK, :KEY_W].set(gla_wa2[:, 0])
    wa2_bd = wa2_bd.at[:, GLA_LOWRANK:2 * GLA_LOWRANK, KEY_W:].set(gla_wa2[:, 1])
    ba2 = gla_ba2.reshape(depth, 1, 2 * KEY_W)
    f1w = jnp.pad(hy_f1_w, ((0, 0), (0, LANES - hy_f1_w.shape[1]), (0, 0)))

    consts = {
        "rope": rope_tables(dec_seq),
        "state_gla": state_gla, "cache_k": cache_na_k, "cache_v": cache_na_v,
        "dft_ctx": dft_tables(seq), "dft_lat": dft_tables(dec_seq),
    }
    deltas = jnp.abs(jnp.linspace(math.log(HY_DECAY_TARGET) / HY_DECAY_PCT_LONG,
                                  math.log(HY_DECAY_TARGET) / HY_DECAY_PCT_SHORT, HY_W, dtype=F32))
    deltas = jnp.tile(deltas, HY_ORDER)[None, :]
    zpos = {"ctx": position_features(seq), "lat": position_features(dec_seq)}

    cond = jnp.zeros((16, d), F32).at[0].set(c_ctx).at[1:1 + dec_batch].set(c)

    xp = x_prompt.reshape(batch * seq, d)
    xs = x_sample.reshape(dec_batch * dec_seq, d)
    new_gla, new_k, new_v = [], [], []
    for l in range(depth):
        m = adaln(cond, w_mod, b_mod[l][None, :], l)
        mods_ctx = tuple(m[0:1, i * d:(i + 1) * d][:, None, :] for i in range(N_MOD))
        mods_lat = tuple(m[1:1 + dec_batch, i * d:(i + 1) * d][:, None, :] for i in range(N_MOD))
        w_in_main, w_in_ga = _reorder_w_in(w_in[l])
        p = dict(norm1_g=norm1_g[l][None], norm2_g=norm2_g[l][None], w_in_main=w_in_main, w_in_ga=w_in_ga,
                 wa2_bd=bf(wa2_bd[l]), ba2=ba2[l], gla_norm_g=gla_norm_g[l][None], na_qnorm_g=na_qnorm_g[l][None],
                 na_knorm_g=na_knorm_g[l][None], na_table=na_bias_table(na_rpb[l], dec_seq // GRID_W),
                 hy_conv_w=hy_conv_w[l], hy_conv_b=hy_conv_b[l][None], hy_bias=hy_bias[l],
                 w_br_a=bf(w_br_a[l]), w_br_b=bf(w_br_b[l]), w_br_c=bf(w_br_c[l]), w_out=bf(w_out[l]),
                 w_mlp1=bf(w_mlp1[l]), w_mlp2=bf(w_mlp2[l]))
        for key, length in (("ctx", seq), ("lat", dec_seq)):
            filt = hyena_filters(zpos[key], f1w[l], hy_f1_b[l][None], hy_f2_w[l], hy_f2_b[l][None],
                                 hy_f3_w[l], hy_freq[l][None], deltas, seq=length)
            p["spectrum_" + key] = filter_spectrum(filt, consts["dft_" + key], seq=length)
        xp, (s_gla, k_ctx, v_ctx) = _trunk_layer(xp, mods_ctx, p, consts, batch=batch, seq=seq, layer=l, latent=False)
        new_gla.append(s_gla)
        new_k.append(k_ctx)
        new_v.append(v_ctx)
        xs, _ = _trunk_layer(xs, mods_lat, p, consts, batch=dec_batch, seq=dec_seq, layer=l, latent=True)
    return (xp.reshape(batch, seq, d), xs.reshape(dec_batch, dec_seq, d),
            jnp.stack(new_gla, axis=1), jnp.stack(new_k, axis=1), jnp.stack(new_v, axis=1))
```

```python
import functools
import math

import jax
import jax.numpy as jnp
import numpy as np
from jax import lax
from jax.experimental import pallas as pl
from jax.experimental.pallas import tpu as pltpu

F32 = jnp.float32
BF16 = jnp.bfloat16
HIGHEST = lax.Precision.HIGHEST

GRID_W = 64
N_MOD = 6
NORM_EPS = 1e-6
GLA_HEADS = 4
GLA_DK = 128
GLA_DV = 256
GLA_LOWRANK = 16
GLA_TAU = 16.0
GLA_CHUNK = 64
GLA_SUB = 32
GLA_HEADS_PER_STEP = 4
GLA_SAFE_EXP = 60.0
ROPE_BASE = 10000.0
NA_HEADS = 8
NA_HD = 128
NA_KH = 8
NA_KW = 16
NA_QROWS = 4
NA_UNION = NA_KH + NA_QROWS
NA_BLOCK_UNROLL = 4
HY_W = 1024
HY_ORDER = 2
HY_BANDS = 16
HY_HIDDEN = 64
HY_DECAY_TARGET = 1e-2
HY_DECAY_PCT_SHORT = 0.3
HY_DECAY_PCT_LONG = 1.5
HY_RADIX = 8
HY_RADIX_MIN_SEQ = 2048
HY_KC = 256
HY_TC_FFT = 256

V7X_VMEM_LIMIT_BYTES = 56 * 1024 * 1024
LANES = 128

KEY_W = GLA_HEADS * GLA_DK
VAL_W = GLA_HEADS * GLA_DV
NA_W = NA_HEADS * NA_HD
OFF_GQ = 0
OFF_GK = OFF_GQ + KEY_W
OFF_GV = OFF_GK + KEY_W
OFF_GG = OFF_GV + VAL_W
OFF_NQ = OFF_GG + VAL_W
OFF_NK = OFF_NQ + NA_W
OFF_NV = OFF_NK + NA_W
OFF_GATES = OFF_NV + NA_W
OFF_HY = OFF_GATES + 3 * 2048


def _cparams(*sem):
    return pltpu.CompilerParams(dimension_semantics=sem, vmem_limit_bytes=V7X_VMEM_LIMIT_BYTES)


def _modulated_norm(x, g, scale, shift):
    ms = jnp.mean(x * x, axis=-1, keepdims=True)
    y = x * lax.rsqrt(ms + NORM_EPS) * g
    return y * (1.0 + scale) + shift


def _adaln_kernel(c_ref, w_ref, b_ref, o_ref):
    c = c_ref[...]
    a = (c * jax.nn.sigmoid(c)).astype(BF16)
    o_ref[...] = jnp.dot(a, w_ref[...].astype(BF16), preferred_element_type=F32) + b_ref[...]


def adaln(cond, w_mod, b_mod, layer):
    m, d = cond.shape
    n = w_mod.shape[2]
    tn = 1024
    return pl.pallas_call(
        _adaln_kernel,
        grid=(n // tn,),
        in_specs=[pl.BlockSpec((m, d), lambda j: (0, 0)),
                  pl.BlockSpec((None, d, tn), lambda j: (layer, 0, j)),
                  pl.BlockSpec((1, tn), lambda j: (0, j))],
        out_specs=pl.BlockSpec((m, tn), lambda j: (0, j)),
        out_shape=jax.ShapeDtypeStruct((m, n), F32),
        compiler_params=_cparams("parallel"),
        name="adaln",
    )(cond, w_mod, b_mod)


def _log_decay(h, wga_ref, wa2_ref, ba2_ref):
    ga = jnp.dot(h, wga_ref[...], preferred_element_type=F32)
    pre = jnp.dot(ga.astype(BF16), wa2_ref[...], preferred_element_type=F32) + ba2_ref[...]
    log_sig = jnp.minimum(pre, 0.0) - jnp.log(1.0 + jnp.exp(-jnp.abs(pre)))
    return log_sig * (1.0 / GLA_TAU)


def _norm_matmul_kernel(*refs, relu2, decay):
    x_ref, g_ref, sc_ref, sh_ref, w_ref = refs[:5]
    if decay:
        wga_ref, wa2_ref, ba2_ref, o_ref, la_ref, h_ref = refs[5:]
    else:
        o_ref, h_ref = refs[5:]

    @pl.when(pl.program_id(1) == 0)
    def _():
        h = _modulated_norm(x_ref[...], g_ref[...], sc_ref[0], sh_ref[0]).astype(BF16)
        h_ref[...] = h
        if decay:
            la_ref[...] = _log_decay(h, wga_ref, wa2_ref, ba2_ref)

    acc = jnp.dot(h_ref[...], w_ref[...], preferred_element_type=F32)
    if relu2:
        acc = jnp.square(jnp.maximum(acc, 0.0))
    o_ref[...] = acc.astype(o_ref.dtype)


def norm_matmul(x, g, scale, shift, w, *, rows_per_mod, tm, tn, relu2, out_dtype, decay_weights=None):
    t, d = x.shape
    n = w.shape[1]
    mod_map = lambda i, j: ((i * tm) // rows_per_mod, 0, 0)
    decay = decay_weights is not None
    in_specs = [pl.BlockSpec((tm, d), lambda i, j: (i, 0)),
                pl.BlockSpec((1, d), lambda i, j: (0, 0)),
                pl.BlockSpec((1, 1, d), mod_map),
                pl.BlockSpec((1, 1, d), mod_map),
                pl.BlockSpec((d, tn), lambda i, j: (0, j))]
    out_specs = pl.BlockSpec((tm, tn), lambda i, j: (i, j))
    out_shape = jax.ShapeDtypeStruct((t, n), out_dtype)
    args = [x, g, scale, shift, w]
    if decay:
        in_specs += [pl.BlockSpec(a.shape, lambda i, j: (0, 0)) for a in decay_weights]
        n_la = decay_weights[1].shape[1]
        out_specs = [out_specs, pl.BlockSpec((tm, n_la), lambda i, j: (i, 0))]
        out_shape = [out_shape, jax.ShapeDtypeStruct((t, n_la), F32)]
        args += list(decay_weights)
    return pl.pallas_call(
        functools.partial(_norm_matmul_kernel, relu2=relu2, decay=decay),
        grid=(t // tm, n // tn),
        in_specs=in_specs, out_specs=out_specs, out_shape=out_shape,
        scratch_shapes=[pltpu.VMEM((tm, d), BF16)],
        compiler_params=_cparams("parallel", "arbitrary"),
        name="norm_matmul_relu2" if relu2 else "norm_matmul",
    )(*args)


def _cumulative_log_decay(la, reverse):
    c = la.shape[0]
    row = lax.broadcasted_iota(jnp.int32, (c, c), 0)
    col = lax.broadcasted_iota(jnp.int32, (c, c), 1)
    tri = jnp.where((row <= col) if reverse else (row >= col), 1.0, 0.0).astype(BF16)
    hi = la.astype(BF16)
    rest = la - hi.astype(F32)
    mid = rest.astype(BF16)
    lo = (rest - mid.astype(F32)).astype(BF16)
    dot = lambda t: jnp.dot(tri, t, preferred_element_type=F32)
    return dot(hi) + dot(mid) + dot(lo)


def _gla_chunk_scores(q, k, v, b, *, reverse, exact):
    c = GLA_CHUNK
    n = GLA_SUB
    edge = b[0:1, :] if reverse else b[c - 1:c, :]
    qe = (q * jnp.exp(b)).astype(BF16)
    kd = (k * jnp.exp(edge - b)).astype(BF16)
    m = jnp.dot(v.T.astype(BF16), kd, preferred_element_type=F32)
    decay = jnp.exp(edge)

    lane = lax.broadcasted_iota(jnp.int32, (n, c), 1)
    sub_row = lax.broadcasted_iota(jnp.int32, (n, c), 0)
    blocks = []
    cap = 0.0 if exact else GLA_SAFE_EXP
    for i in range(c // n):
        lo, hi = i * n, (i + 1) * n
        q_i = q[lo:hi, :]
        b_i = b[lo:hi, :]
        has_border = (hi < c) if reverse else (lo > 0)
        if has_border:
            r = b[hi:hi + 1, :] if reverse else b[lo - 1:lo, :]
            lhs = (q_i * jnp.exp(b_i - r)).astype(BF16)
            rhs = (k * jnp.exp(jnp.minimum(r - b, cap))).astype(BF16)
        else:
            lhs = (q_i * jnp.exp(b_i)).astype(BF16)
            rhs = (k * jnp.exp(jnp.minimum(-b, cap))).astype(BF16)
        a_i = lax.dot_general(lhs, rhs, (((1,), (1,)), ((), ())), preferred_element_type=F32)
        if exact:
            a_diag = jnp.zeros((n, c), F32)
            for j in range(n):
                s = lo + j
                p = q_i * k[s:s + 1, :] * jnp.exp(b_i - b[s:s + 1, :])
                a_diag = jnp.where(lane == s, jnp.sum(p, axis=1, keepdims=True), a_diag)
            in_block = (lane >= lo) & (lane < hi)
            a_i = jnp.where(in_block, a_diag, a_i)
        causal = (lane >= sub_row + lo) if reverse else (lane <= sub_row + lo)
        blocks.append(jnp.where(causal, a_i, 0.0))
    a = jnp.concatenate(blocks, axis=0).astype(BF16)
    return qe, a, m, decay


def _gla_inter(qe, st):
    return lax.dot_general(qe, st.astype(BF16), (((1,), (1,)), ((), ())), preferred_element_type=F32)


def _gla_kernel(*refs, reverse, rope, has_s0, nchunk, heads):
    it = iter(refs)
    q_ref, k_ref, v_ref, la_ref = next(it), next(it), next(it), next(it)
    cos_ref = sin_ref = s0_ref = None
    if rope:
        cos_ref, sin_ref = next(it), next(it)
    if has_s0:
        s0_ref = next(it)
    o_ref, sfin_ref, st_ref = next(it), next(it), next(it)

    blk = pl.program_id(2)

    @pl.when(blk == 0)
    def _():
        for hh in range(heads):
            st_ref[hh] = s0_ref[0, 0, 0, hh].T if has_s0 else jnp.zeros(st_ref.shape[1:], F32)

    kcols = lambda hh: slice(hh * GLA_DK, (hh + 1) * GLA_DK)
    vcols = lambda hh: slice(hh * GLA_DV, (hh + 1) * GLA_DV)

    def load_qk(rows, hh):
        q = q_ref[rows, kcols(hh)] * (GLA_DK ** -0.5)
        k = k_ref[rows, kcols(hh)]
        if rope:
            cs, sn = cos_ref[rows, :], sin_ref[rows, :]
            q = q * cs + pltpu.roll(q, GLA_DK // 2, 1) * sn
            k = k * cs + pltpu.roll(k, GLA_DK // 2, 1) * sn
        return q, k

    def intra(a, rows, hh):
        return jnp.dot(a, v_ref[rows, vcols(hh)].astype(BF16), preferred_element_type=F32)

    def run_exact():
        def body(step, carry):
            ci = (nchunk - 1 - step) if reverse else step
            rows = pl.ds(pl.multiple_of(ci * GLA_CHUNK, GLA_CHUNK), GLA_CHUNK)
            for hh in range(heads):
                q, k = load_qk(rows, hh)
                b = _cumulative_log_decay(la_ref[rows, kcols(hh)], reverse)
                qe, a, m, decay = _gla_chunk_scores(q, k, v_ref[rows, vcols(hh)], b, reverse=reverse, exact=True)
                o_ref[rows, vcols(hh)] = _gla_inter(qe, st_ref[hh]) + intra(a, rows, hh)
                st_ref[hh] = st_ref[hh] * decay + m
            return carry

        lax.fori_loop(0, nchunk, body, 0)

    def run_fast():
        order = list(range(nchunk))[::-1] if reverse else list(range(nchunk))
        rows = [pl.ds(ci * GLA_CHUNK, GLA_CHUNK) for ci in range(nchunk)]
        units = [(hh, ci) for hh in range(heads) for ci in range(nchunk)]
        bs = {u: _cumulative_log_decay(la_ref[rows[u[1]], kcols(u[0])], reverse) for u in units}
        scores = {u: _gla_chunk_scores(*load_qk(rows[u[1]], u[0]), v_ref[rows[u[1]], vcols(u[0])], bs[u],
                                       reverse=reverse, exact=False) for u in units}
        o_intra = {u: intra(scores[u][1], rows[u[1]], u[0]) for u in units}
        states = {}
        for hh in range(heads):
            st = st_ref[hh]
            for ci in order:
                states[hh, ci] = st
                st = st * scores[hh, ci][3] + scores[hh, ci][2]
            st_ref[hh] = st
        for hh, ci in units:
            o_ref[rows[ci], vcols(hh)] = _gla_inter(scores[hh, ci][0], states[hh, ci]) + o_intra[hh, ci]

    steep = jnp.min(la_ref[...]) < -(GLA_SAFE_EXP / GLA_SUB)
    pl.when(steep)(run_exact)
    pl.when(jnp.logical_not(steep))(run_fast)

    @pl.when(blk == pl.num_programs(2) - 1)
    def _():
        for hh in range(heads):
            sfin_ref[0, hh] = st_ref[hh].T


def gla_direction(proj, la, rope_tabs, s0, *, batch, seq, reverse, layer, direction, rows_blk):
    t = proj.shape[0]
    nblk = seq // rows_blk
    nchunk = rows_blk // GLA_CHUNK
    rope = rope_tabs is not None
    has_s0 = s0 is not None

    def rb(b, c):
        cc = (nblk - 1 - c) if reverse else c
        return b * nblk + cc

    hp = GLA_HEADS_PER_STEP
    kw, vw = hp * GLA_DK, hp * GLA_DV
    assert OFF_GQ % kw == 0 and OFF_GK % kw == 0 and OFF_GV % vw == 0 and KEY_W % kw == 0
    kq, kk, kv = OFF_GQ // kw, OFF_GK // kw, OFF_GV // vw
    kla = direction * (KEY_W // kw)
    in_specs = [pl.BlockSpec((rows_blk, kw), lambda b, h, c: (rb(b, c), kq + h)),
                pl.BlockSpec((rows_blk, kw), lambda b, h, c: (rb(b, c), kk + h)),
                pl.BlockSpec((rows_blk, vw), lambda b, h, c: (rb(b, c), kv + h)),
                pl.BlockSpec((rows_blk, kw), lambda b, h, c: (rb(b, c), kla + h))]
    args = [proj, proj, proj, la]
    if rope:
        tab_map = lambda b, h, c: ((nblk - 1 - c) if reverse else c, 0)
        in_specs += [pl.BlockSpec((rows_blk, GLA_DK), tab_map)] * 2
        args += list(rope_tabs)
    if has_s0:
        in_specs.append(pl.BlockSpec((1, 1, 1, hp, GLA_DK, GLA_DV), lambda b, h, c: (b, layer, direction, h, 0, 0)))
        args.append(s0)
    o, sfin = pl.pallas_call(
        functools.partial(_gla_kernel, reverse=reverse, rope=rope, has_s0=has_s0, nchunk=nchunk, heads=hp),
        grid=(batch, GLA_HEADS // hp, nblk),
        in_specs=in_specs,
        out_specs=[pl.BlockSpec((rows_blk, vw), lambda b, h, c: (rb(b, c), h)),
                   pl.BlockSpec((1, hp, GLA_DK, GLA_DV), lambda b, h, c: (b, h, 0, 0))],
        out_shape=[jax.ShapeDtypeStruct((t, VAL_W), F32),
                   jax.ShapeDtypeStruct((batch, GLA_HEADS, GLA_DK, GLA_DV), F32)],
        scratch_shapes=[pltpu.VMEM((hp, GLA_DV, GLA_DK), F32)],
        compiler_params=_cparams("parallel", "parallel", "arbitrary"),
        name="gla_bwd" if reverse else "gla_fwd",
    )(*args)
    return o, sfin


def _head_norm(x, g):
    return x * lax.rsqrt(jnp.mean(x * x, axis=-1, keepdims=True) + NORM_EPS) * g


def _head_norm_mxu(x, g):
    n = x.shape[1]
    mean_sq = jnp.dot((x * x).astype(BF16), jnp.full((n, n), 1.0 / n, BF16), preferred_element_type=F32)
    return x * lax.rsqrt(mean_sq + NORM_EPS) * g


def _ctx_attn_kernel(q_ref, k_ref, v_ref, gq_ref, gk_ref, o_ref, kn_ref, vn_ref):
    scale = NA_HD ** -0.5
    for h in range(NA_HEADS):
        cols = slice(h * NA_HD, (h + 1) * NA_HD)
        qn = _head_norm(q_ref[:, cols], gq_ref[...]) * scale
        kn = _head_norm(k_ref[:, cols], gk_ref[...])
        v = v_ref[:, cols]
        kn_ref[0, h] = kn
        vn_ref[0, h] = v
        s = lax.dot_general(qn.astype(BF16), kn.astype(BF16), (((1,), (1,)), ((), ())),
                            preferred_element_type=F32)
        e = jnp.exp(s - jnp.max(s, axis=-1, keepdims=True))
        p = e / jnp.sum(e, axis=-1, keepdims=True)
        o_ref[:, cols] = jnp.dot(p.astype(BF16), v.astype(BF16), preferred_element_type=F32)


def ctx_attention(proj, gq, gk, *, batch, seq):
    t = proj.shape[0]
    cq, ck, cv = OFF_NQ // NA_W, OFF_NK // NA_W, OFF_NV // NA_W
    cache_shape = jax.ShapeDtypeStruct((batch, NA_HEADS, seq, NA_HD), F32)
    cache_spec = pl.BlockSpec((1, NA_HEADS, seq, NA_HD), lambda b: (b, 0, 0, 0))
    return pl.pallas_call(
        _ctx_attn_kernel,
        grid=(batch,),
        in_specs=[pl.BlockSpec((seq, NA_W), lambda b: (b, cq)),
                  pl.BlockSpec((seq, NA_W), lambda b: (b, ck)),
                  pl.BlockSpec((seq, NA_W), lambda b: (b, cv)),
                  pl.BlockSpec((1, NA_HD), lambda b: (0, 0)),
                  pl.BlockSpec((1, NA_HD), lambda b: (0, 0))],
        out_specs=[pl.BlockSpec((seq, NA_W), lambda b: (b, 0)), cache_spec, cache_spec],
        out_shape=[jax.ShapeDtypeStruct((t, NA_W), F32), cache_shape, cache_shape],
        compiler_params=_cparams("parallel"),
        name="ctx_attention",
    )(proj, proj, proj, gq, gk)


def _na_kernel(q_ref, k_ref, v_ref, kc_ref, vc_ref, gq_ref, gk_ref, tab_ref, o_ref, qn_s, kn_s, v_s, *, rows):
    scale = NA_HD ** -0.5
    qn_s[...] = (_head_norm_mxu(q_ref[...], gq_ref[...]) * scale).astype(BF16)
    kn_s[...] = _head_norm_mxu(k_ref[...], gk_ref[...]).astype(BF16)
    v_s[...] = v_ref[...].astype(BF16)
    kctx = kc_ref[0, 0, 0].astype(BF16)
    vctx = vc_ref[0, 0, 0].astype(BF16)
    nblk = rows // NA_QROWS
    qlen, win = NA_QROWS * GRID_W, NA_UNION * GRID_W

    def scores(g):
        u0 = jnp.clip(g * NA_QROWS - NA_KH // 2, 0, rows - NA_UNION)
        cls = jnp.where(g == 0, 0, jnp.where(g == nblk - 1, 2, 1))
        qrows = pl.ds(pl.multiple_of(g * qlen, qlen), qlen)
        wrows = pl.ds(pl.multiple_of(u0 * GRID_W, GRID_W), win)
        qn = qn_s[qrows, :]
        s_loc = lax.dot_general(qn, kn_s[wrows, :], (((1,), (1,)), ((), ())), preferred_element_type=F32)
        s_ctx = lax.dot_general(qn, kctx, (((1,), (1,)), ((), ())), preferred_element_type=F32)
        return qrows, wrows, s_loc + tab_ref[0, cls], s_ctx

    def softmax(s_loc, s_ctx):
        m = jnp.maximum(jnp.max(s_loc, axis=-1, keepdims=True), jnp.max(s_ctx, axis=-1, keepdims=True))
        e_loc = jnp.exp(s_loc - m)
        e_ctx = jnp.exp(s_ctx - m)
        den = jnp.sum(e_loc, axis=-1, keepdims=True) + jnp.sum(e_ctx, axis=-1, keepdims=True)
        return e_loc.astype(BF16), e_ctx.astype(BF16), den

    def body(step, carry):
        blocks = [scores(step * NA_BLOCK_UNROLL + u) for u in range(NA_BLOCK_UNROLL)]
        probs = [softmax(s_loc, s_ctx) for _, _, s_loc, s_ctx in blocks]
        for (qrows, wrows, _, _), (e_loc, e_ctx, den) in zip(blocks, probs):
            acc = jnp.dot(e_loc, v_s[wrows, :], preferred_element_type=F32)
            acc = acc + jnp.dot(e_ctx, vctx, preferred_element_type=F32)
            o_ref[qrows, :] = acc / den
        return carry

    lax.fori_loop(0, nblk // NA_BLOCK_UNROLL, body, 0)


def neighbourhood_attention(proj, cache_k, cache_v, gq, gk, table, *, batch, seq, layer, row0):
    rows = seq // GRID_W
    assert rows % NA_QROWS == 0 and rows // NA_QROWS >= 3 and rows >= NA_UNION
    past = cache_k.shape[3]
    cq, ck, cv = OFF_NQ // NA_HD, OFF_NK // NA_HD, OFF_NV // NA_HD
    cache_spec = pl.BlockSpec((1, 1, 1, past, NA_HD), lambda b, h: (b, layer, h, 0, 0))
    return pl.pallas_call(
        functools.partial(_na_kernel, rows=rows),
        grid=(batch, NA_HEADS),
        in_specs=[pl.BlockSpec((seq, NA_HD), lambda b, h: (row0 + b, cq + h)),
                  pl.BlockSpec((seq, NA_HD), lambda b, h: (row0 + b, ck + h)),
                  pl.BlockSpec((seq, NA_HD), lambda b, h: (row0 + b, cv + h)),
                  cache_spec, cache_spec,
                  pl.BlockSpec((1, NA_HD), lambda b, h: (0, 0)),
                  pl.BlockSpec((1, NA_HD), lambda b, h: (0, 0)),
                  pl.BlockSpec((1, 3, NA_QROWS * GRID_W, NA_UNION * GRID_W), lambda b, h: (h, 0, 0, 0))],
        out_specs=pl.BlockSpec((seq, NA_HD), lambda b, h: (b, h)),
        out_shape=jax.ShapeDtypeStruct((batch * seq, NA_W), F32),
        scratch_shapes=[pltpu.VMEM((seq, NA_HD), BF16)] * 3,
        compiler_params=_cparams("parallel", "parallel"),
        name="neighbourhood_attention",
    )(proj, proj, proj, cache_k, cache_v, gq, gk, table)


def na_bias_table(rpb, rows):
    nblk = rows // NA_QROWS
    col = np.arange(GRID_W)
    c_start = np.clip(col - NA_KW // 2, 0, GRID_W - NA_KW)
    col_in = (col[None, :] >= c_start[:, None]) & (col[None, :] < c_start[:, None] + NA_KW)
    col_idx = np.clip(col[None, :] - col[:, None] + NA_KW - 1, 0, 2 * NA_KW - 2)
    row_off, row_in = [], []
    for g in (0, 1, nblk - 1):
        u0 = np.clip(g * NA_QROWS - NA_KH // 2, 0, rows - NA_UNION)
        r = g * NA_QROWS + np.arange(NA_QROWS)[:, None]
        r_start = np.clip(r - NA_KH // 2, 0, rows - NA_KH)
        key_row = u0 + np.arange(NA_UNION)[None, :]
        row_in.append((key_row >= r_start) & (key_row < r_start + NA_KH))
        row_off.append(np.clip(key_row - r + NA_KH - 1, 0, 2 * NA_KH - 2))
    row_off, row_in = np.stack(row_off), np.stack(row_in)
    n_off = 2 * NA_KH - 1
    tiles = jnp.where(col_in[None, None], rpb[:, :, col_idx], -jnp.inf)
    tiles = jnp.concatenate([tiles, jnp.full_like(tiles[:, :1], -jnp.inf)], axis=1)
    tab = tiles[:, np.where(row_in, row_off, n_off)]
    tab = jnp.transpose(tab, (0, 1, 2, 4, 3, 5))
    return tab.reshape(rpb.shape[0], 3, NA_QROWS * GRID_W, NA_UNION * GRID_W).astype(F32)


def _filter_kernel(z_ref, f1w_ref, f1b_ref, f2w_ref, f2b_ref, f3w_ref, fr_ref, dl_ref, o_ref, *, seq):
    fr = fr_ref[...]
    h = jnp.sin(fr * (jnp.dot(z_ref[...], f1w_ref[...], preferred_element_type=F32, precision=HIGHEST)
                      + f1b_ref[...]))
    h = jnp.sin(fr * (jnp.dot(h, f2w_ref[...], preferred_element_type=F32, precision=HIGHEST)
                      + f2b_ref[...]))
    h = jnp.dot(h, f3w_ref[...], preferred_element_type=F32, precision=HIGHEST)
    t = lax.broadcasted_iota(jnp.int32, (seq, 1), 0).astype(F32)
    dist = jnp.abs(t - float(seq // 2)) * (2.0 / seq)
    h = h * jnp.exp(-dist * dl_ref[...])
    o_ref[...] = h * lax.rsqrt(jnp.sum(h * h, axis=0, keepdims=True) + NORM_EPS)


def hyena_filters(zpos, f1w, f1b, f2w, f2b, f3w, freq, deltas, *, seq):
    n = f3w.shape[1]
    tc = 512
    full = lambda a: pl.BlockSpec(a.shape, lambda j: (0,) * a.ndim)
    return pl.pallas_call(
        functools.partial(_filter_kernel, seq=seq),
        grid=(n // tc,),
        in_specs=[full(zpos), full(f1w), full(f1b), full(f2w), full(f2b),
                  pl.BlockSpec((HY_HIDDEN, tc), lambda j: (0, j)), full(freq),
                  pl.BlockSpec((1, tc), lambda j: (0, j))],
        out_specs=pl.BlockSpec((seq, tc), lambda j: (0, j)),
        out_shape=jax.ShapeDtypeStruct((seq, n), F32),
        compiler_params=_cparams("parallel"),
        name="hyena_filters",
    )(zpos, f1w, f1b, f2w, f2b, f3w, freq, deltas)


def _lane_tile(t, width):
    return jnp.concatenate([t] * (width // LANES), axis=1) if width > LANES else t


def _cdft(vals, sign):
    if len(vals) == 1:
        return vals
    if len(vals) == 2:
        (ar, ai), (br, bi) = vals
        return [(ar + br, ai + bi), (ar - br, ai - bi)]
    if len(vals) == 8:
        ev, od = _cdft(vals[0::2], sign), _cdft(vals[1::2], sign)
        h = math.sqrt(0.5)
        (o0r, o0i), (o1r, o1i), (o2r, o2i), (o3r, o3i) = od
        tw = [(o0r, o0i),
              ((o1r - sign * o1i) * h, (o1i + sign * o1r) * h),
              (-sign * o2i, sign * o2r),
              ((-o3r - sign * o3i) * h, (sign * o3r - o3i) * h)]
        return ([(er + tr, ei + ti) for (er, ei), (tr, ti) in zip(ev, tw)]
                + [(er - tr, ei - ti) for (er, ei), (tr, ti) in zip(ev, tw)])
    assert len(vals) == 4
    (x0r, x0i), (x1r, x1i), (x2r, x2i), (x3r, x3i) = vals
    a0r, a0i, a1r, a1i = x0r + x2r, x0i + x2i, x0r - x2r, x0i - x2i
    a2r, a2i, a3r, a3i = x1r + x3r, x1i + x3i, x1r - x3r, x1i - x3i
    wr, wi = -sign * a3i, sign * a3r
    return [(a0r + a2r, a0i + a2i), (a1r + wr, a1i + wi), (a0r - a2r, a0i - a2i), (a1r - wr, a1i - wi)]


def _fwd_spectrum(z_list, f_ref, rows):
    lp = f_ref.shape[2]
    xs = []
    for r, z in enumerate(z_list):
        xc = jnp.dot(f_ref[r, rows, :], z, preferred_element_type=F32)
        xn = jnp.dot(f_ref[r, pl.ds(lp + rows.start, rows.size), :], z, preferred_element_type=F32)
        xs.append((xc, -xn))
    return _cdft(xs, -1)


def _spectrum_kernel(*refs, radix, seq, kc, tc):
    h_refs, (f_ref, sh_ref, o_ref) = refs[:radix], refs[radix:]
    lp = seq // radix
    hs = [h[...].astype(BF16) for h in h_refs]
    for c0 in range(0, lp, kc):
        rows = pl.ds(c0, kc)
        cs = _lane_tile(sh_ref[0, rows, :], tc) * (1.0 / seq)
        sn = _lane_tile(sh_ref[1, rows, :], tc) * (1.0 / seq)
        for q, (re, im) in enumerate(_fwd_spectrum(hs, f_ref, rows)):
            o_ref[0, q, rows, :] = re * cs - im * sn
            o_ref[1, q, rows, :] = re * sn + im * cs


def _whole(a):
    return pl.BlockSpec(a.shape, lambda *_: (0,) * a.ndim, pipeline_mode=pl.Buffered(1))


def filter_spectrum(filt, tabs, *, seq):
    n = filt.shape[1]
    radix, tc = _hy_radix(seq), _hy_fft_tc(seq)
    lp = seq // radix
    view = filt.reshape(lp, radix * n)
    f, _, sh = tabs
    groups = [pl.BlockSpec((lp, tc), functools.partial(lambda j, r: (0, r * (n // tc) + j), r=r)) for r in range(radix)]
    return pl.pallas_call(
        functools.partial(_spectrum_kernel, radix=radix, seq=seq, kc=min(HY_KC, lp), tc=tc),
        grid=(n // tc,),
        in_specs=groups + [_whole(f), _whole(sh)],
        out_specs=pl.BlockSpec((2, radix, lp, tc), lambda j: (0, 0, 0, j)),
        out_shape=jax.ShapeDtypeStruct((2, radix, lp, n), F32),
        compiler_params=_cparams("parallel"),
        name="filter_spectrum",
    )(*([view] * radix), f, sh)


def _short_conv(u, w, bias, seq):
    t = lax.broadcasted_iota(jnp.int32, (seq, 1), 0)
    prev = jnp.where(t == 0, 0.0, pltpu.roll(u, 1, 0))
    nxt = jnp.where(t == seq - 1, 0.0, pltpu.roll(u, seq - 1, 0))
    return bias + prev * w[0:1, :] + u * w[1:2, :] + nxt * w[2:3, :]


def _hy_radix(seq):
    return HY_RADIX if seq >= HY_RADIX_MIN_SEQ else 1


def _split_rows(val, out_refs, tmp_ref):
    radix = len(out_refs)
    if radix == 1:
        out_refs[0][...] = val.astype(BF16)
        return
    tmp_ref[...] = val
    for r in range(radix):
        out_refs[r][...] = tmp_ref[pl.ds(r, val.shape[0] // radix, stride=radix), :].astype(BF16)


def _merge_rows(in_refs, tmp_ref):
    radix = len(in_refs)
    if radix == 1:
        return in_refs[0][...]
    for r in range(radix):
        tmp_ref[pl.ds(r, in_refs[r].shape[0], stride=radix), :] = in_refs[r][...]
    return tmp_ref[...]


def _hy_pre_kernel(*refs, seq, radix):
    (u_ref, w_ref, b_ref), z_refs, tmp = refs[:3], refs[3:3 + radix], refs[3 + radix:]
    z = _short_conv(u_ref[...], w_ref[...], b_ref[...], seq)
    _split_rows(z, z_refs, tmp[0] if tmp else None)


def _hy_gate_kernel(*refs, seq, radix):
    ux_ref, uz_ref, wx_ref, bx_ref, wz_ref, bz_ref = refs[:6]
    cv_refs, hb_ref, zo_ref = refs[6:6 + radix], refs[6 + radix], refs[7 + radix]
    zb_refs, tmp = refs[8 + radix:8 + 2 * radix], refs[8 + 2 * radix:]
    tmp_ref = tmp[0] if tmp else None
    x = _short_conv(ux_ref[...], wx_ref[...], bx_ref[...], seq)
    z = _short_conv(uz_ref[...], wz_ref[...], bz_ref[...], seq)
    out = x * (_merge_rows(cv_refs, tmp_ref) + hb_ref[...] * z)
    zo_ref[...] = out
    _split_rows(out, zb_refs, tmp_ref)


def _hy_final_kernel(*refs, seq, radix):
    (ux_ref, wx_ref, bx_ref), cv_refs = refs[:3], refs[3:3 + radix]
    hb_ref, z_ref, zb_ref = refs[3 + radix:6 + radix]
    tmp = refs[6 + radix:]
    x = _short_conv(ux_ref[...], wx_ref[...], bx_ref[...], seq)
    cv = _merge_rows(cv_refs, tmp[0] if tmp else None)
    zb_ref[...] = (x * (cv + hb_ref[...] * z_ref[...])).astype(BF16)


HY_ELEM_SLAB_BYTES = 2 * 1024 * 1024


def _hy_tc(seq):
    if _hy_radix(seq) > 1:
        return LANES
    return max(LANES, min(HY_W, HY_ELEM_SLAB_BYTES // (4 * seq)))


def _hy_specs(seq, row0, col0):
    tc = _hy_tc(seq)
    cb, wb = col0 // tc, (col0 - OFF_HY) // tc
    u = pl.BlockSpec((seq, tc), lambda b, j: (row0 + b, cb + j))
    w = pl.BlockSpec((3, tc), lambda b, j: (0, wb + j))
    bb = pl.BlockSpec((1, tc), lambda b, j: (0, wb + j))
    act = pl.BlockSpec((seq, tc), lambda b, j: (b, j))
    hb = pl.BlockSpec((1, tc), lambda b, j: (0, j))
    part = pl.BlockSpec((seq // _hy_radix(seq), tc), lambda b, j: (b, j))
    return u, w, bb, act, hb, part


def _hy_elementwise_call(kernel, name, seq, batch, in_specs, out_specs, out_shape, args):
    radix, tc = _hy_radix(seq), _hy_tc(seq)
    assert radix == 1 or tc == LANES
    return pl.pallas_call(
        functools.partial(kernel, seq=seq, radix=radix),
        grid=(batch, HY_W // tc),
        in_specs=in_specs, out_specs=out_specs, out_shape=out_shape,
        scratch_shapes=[pltpu.VMEM((seq, tc), F32)] if radix > 1 else [],
        compiler_params=_cparams("parallel", "parallel"), name=name,
    )(*args)


def hyena_pre(proj, conv_w, conv_b, *, batch, seq, row0):
    radix = _hy_radix(seq)
    u, w, bb, _, _, part = _hy_specs(seq, row0, OFF_HY + HY_ORDER * HY_W)
    parts = [jax.ShapeDtypeStruct((batch * seq // radix, HY_W), BF16)] * radix
    return _hy_elementwise_call(_hy_pre_kernel, "hyena_pre", seq, batch, [u, w, bb], [part] * radix, parts,
                                (proj, conv_w, conv_b))


def hyena_gate(proj, conv_w, conv_b, conv, hy_bias, *, batch, seq, row0):
    radix = _hy_radix(seq)
    ux, wx, bx, act, hb, part = _hy_specs(seq, row0, OFF_HY)
    uz, wz, bz, _, _, _ = _hy_specs(seq, row0, OFF_HY + HY_ORDER * HY_W)
    parts = [jax.ShapeDtypeStruct((batch * seq // radix, HY_W), BF16)] * radix
    outs = _hy_elementwise_call(
        _hy_gate_kernel, "hyena_gate", seq, batch,
        [ux, uz, wx, bx, wz, bz] + [part] * radix + [hb], [act] + [part] * radix,
        [jax.ShapeDtypeStruct((batch * seq, HY_W), F32)] + parts,
        (proj, proj, conv_w, conv_b, conv_w, conv_b, *conv, hy_bias[0:1]))
    return outs[0], outs[1:]


def hyena_final(proj, conv_w, conv_b, conv, hy_bias, z1, *, batch, seq, row0):
    radix = _hy_radix(seq)
    ux, wx, bx, act, hb, part = _hy_specs(seq, row0, OFF_HY + HY_W)
    return _hy_elementwise_call(
        _hy_final_kernel, "hyena_final", seq, batch, [ux, wx, bx] + [part] * radix + [hb, act], act,
        jax.ShapeDtypeStruct((batch * seq, HY_W), BF16), (proj, conv_w, conv_b, *conv, hy_bias[1:2], z1))


def _long_conv_kernel(*refs, radix, seq, kc, tc):
    z_refs, (f_ref, ft_ref, g_ref) = refs[:radix], refs[radix:radix + 3]
    o_refs, u_s = refs[radix + 3:2 * radix + 3], refs[2 * radix + 3]
    lp = seq // radix
    zs = [z[...] for z in z_refs]
    for c0 in range(0, lp, kc):
        rows = pl.ds(c0, kc)
        ys = []
        for q, (zr, zi) in enumerate(_fwd_spectrum(zs, f_ref, rows)):
            gr, gi = g_ref[0, q, rows, :], g_ref[1, q, rows, :]
            ys.append((zr * gr - zi * gi, zr * gi + zi * gr))
        for r, (re, im) in enumerate(_cdft(ys, +1)):
            u_s[r, rows, :] = re.astype(BF16)
            u_s[r, pl.ds(lp + c0, kc), :] = (-im).astype(BF16)
    for r in range(radix):
        o_refs[r][...] = jnp.dot(ft_ref[r], u_s[r], preferred_element_type=F32)


def _hy_fft_tc(seq):
    return HY_TC_FFT if _hy_radix(seq) > 1 else 2 * HY_TC_FFT


def hyena_long_conv(z_parts, g, tabs, *, batch, seq, order):
    c = z_parts[0].shape[1]
    radix, tc = _hy_radix(seq), _hy_fft_tc(seq)
    lp = seq // radix
    f, ft, _ = tabs
    ncb = c // tc
    goff = order * ncb
    part = pl.BlockSpec((lp, tc), lambda j, b: (b, j))
    return pl.pallas_call(
        functools.partial(_long_conv_kernel, radix=radix, seq=seq, kc=min(HY_KC, lp), tc=tc),
        grid=(ncb, batch),
        in_specs=[part] * radix + [_whole(f), _whole(ft),
                                   pl.BlockSpec((2, radix, lp, tc), lambda j, b: (0, 0, 0, goff + j),
                                                pipeline_mode=pl.Buffered(1))],
        out_specs=[part] * radix,
        out_shape=[jax.ShapeDtypeStruct((batch * lp, c), F32)] * radix,
        scratch_shapes=[pltpu.VMEM((radix, 2 * lp, tc), BF16)],
        compiler_params=_cparams("parallel", "parallel"),
        name="hyena_long_conv",
    )(*z_parts, f, ft, g)


def dft_tables(seq):
    radix = _hy_radix(seq)
    lp = seq // radix
    odd = 2 * jnp.arange(lp, dtype=jnp.int32) + 1

    def cs(num, den):
        ang = (num % (2 * den)).astype(F32) * (math.pi / den)
        return jnp.cos(ang), jnp.sin(ang)

    ca, sa = cs(odd[:, None] * jnp.arange(lp, dtype=jnp.int32)[None, :], 2 * lp)
    cb, sb = cs(odd[None, :] * jnp.arange(radix, dtype=jnp.int32)[:, None], 2 * seq)
    fc = ca[None] * cb[:, :, None] - sa[None] * sb[:, :, None]
    fs = sa[None] * cb[:, :, None] + ca[None] * sb[:, :, None]
    f = jnp.concatenate([fc, fs], axis=1).astype(BF16)
    sc, ss = cs(odd, 4)
    sh = jnp.broadcast_to(jnp.stack([sc, ss], axis=0)[..., None], (2, lp, LANES))
    return f, jnp.transpose(f, (0, 2, 1)), sh


def position_features(seq):
    t = jnp.arange(seq, dtype=F32)
    tn = t / seq
    bands = jnp.arange(1, HY_BANDS + 1, dtype=F32)
    ang = 2.0 * jnp.pi * tn[:, None] * bands[None, :]
    z = jnp.concatenate([tn[:, None], jnp.cos(ang), jnp.sin(ang)], axis=-1)
    return jnp.pad(z, ((0, 0), (0, LANES - z.shape[1])))


def rope_tables(seq):
    n_freq = GLA_DK // 4
    pos = jnp.arange(seq)
    row = (pos // GRID_W).astype(F32)
    col = (pos % GRID_W).astype(F32)
    inv = ROPE_BASE ** (-jnp.arange(n_freq, dtype=F32) / n_freq)
    ang = jnp.concatenate([row[:, None] * inv, col[:, None] * inv], axis=-1)
    cos, sin = jnp.cos(ang), jnp.sin(ang)
    return jnp.concatenate([cos, cos], axis=-1), jnp.concatenate([-sin, sin], axis=-1)


def _merge_kernel(of_ref, ob_ref, gg_ref, gn_ref, nb_ref, hc_ref, ga_ref, gb_ref, gc_ref,
                  wa_ref, wb_ref, wc_ref, o_ref, a_s):
    @pl.when(pl.program_id(1) == 0)
    def _():
        for h in range(GLA_HEADS):
            cols = slice(h * GLA_DV, (h + 1) * GLA_DV)
            oa = of_ref[:, cols] + ob_ref[:, cols]
            oa = _head_norm(oa, gn_ref[...])
            gg = gg_ref[:, cols]
            a_s[:, cols] = (oa * (gg * jax.nn.sigmoid(gg))).astype(BF16)

    ya = jnp.dot(a_s[...], wa_ref[...], preferred_element_type=F32)
    yb = jnp.dot(nb_ref[...].astype(BF16), wb_ref[...], preferred_element_type=F32)
    yc = jnp.dot(hc_ref[...], wc_ref[...], preferred_element_type=F32)
    merged = (jax.nn.sigmoid(ga_ref[...]) * ya + jax.nn.sigmoid(gb_ref[...]) * yb
              + jax.nn.sigmoid(gc_ref[...]) * yc)
    o_ref[...] = merged.astype(o_ref.dtype)


def branch_merge(o_f, o_b, proj, gnorm, o_attn, o_hy, w_a, w_b, w_c, *, row0_blocks, tm, tn):
    t = o_f.shape[0]
    d = w_a.shape[1]
    assert OFF_GATES % tn == 0 and d % tn == 0
    act = lambda w: pl.BlockSpec((tm, w), lambda i, j: (i, 0))
    gate = lambda g: pl.BlockSpec((tm, tn), lambda i, j: (row0_blocks + i, (OFF_GATES + g * d) // tn + j))
    wspec = lambda w: pl.BlockSpec((w.shape[0], tn), lambda i, j: (0, j))
    return pl.pallas_call(
        _merge_kernel,
        grid=(t // tm, d // tn),
        in_specs=[act(VAL_W), act(VAL_W),
                  pl.BlockSpec((tm, VAL_W), lambda i, j: (row0_blocks + i, OFF_GG // VAL_W)),
                  pl.BlockSpec((1, GLA_DV), lambda i, j: (0, 0)),
                  act(NA_W), act(HY_W), gate(0), gate(1), gate(2),
                  wspec(w_a), wspec(w_b), wspec(w_c)],
        out_specs=pl.BlockSpec((tm, tn), lambda i, j: (i, j)),
        out_shape=jax.ShapeDtypeStruct((t, d), BF16),
        scratch_shapes=[pltpu.VMEM((tm, VAL_W), BF16)],
        compiler_params=_cparams("parallel", "arbitrary"),
        name="branch_merge",
    )(o_f, o_b, proj, gnorm, o_attn, o_hy, proj, proj, proj, w_a, w_b, w_c)


def _matmul_residual_kernel(a_ref, w_ref, g_ref, r_ref, o_ref):
    y = jnp.dot(a_ref[...], w_ref[...], preferred_element_type=F32)
    o_ref[...] = r_ref[...] + g_ref[0] * y


def matmul_residual(a, w, gate, resid, *, rows_per_mod, tm, tn):
    t, k = a.shape
    n = w.shape[1]
    return pl.pallas_call(
        _matmul_residual_kernel,
        grid=(t // tm, n // tn),
        in_specs=[pl.BlockSpec((tm, k), lambda i, j: (i, 0)),
                  pl.BlockSpec((k, tn), lambda i, j: (0, j)),
                  pl.BlockSpec((1, 1, tn), lambda i, j: ((i * tm) // rows_per_mod, 0, j)),
                  pl.BlockSpec((tm, tn), lambda i, j: (i, j))],
        out_specs=pl.BlockSpec((tm, tn), lambda i, j: (i, j)),
        out_shape=jax.ShapeDtypeStruct((t, n), F32),
        compiler_params=_cparams("parallel", "arbitrary"),
        name="matmul_residual",
    )(a, w, gate, resid)


def _trunk_layer(x, mods, p, consts, *, batch, seq, layer, latent):
    sh1, sc1, g1, sh2, sc2, g2 = mods
    t = x.shape[0]
    rows_per_mod = t // sh1.shape[0]
    tm = 512

    proj, la = norm_matmul(x, p["norm1_g"], sc1, sh1, p["w_in_main"], rows_per_mod=rows_per_mod,
                           tm=tm, tn=1536, relu2=False, out_dtype=F32,
                           decay_weights=(p["w_in_ga"], p["wa2_bd"], p["ba2"]))

    rows_blk = min(seq, 512)
    rope = consts["rope"] if latent else None
    s0 = consts["state_gla"] if latent else None
    o_f, s_f = gla_direction(proj, la, rope, s0, batch=batch, seq=seq, reverse=False, layer=layer,
                             direction=0, rows_blk=rows_blk)
    o_b, s_b = gla_direction(proj, la, rope, s0, batch=batch, seq=seq, reverse=True, layer=layer,
                             direction=1, rows_blk=rows_blk)

    if latent:
        o_attn = neighbourhood_attention(proj, consts["cache_k"], consts["cache_v"], p["na_qnorm_g"],
                                         p["na_knorm_g"], p["na_table"], batch=batch, seq=seq,
                                         layer=layer, row0=0)
        new_ctx = None
    else:
        o_attn, kn, vn = ctx_attention(proj, p["na_qnorm_g"], p["na_knorm_g"], batch=batch, seq=seq)
        new_ctx = (jnp.stack([s_f, s_b], axis=1), kn, vn)

    key = "lat" if latent else "ctx"
    tabs = consts["dft_" + key]
    g_spec = p["spectrum_" + key]
    z0 = hyena_pre(proj, p["hy_conv_w"], p["hy_conv_b"], batch=batch, seq=seq, row0=0)
    cv0 = hyena_long_conv(z0, g_spec, tabs, batch=batch, seq=seq, order=0)
    z1, z1b = hyena_gate(proj, p["hy_conv_w"], p["hy_conv_b"], cv0, p["hy_bias"], batch=batch, seq=seq, row0=0)
    cv1 = hyena_long_conv(z1b, g_spec, tabs, batch=batch, seq=seq, order=1)
    o_hy = hyena_final(proj, p["hy_conv_w"], p["hy_conv_b"], cv1, p["hy_bias"], z1, batch=batch, seq=seq, row0=0)

    merged = branch_merge(o_f, o_b, proj, p["gla_norm_g"], o_attn, o_hy, p["w_br_a"], p["w_br_b"],
                          p["w_br_c"], row0_blocks=0, tm=256, tn=2048)
    x = matmul_residual(merged, p["w_out"], g1, x, rows_per_mod=rows_per_mod, tm=tm, tn=2048)

    u = norm_matmul(x, p["norm2_g"], sc2, sh2, p["w_mlp1"], rows_per_mod=rows_per_mod,
                    tm=tm, tn=2048, relu2=True, out_dtype=BF16)
    x = matmul_residual(u, p["w_mlp2"], g2, x, rows_per_mod=rows_per_mod, tm=tm, tn=512)
    return x, new_ctx


def _reorder_w_in(w_in):
    o_ga = 2 * KEY_W + 2 * VAL_W
    n_ga = 2 * GLA_LOWRANK
    o_hy = o_ga + n_ga + 3 * NA_W
    o_gates = o_hy + (HY_ORDER + 1) * HY_W
    main = jnp.concatenate([w_in[..., :o_ga], w_in[..., o_ga + n_ga:o_hy], w_in[..., o_gates:],
                            w_in[..., o_hy:o_gates]], axis=-1).astype(BF16)
    ga = jnp.pad(w_in[..., o_ga:o_ga + n_ga], ((0, 0), (0, LANES - n_ga))).astype(BF16)
    return main, ga


def kernel(x_prompt, x_sample, state_gla, cache_na_k, cache_na_v, c, c_ctx, w_mod, b_mod, norm1_g, norm2_g, w_in, gla_wa2, gla_ba2, gla_norm_g, na_qnorm_g, na_knorm_g, na_rpb, hy_conv_w, hy_conv_b, hy_f1_w, hy_f1_b, hy_f2_w, hy_f2_b, hy_f3_w, hy_freq, hy_bias, w_br_a, w_br_b, w_br_c, w_out, w_mlp1, w_mlp2):
    batch, seq, d = x_prompt.shape
    dec_batch, dec_seq, _ = x_sample.shape
    depth = w_mod.shape[0]

    bf = lambda w: w.astype(BF16)
    wa2_bd = jnp.zeros((depth, LANES, 2 * KEY_W), F32)
    wa2_bd = wa2_bd.at[:, :GLA_LOWRANK, :KEY_W].set(gla_wa2[:, 0])
    wa2_bd = wa2_bd.at[:, GLA_LOWRANK:2 * GLA_LOWRANK, KEY_W:].set(gla_wa2[:, 1])
    ba2 = gla_ba2.reshape(depth, 1, 2 * KEY_W)
    f1w = jnp.pad(hy_f1_w, ((0, 0), (0, LANES - hy_f1_w.shape[1]), (0, 0)))

    consts = {
        "rope": rope_tables(dec_seq),
        "state_gla": state_gla, "cache_k": cache_na_k, "cache_v": cache_na_v,
        "dft_ctx": dft_tables(seq), "dft_lat": dft_tables(dec_seq),
    }
    deltas = jnp.abs(jnp.linspace(math.log(HY_DECAY_TARGET) / HY_DECAY_PCT_LONG,
                                  math.log(HY_DECAY_TARGET) / HY_DECAY_PCT_SHORT, HY_W, dtype=F32))
    deltas = jnp.tile(deltas, HY_ORDER)[None, :]
    zpos = {"ctx": position_features(seq), "lat": position_features(dec_seq)}

    cond = jnp.zeros((16, d), F32).at[0].set(c_ctx).at[1:1 + dec_batch].set(c)

    xp = x_prompt.reshape(batch * seq, d)
    xs = x_sample.reshape(dec_batch * dec_seq, d)
    new_gla, new_k, new_v = [], [], []
    for l in range(depth):
        m = adaln(cond, w_mod, b_mod[l][None, :], l)
        mods_ctx = tuple(m[0:1, i * d:(i + 1) * d][:, None, :] for i in range(N_MOD))
        mods_lat = tuple(m[1:1 + dec_batch, i * d:(i + 1) * d][:, None, :] for i in range(N_MOD))
        w_in_main, w_in_ga = _reorder_w_in(w_in[l])
        p = dict(norm1_g=norm1_g[l][None], norm2_g=norm2_g[l][None], w_in_main=w_in_main, w_in_ga=w_in_ga,
                 wa2_bd=bf(wa2_bd[l]), ba2=ba2[l], gla_norm_g=gla_norm_g[l][None], na_qnorm_g=na_qnorm_g[l][None],
                 na_knorm_g=na_knorm_g[l][None], na_table=na_bias_table(na_rpb[l], dec_seq // GRID_W),
                 hy_conv_w=hy_conv_w[l], hy_conv_b=hy_conv_b[l][None], hy_bias=hy_bias[l],
                 w_br_a=bf(w_br_a[l]), w_br_b=bf(w_br_b[l]), w_br_c=bf(w_br_c[l]), w_out=bf(w_out[l]),
                 w_mlp1=bf(w_mlp1[l]), w_mlp2=bf(w_mlp2[l]))
        for key, length in (("ctx", seq), ("lat", dec_seq)):
            filt = hyena_filters(zpos[key], f1w[l], hy_f1_b[l][None], hy_f2_w[l], hy_f2_b[l][None],
                                 hy_f3_w[l], hy_freq[l][None], deltas, seq=length)
            p["spectrum_" + key] = filter_spectrum(filt, consts["dft_" + key], seq=length)
        xp, (s_gla, k_ctx, v_ctx) = _trunk_layer(xp, mods_ctx, p, consts, batch=batch, seq=seq, layer=l, latent=False)
        new_gla.append(s_gla)
        new_k.append(k_ctx)
        new_v.append(v_ctx)
        xs, _ = _trunk_layer(xs, mods_lat, p, consts, batch=dec_batch, seq=dec_seq, layer=l, latent=True)
    return (xp.reshape(batch, seq, d), xs.reshape(dec_batch, dec_seq, d),
            jnp.stack(new_gla, axis=1), jnp.stack(new_k, axis=1), jnp.stack(new_v, axis=1))
```
